```python
import math
import jax, jax.numpy as jnp
from jax import lax
import numpy as np

D_MODEL = 2048
BATCH = 2
SEQ = 4096
DEPTH = 1
DEC_BATCH = 8
DEC_SEQ = 8
PAST_LEN = 16384
PAGE_SIZE = 128

N_HEADS = 8
DK = 64
DV = 2 * DK
W_ATTN = N_HEADS * DV
D_QK = 2 * N_HEADS * DK
Q_BLOCK = 128
W_SGU = D_MODEL // 2
N_SGU_GROUPS = 8
SGU_GROUP = W_SGU // N_SGU_GROUPS
CHUNK = 128
N_EXPERTS = 32
TOP_K = 4
D_FF = D_MODEL
SWIGLU_LIMIT = 7.0
SWIGLU_ALPHA = 1.702
MOE_BLOCK = 128
EPS = 1e-6
SPLITS = (W_SGU, 2 * W_SGU, 2 * W_SGU + D_QK, 2 * W_SGU + 2 * D_QK, 2 * W_SGU + 2 * D_QK + W_ATTN, 2 * W_SGU + 2 * D_QK + W_ATTN + D_MODEL)
D_IN = 2 * W_SGU + 2 * D_QK + W_ATTN + 2 * D_MODEL

kernel_name = "hybrid_sgu_diffattn_moe_step"


def rmsnorm(x, g):
    xf = x.astype(jnp.float32)
    y = xf * lax.rsqrt(jnp.mean(xf * xf, axis=-1, keepdims=True) + EPS)
    return (y * g.astype(jnp.float32)).astype(x.dtype)


def layernorm(x, g, b):
    xf = x.astype(jnp.float32)
    mu = jnp.mean(xf, axis=-1, keepdims=True)
    var = jnp.mean(jnp.square(xf - mu), axis=-1, keepdims=True)
    y = (xf - mu) * lax.rsqrt(var + EPS)
    return (y * g.astype(jnp.float32) + b.astype(jnp.float32)).astype(x.dtype)


def lambda_init(layer_idx):
    return 0.8 - 0.6 * math.exp(-0.3 * layer_idx)


def alibi_slopes():
    return jnp.exp2(-(8.0 / N_HEADS) * jnp.arange(1, N_HEADS + 1, dtype=jnp.float32))


def mixer_projections(x, norm_g, w_in, ln_g, ln_b):
    b, t = x.shape[0], x.shape[1]
    z = rmsnorm(x, norm_g) @ w_in
    z_u, z_v, z_q, z_k, z_va, z_ga, z_gb = jnp.split(z, SPLITS, axis=-1)
    u = jax.nn.gelu(z_u, approximate=False)
    v = layernorm(jax.nn.gelu(z_v, approximate=False), ln_g, ln_b)
    q = z_q.reshape(b, t, N_HEADS, 2, DK)
    k = z_k.reshape(b, t, N_HEADS, 2, DK)
    va = z_va.reshape(b, t, N_HEADS, DV)
    return u, v, q, k, va, z_ga, z_gb


def spatial_gating(u, v, w_s, b_s):
    b, t = v.shape[0], v.shape[1]
    n_chunks = -(-t // CHUNK)
    pad = n_chunks * CHUNK - t
    vp = jnp.pad(v, ((0, 0), (0, pad), (0, 0))).reshape(b, n_chunks, CHUNK, N_SGU_GROUPS, SGU_GROUP)
    causal = jnp.tril(jnp.ones((CHUNK, CHUNK), dtype=bool))
    w = jnp.where(causal[None], w_s, 0.0).astype(v.dtype)
    f = jnp.einsum('gij,bcjgw->bcigw', w, vp) + b_s.T.astype(v.dtype)[None, None, :, :, None]
    f = f.reshape(b, n_chunks * CHUNK, W_SGU)[:, :t]
    return u * f


def diff_attn_core(q, k, v, q_pos, k_pos, lam, lam_init, subln_g):
    s = jnp.einsum('bqhcd,bkhcd->bhcqk', q, k, preferred_element_type=jnp.float32) * (DK ** -0.5)
    dist = q_pos[:, None] - k_pos[None, :]
    bias = -alibi_slopes()[:, None, None] * dist.astype(jnp.float32)
    s = jnp.where(dist >= 0, s + bias[None, :, None], -jnp.inf)
    p = jax.nn.softmax(s, axis=-1)
    a = (p[:, :, 0] - lam * p[:, :, 1]).astype(v.dtype)
    o = jnp.einsum('bhqk,bkhd->bqhd', a, v)
    o = rmsnorm(o, subln_g) * (1.0 - lam_init)
    return o.reshape(o.shape[0], o.shape[1], W_ATTN)


def prompt_attention(q, k, v, lam, lam_init, subln_g):
    b, s = q.shape[0], q.shape[1]
    n_qb = s // Q_BLOCK
    k_pos = jnp.arange(s, dtype=jnp.int32)
    qb = q.reshape(b, n_qb, Q_BLOCK, N_HEADS, 2, DK).swapaxes(0, 1)

    def one_block(args):
        q_blk, i = args
        q_pos = i * Q_BLOCK + jnp.arange(Q_BLOCK, dtype=jnp.int32)
        return diff_attn_core(q_blk, k, v, q_pos, k_pos, lam, lam_init, subln_g)

    o = lax.map(one_block, (qb, jnp.arange(n_qb, dtype=jnp.int32)))
    return o.swapaxes(0, 1).reshape(b, s, W_ATTN)


def sample_attention(q, k_new, v_new, cache_k, cache_v, page_table, lam, lam_init, subln_g):
    bd, t = q.shape[0], q.shape[1]
    past = page_table.shape[1] * PAGE_SIZE
    k_past = cache_k[page_table].reshape(bd, past, N_HEADS, 2, DK)
    v_past = cache_v[page_table].reshape(bd, past, N_HEADS, DV)
    k_all = jnp.concatenate([k_past, k_new], axis=1)
    v_all = jnp.concatenate([v_past, v_new], axis=1)
    q_pos = past + jnp.arange(t, dtype=jnp.int32)
    k_pos = jnp.arange(past + t, dtype=jnp.int32)
    return diff_attn_core(q, k_all, v_all, q_pos, k_pos, lam, lam_init, subln_g)


def expert_swiglu(xb, w_gu, b_gu, w_down, b_down):
    gu = xb @ w_gu + b_gu
    glu = jnp.minimum(gu[:, :D_FF], SWIGLU_LIMIT)
    lin = jnp.clip(gu[:, D_FF:], -SWIGLU_LIMIT, SWIGLU_LIMIT)
    act = glu * jax.nn.sigmoid(SWIGLU_ALPHA * glu) * (lin + 1.0)
    return act @ w_down + b_down


def moe_ffn(h, router_w, router_b, w_gu, b_gu, w_down, b_down):
    n_tok, d = h.shape
    n_assign = n_tok * TOP_K
    logits = h.astype(jnp.float32) @ router_w.astype(jnp.float32) + router_b.astype(jnp.float32)
    top_logits, top_idx = lax.top_k(logits, TOP_K)
    gates = jax.nn.softmax(top_logits, axis=-1)
    flat_e = top_idx.reshape(-1).astype(jnp.int32)
    order = jnp.argsort(flat_e)
    sorted_e = flat_e[order]
    tok = order // TOP_K
    counts = jnp.bincount(flat_e, length=N_EXPERTS)
    blocks = (counts + MOE_BLOCK - 1) // MOE_BLOCK
    block_end = jnp.cumsum(blocks)
    row_start = (block_end - blocks) * MOE_BLOCK
    assign_start = jnp.cumsum(counts) - counts
    dest = row_start[sorted_e] + jnp.arange(n_assign, dtype=jnp.int32) - assign_start[sorted_e]
    n_blocks = -(-n_assign // MOE_BLOCK) + N_EXPERTS
    block_expert = jnp.minimum(jnp.searchsorted(block_end, jnp.arange(n_blocks, dtype=jnp.int32), side='right'), N_EXPERTS - 1)
    x_rows = jnp.zeros((n_blocks * MOE_BLOCK, d), h.dtype).at[dest].set(h[tok])

    def run_block(args):
        xb, e = args
        return expert_swiglu(xb, w_gu[e], b_gu[e], w_down[e], b_down[e])

    y_rows = lax.map(run_block, (x_rows.reshape(n_blocks, MOE_BLOCK, d), block_expert)).reshape(n_blocks * MOE_BLOCK, d)
    w_assign = gates.reshape(-1)[order].astype(h.dtype)
    return jax.ops.segment_sum(y_rows[dest] * w_assign[:, None], tok, num_segments=n_tok)


def merge_and_ffn(x, y_a, y_b, z_ga, z_gb, w_pa, w_pb, w_o, norm2_g, router_w, router_b, w_gu, b_gu, w_down, b_down):
    mixed = jax.nn.sigmoid(z_ga) * (y_a @ w_pa) + jax.nn.sigmoid(z_gb) * (y_b @ w_pb)
    x = x + mixed @ w_o
    b, t, d = x.shape
    h = rmsnorm(x, norm2_g).reshape(b * t, d)
    return x + moe_ffn(h, router_w, router_b, w_gu, b_gu, w_down, b_down).reshape(b, t, d)


def setup_inputs(seed: int = 0) -> dict:
    key = jax.random.key(seed)
    ks = iter(jax.random.split(key, 40))

    def nrm(shape, scale):
        return jax.random.normal(next(ks), shape, jnp.float32) * scale

    n_pages = PAST_LEN // PAGE_SIZE
    n_used = DEC_BATCH * n_pages
    n_phys = n_used + max(1, n_used // 4)
    x_prompt = nrm((BATCH, SEQ, D_MODEL), 1.0)
    x_sample = nrm((DEC_BATCH, DEC_SEQ, D_MODEL), 1.0)
    cache_k = nrm((DEPTH, n_phys, PAGE_SIZE, N_HEADS, 2 * DK), 1.0)
    cache_v = nrm((DEPTH, n_phys, PAGE_SIZE, N_HEADS, DV), 1.0)
    page_table = jax.random.permutation(next(ks), n_phys)[:n_used].reshape(DEC_BATCH, n_pages).astype(jnp.int32)
    return {
        'x_prompt': x_prompt,
        'x_sample': x_sample,
        'cache_k': cache_k,
        'cache_v': cache_v,
        'page_table': page_table,
        'norm1_g': 1.0 + nrm((DEPTH, D_MODEL), 0.05),
        'w_in': nrm((DEPTH, D_MODEL, D_IN), D_MODEL ** -0.5),
        'sgu_ln_g': 1.0 + nrm((DEPTH, W_SGU), 0.05),
        'sgu_ln_b': nrm((DEPTH, W_SGU), 0.02),
        'sgu_w': nrm((DEPTH, N_SGU_GROUPS, CHUNK, CHUNK), CHUNK ** -0.5),
        'sgu_b': 1.0 + nrm((DEPTH, N_SGU_GROUPS, CHUNK), 0.1),
        'lambda_q1': nrm((DEPTH, DK), 0.1),
        'lambda_k1': nrm((DEPTH, DK), 0.1),
        'lambda_q2': nrm((DEPTH, DK), 0.1),
        'lambda_k2': nrm((DEPTH, DK), 0.1),
        'subln_g': 1.0 + nrm((DEPTH, DV), 0.05),
        'w_branch_a': nrm((DEPTH, W_SGU, D_MODEL), W_SGU ** -0.5),
        'w_branch_b': nrm((DEPTH, W_ATTN, D_MODEL), W_ATTN ** -0.5),
        'w_out': nrm((DEPTH, D_MODEL, D_MODEL), D_MODEL ** -0.5),
        'norm2_g': 1.0 + nrm((DEPTH, D_MODEL), 0.05),
        'router_w': nrm((DEPTH, D_MODEL, N_EXPERTS), D_MODEL ** -0.5),
        'router_b': nrm((DEPTH, N_EXPERTS), 0.01),
        'expert_w_gu': nrm((DEPTH, N_EXPERTS, D_MODEL, 2 * D_FF), D_MODEL ** -0.5),
        'expert_b_gu': nrm((DEPTH, N_EXPERTS, 2 * D_FF), 0.01),
        'expert_w_down': nrm((DEPTH, N_EXPERTS, D_FF, D_MODEL), D_FF ** -0.5),
        'expert_b_down': nrm((DEPTH, N_EXPERTS, D_MODEL), 0.01),
        'final_norm_g': 1.0 + nrm((D_MODEL,), 0.05),
    }


def reference(x_prompt, x_sample, cache_k, cache_v, page_table, norm1_g, w_in, sgu_ln_g, sgu_ln_b, sgu_w, sgu_b,
              lambda_q1, lambda_k1, lambda_q2, lambda_k2, subln_g, w_branch_a, w_branch_b, w_out, norm2_g,
              router_w, router_b, expert_w_gu, expert_b_gu, expert_w_down, expert_b_down, final_norm_g):
    xp, xs = x_prompt, x_sample
    kp_rows, vp_rows, ks_rows, vs_rows, sgu_rows = [], [], [], [], []
    for l in range(DEPTH):
        lam_init = lambda_init(l)
        lam = (jnp.exp(jnp.sum(lambda_q1[l].astype(jnp.float32) * lambda_k1[l].astype(jnp.float32)))
               - jnp.exp(jnp.sum(lambda_q2[l].astype(jnp.float32) * lambda_k2[l].astype(jnp.float32))) + lam_init)
        ffn_w = (w_branch_a[l], w_branch_b[l], w_out[l], norm2_g[l], router_w[l], router_b[l],
                 expert_w_gu[l], expert_b_gu[l], expert_w_down[l], expert_b_down[l])
        u, v, q, k, va, ga, gb = mixer_projections(xp, norm1_g[l], w_in[l], sgu_ln_g[l], sgu_ln_b[l])
        y_a = spatial_gating(u, v, sgu_w[l], sgu_b[l])
        y_b = prompt_attention(q, k, va, lam, lam_init, subln_g[l])
        xp = merge_and_ffn(xp, y_a, y_b, ga, gb, *ffn_w)
        kp_rows.append(k.reshape(k.shape[0], k.shape[1], N_HEADS, 2 * DK))
        vp_rows.append(va)
        u, v, q, k, va, ga, gb = mixer_projections(xs, norm1_g[l], w_in[l], sgu_ln_g[l], sgu_ln_b[l])
        y_a = spatial_gating(u, v, sgu_w[l], sgu_b[l])
        y_b = sample_attention(q, k, va, cache_k[l], cache_v[l], page_table, lam, lam_init, subln_g[l])
        xs = merge_and_ffn(xs, y_a, y_b, ga, gb, *ffn_w)
        ks_rows.append(k.reshape(k.shape[0], k.shape[1], N_HEADS, 2 * DK))
        vs_rows.append(va)
        sgu_rows.append(v)
    y_prompt = rmsnorm(xp, final_norm_g)
    y_sample = rmsnorm(xs, final_norm_g)
    return (y_prompt, y_sample, jnp.stack(kp_rows), jnp.stack(vp_rows), jnp.stack(ks_rows), jnp.stack(vs_rows), jnp.stack(sgu_rows))
```

```python
import functools
import math

import jax
import jax.numpy as jnp
from jax import lax
from jax.experimental import pallas as pl
from jax.experimental.pallas import tpu as pltpu

F32 = jnp.float32
BF16 = jnp.bfloat16

D_MODEL = 2048
N_HEADS = 8
DK = 64
DV = 2 * DK
W_ATTN = N_HEADS * DV
D_QK = 2 * N_HEADS * DK
W_SGU = D_MODEL // 2
N_SGU_GROUPS = 8
SGU_GROUP = W_SGU // N_SGU_GROUPS
CHUNK = 128
N_EXPERTS = 32
TOP_K = 4
D_FF = D_MODEL
SWIGLU_LIMIT = 7.0
SWIGLU_ALPHA = 1.702
EPS = 1e-6
PAGE_SIZE = 128
D_IN = 2 * W_SGU + 2 * D_QK + W_ATTN + 2 * D_MODEL

LANES = 128
VMEM_BYTES_V7X = 64 * 1024 * 1024
MIB = 1024 * 1024

PROJ_TN = 1024
PROJ_RC = 256
ATT_TQ = 256
ATT_TK = 512
DEC_PP = 4
MERGE_TM = 256
MOE_RB = 1024
MOE_SB = 256
MOE_TF = 256
NEG_BIG = -1e30


def _cparams(semantics, vmem_mib):
    return pltpu.CompilerParams(dimension_semantics=semantics, vmem_limit_bytes=vmem_mib * MIB)


def _const_spec(shape):
    nd = len(shape)
    return pl.BlockSpec(shape, lambda *_: (0,) * nd, pipeline_mode=pl.Buffered(1))


def _rmsnorm_rows(x, g):
    ms = jnp.mean(x * x, axis=-1, keepdims=True)
    return x * lax.rsqrt(ms + EPS) * g


def _rmsnorm_body(x_ref, g_ref, o_ref):
    o_ref[...] = _rmsnorm_rows(x_ref[...], g_ref[...]).astype(o_ref.dtype)


def rmsnorm_to(x2d, g, tm, dtype):
    m, d = x2d.shape
    return pl.pallas_call(
        _rmsnorm_body,
        grid=(m // tm,),
        in_specs=[pl.BlockSpec((tm, d), lambda i: (i, 0)), pl.BlockSpec((1, d), lambda i: (0, 0))],
        out_specs=pl.BlockSpec((tm, d), lambda i: (i, 0)),
        out_shape=jax.ShapeDtypeStruct((m, d), dtype),
        compiler_params=_cparams(("parallel",), 32),
        name="rmsnorm",
    )(x2d, g.reshape(1, d))


def _gelu(z):
    return 0.5 * z * (1.0 + lax.erf(z * (2.0 ** -0.5)))


def _proj_body(h_ref, w_ref, *refs, kind, rc):
    tm = h_ref.shape[0]

    def chunk(c, carry):
        r = pl.multiple_of(c * rc, rc)
        rows = pl.ds(r, rc)
        z = jnp.dot(h_ref[rows, :], w_ref[...], preferred_element_type=F32)
        if kind == "gelu":
            refs[0][rows, :] = _gelu(z).astype(refs[0].dtype)
        elif kind == "gelu_ln":
            g_ref, b_ref, o_ref = refs
            a = _gelu(z)
            mu = jnp.mean(a, axis=-1, keepdims=True)
            var = jnp.mean(jnp.square(a - mu), axis=-1, keepdims=True)
            y = (a - mu) * lax.rsqrt(var + EPS)
            o_ref[rows, :] = (y * g_ref[...] + b_ref[...]).astype(o_ref.dtype)
        elif kind == "scale":
            refs[0][rows, :] = (z * (DK ** -0.5)).astype(refs[0].dtype)
        elif kind == "dual":
            refs[0][rows, :] = z
            refs[1][rows, :] = z.astype(refs[1].dtype)
        elif kind == "sigmoid":
            refs[0][rows, :] = jax.nn.sigmoid(z).astype(refs[0].dtype)
        return carry

    lax.fori_loop(0, tm // rc, chunk, 0)


def in_proj(h, w_bf, col_blk, n_blk, kind, tm, out_dtypes, extra=()):
    m, d = h.shape
    tn = PROJ_TN
    rc = min(PROJ_RC, tm)
    in_specs = [
        pl.BlockSpec((tm, d), lambda n, i: (i, 0)),
        pl.BlockSpec((d, tn), lambda n, i: (0, col_blk + n)),
    ] + [pl.BlockSpec((1, tn), lambda n, i: (0, 0)) for _ in extra]
    out_specs = [pl.BlockSpec((tm, tn), lambda n, i: (i, n)) for _ in out_dtypes]
    out_shape = [jax.ShapeDtypeStruct((m, n_blk * tn), dt) for dt in out_dtypes]
    res = pl.pallas_call(
        functools.partial(_proj_body, kind=kind, rc=rc),
        grid=(n_blk, m // tm),
        in_specs=in_specs,
        out_specs=out_specs,
        out_shape=out_shape,
        compiler_params=_cparams(("parallel", "parallel"), 48),
        name="in_proj_" + kind,
    )(h, w_bf, *extra)
    return res


def _sgu_body(u_ref, v_ref, w_ref, bt_ref, o_ref, *, rows, n_chunks):
    ii = lax.broadcasted_iota(jnp.int32, (CHUNK, CHUNK), 0)
    jj = lax.broadcasted_iota(jnp.int32, (CHUNK, CHUNK), 1)
    causal = jj <= ii
    for c in range(n_chunks):
        r0 = c * rows
        vb = v_ref[r0:r0 + rows, :]
        if rows < CHUNK:
            vb = jnp.concatenate([vb, jnp.zeros((CHUNK - rows, vb.shape[1]), vb.dtype)], axis=0)
        vb = vb.astype(BF16)
        for g in range(N_SGU_GROUPS):
            w = jnp.where(causal, w_ref[g], 0.0).astype(BF16)
            cols = slice(g * SGU_GROUP, (g + 1) * SGU_GROUP)
            f = jnp.dot(w, vb[:, cols], preferred_element_type=F32) + bt_ref[:, g:g + 1]
            u = u_ref[r0:r0 + rows, cols].astype(F32)
            o_ref[r0:r0 + rows, cols] = (u * f[:rows]).astype(o_ref.dtype)


def sgu(u, v, w_s, b_s, rows, n_chunks, out_dtype):
    m, w = u.shape
    tm = rows * n_chunks
    return pl.pallas_call(
        functools.partial(_sgu_body, rows=rows, n_chunks=n_chunks),
        grid=(m // tm,),
        in_specs=[
            pl.BlockSpec((tm, w), lambda i: (i, 0)),
            pl.BlockSpec((tm, w), lambda i: (i, 0)),
            pl.BlockSpec((N_SGU_GROUPS, CHUNK, CHUNK), lambda i: (0, 0, 0)),
            pl.BlockSpec((CHUNK, N_SGU_GROUPS), lambda i: (0, 0)),
        ],
        out_specs=pl.BlockSpec((tm, w), lambda i: (i, 0)),
        out_shape=jax.ShapeDtypeStruct((m, w), out_dtype),
        compiler_params=_cparams(("parallel",), 32),
        name="sgu",
    )(u, v, w_s, b_s.T)


def _lambda_value(lam_ref, lam_init):
    lv = lam_ref[...]
    d1 = jnp.sum(lv[0:1] * lv[1:2], axis=-1, keepdims=True)
    d2 = jnp.sum(lv[2:3] * lv[3:4], axis=-1, keepdims=True)
    return jnp.exp(d1) - jnp.exp(d2) + lam_init


def _head_out(o0, o1, lam, g, lam_init):
    o = o0 - lam * o1
    return _rmsnorm_rows(o, g) * (1.0 - lam_init)


def _flash_body(q_ref, k_ref, v_ref, lam_ref, g_ref, slope_ref, o_ref, m_s, l_s, acc, *, tq, tk, lam_init):
    qi = pl.program_id(2)
    q = q_ref[...]
    lane = lax.broadcasted_iota(jnp.int32, q.shape, 1)
    zero = jnp.zeros_like(q)
    q2 = jnp.concatenate([jnp.where(lane < DK, q, zero), jnp.where(lane >= DK, q, zero)], axis=0)
    slope = slope_ref[...]
    col = lax.broadcasted_iota(jnp.int32, (1, tk), 1)
    q0 = qi * tq

    m_s[...] = jnp.full(m_s.shape, NEG_BIG, F32)
    l_s[...] = jnp.zeros(l_s.shape, F32)
    acc[...] = jnp.zeros(acc.shape, F32)

    def step(j, masked):
        k0 = pl.multiple_of(j * tk, tk)
        kj = k_ref[pl.ds(k0, tk), :]
        vj = v_ref[pl.ds(k0, tk), :]
        s = lax.dot_general(q2, kj, (((1,), (1,)), ((), ())), preferred_element_type=F32)
        s = s + slope * (col + (k0 - q0)).astype(F32)
        if masked:
            rr = lax.broadcasted_iota(jnp.int32, s.shape, 0)
            cc = lax.broadcasted_iota(jnp.int32, s.shape, 1)
            qpos = jnp.where(rr >= tq, rr - tq, rr) + q0
            s = jnp.where(cc + k0 <= qpos, s, NEG_BIG)
        m_old = m_s[...]
        m_new = jnp.maximum(m_old, jnp.max(s, axis=-1, keepdims=True))
        alpha = jnp.exp(m_old - m_new)
        p = jnp.exp(s - m_new)
        l_s[...] = alpha * l_s[...] + jnp.sum(p, axis=-1, keepdims=True)
        acc[...] = alpha * acc[...] + jnp.dot(p.astype(BF16), vj, preferred_element_type=F32)
        m_s[...] = m_new

    jd = q0 // tk

    def full_step(j, carry):
        step(j, False)
        return carry

    lax.fori_loop(0, jd, full_step, 0)
    step(jd, True)

    inv = 1.0 / l_s[...]
    o0 = acc[0:tq, :] * inv[0:tq]
    o1 = acc[tq:2 * tq, :] * inv[tq:2 * tq]
    lam = _lambda_value(lam_ref, lam_init)
    o_ref[...] = _head_out(o0, o1, lam, g_ref[...], lam_init).astype(o_ref.dtype)


def prompt_attention(q, k, v, lam_vecs, subln_g, slopes, batch, seq, lam_init):
    tq, tk = ATT_TQ, ATT_TK
    nq = seq // tq
    slope_b = jnp.broadcast_to(slopes[:, None, None], (N_HEADS, 1, tk)).astype(F32)
    return pl.pallas_call(
        functools.partial(_flash_body, tq=tq, tk=tk, lam_init=lam_init),
        grid=(batch, N_HEADS, nq),
        in_specs=[
            pl.BlockSpec((tq, DV), lambda b, h, i: (b * nq + i, h)),
            pl.BlockSpec((seq, DV), lambda b, h, i: (b, h)),
            pl.BlockSpec((seq, DV), lambda b, h, i: (b, h)),
            pl.BlockSpec((4, DK), lambda b, h, i: (0, 0)),
            pl.BlockSpec((1, DV), lambda b, h, i: (0, 0)),
            pl.BlockSpec((None, 1, tk), lambda b, h, i: (h, 0, 0)),
        ],
        out_specs=pl.BlockSpec((tq, DV), lambda b, h, i: (b * nq + i, h)),
        out_shape=jax.ShapeDtypeStruct((batch * seq, W_ATTN), BF16),
        scratch_shapes=[
            pltpu.VMEM((2 * tq, 1), F32),
            pltpu.VMEM((2 * tq, 1), F32),
            pltpu.VMEM((2 * tq, DV), F32),
        ],
        compiler_params=_cparams(("parallel", "parallel", "arbitrary"), 32),
        name="prompt_attention",
    )(q, k, v, lam_vecs, subln_g.reshape(1, DV), slope_b)


def _div_pow2(x, n):
    assert n & (n - 1) == 0
    return lax.shift_right_logical(x, n.bit_length() - 1)


def _decode_body(pt_ref, q_ref, kn_ref, vn_ref, *rest, pp, n_steps, t_new, past, lam_init):
    kp = rest[:pp]
    vp = rest[pp:2 * pp]
    lam_ref, g_ref, slope_ref, o_ref, qexp, m_s, l_s, acc = rest[2 * pp:]
    j = pl.program_id(1)
    nt = (((1,), (1,)), ((), ()))
    ncol = 2 * N_HEADS * t_new
    slope = slope_ref[...]
    pos = lax.broadcasted_iota(jnp.int32, (ncol, PAGE_SIZE), 1)

    @pl.when(j == 0)
    def _():
        qt = jnp.tile(q_ref[...], (ncol // t_new, 1))
        rr = _div_pow2(lax.broadcasted_iota(jnp.int32, qt.shape, 0), t_new)
        cc = _div_pow2(lax.broadcasted_iota(jnp.int32, qt.shape, 1), DK)
        qexp[...] = jnp.where(rr == cc, qt, 0.0).astype(BF16)
        pad = jnp.zeros((PAGE_SIZE - t_new, kn_ref.shape[1]), F32)
        kn = jnp.concatenate([kn_ref[...], pad], axis=0).astype(BF16)
        vn = jnp.concatenate([vn_ref[...], pad], axis=0).astype(BF16)
        s = lax.dot_general(qexp[...], kn, nt, preferred_element_type=F32)
        tq = jnp.bitwise_and(lax.broadcasted_iota(jnp.int32, s.shape, 0), t_new - 1)
        s = s + slope * pos.astype(F32)
        s = jnp.where(pos <= tq, s, NEG_BIG)
        m = jnp.max(s, axis=-1, keepdims=True)
        p = jnp.exp(s - m)
        m_s[...] = m
        l_s[...] = jnp.sum(p, axis=-1, keepdims=True)
        acc[...] = jnp.dot(p.astype(BF16), vn, preferred_element_type=F32)

    q2 = qexp[...]
    ss = []
    for i in range(pp):
        s_i = lax.dot_general(q2, kp[i][...].astype(BF16), nt, preferred_element_type=F32)
        rel = pos + ((j * pp + i) * PAGE_SIZE - past)
        ss.append(s_i + slope * rel.astype(F32))
    s = jnp.concatenate(ss, axis=1)
    m_old = m_s[...]
    m_new = jnp.maximum(m_old, jnp.max(s, axis=-1, keepdims=True))
    alpha = jnp.exp(m_old - m_new)
    p = jnp.exp(s - m_new)
    l_s[...] = alpha * l_s[...] + jnp.sum(p, axis=-1, keepdims=True)
    pv = jnp.dot(p[:, 0:PAGE_SIZE].astype(BF16), vp[0][...].astype(BF16), preferred_element_type=F32)
    for i in range(1, pp):
        pv = pv + jnp.dot(p[:, i * PAGE_SIZE:(i + 1) * PAGE_SIZE].astype(BF16), vp[i][...].astype(BF16),
                          preferred_element_type=F32)
    acc[...] = alpha * acc[...] + pv
    m_s[...] = m_new

    @pl.when(j == n_steps - 1)
    def _():
        lam = _lambda_value(lam_ref, lam_init)
        inv = 1.0 / l_s[...]
        for h in range(N_HEADS):
            r0 = h * 2 * t_new
            blk = acc[r0:r0 + 2 * t_new, h * DV:(h + 1) * DV] * inv[r0:r0 + 2 * t_new]
            o = _head_out(blk[0:t_new], blk[t_new:2 * t_new], lam, g_ref[...], lam_init)
            o_ref[:, h * DV:(h + 1) * DV] = o.astype(o_ref.dtype)


def decode_attention(q, k_new, v_new, cache_k, cache_v, page_table, lam_vecs, subln_g, slopes, lam_init):
    bd, n_pages = page_table.shape
    t_new = q.shape[0] // bd
    past = n_pages * PAGE_SIZE
    pp = DEC_PP
    n_steps = n_pages // pp
    ncol = 2 * N_HEADS * t_new
    width = N_HEADS * DV
    slope_rows = jnp.broadcast_to(jnp.repeat(slopes, 2 * t_new)[:, None], (ncol, PAGE_SIZE)).astype(F32)

    def page_spec(i):
        return pl.BlockSpec((None, PAGE_SIZE, width), lambda b, j, pt: (pt[b * n_pages + j * pp + i], 0, 0))

    grid_spec = pltpu.PrefetchScalarGridSpec(
        num_scalar_prefetch=1,
        grid=(bd, n_steps),
        in_specs=[
            pl.BlockSpec((t_new, width), lambda b, j, pt: (b, 0)),
            pl.BlockSpec((t_new, width), lambda b, j, pt: (b, 0)),
            pl.BlockSpec((t_new, width), lambda b, j, pt: (b, 0)),
        ] + [page_spec(i) for i in range(pp)] + [page_spec(i) for i in range(pp)] + [
            pl.BlockSpec((4, DK), lambda b, j, pt: (0, 0)),
            pl.BlockSpec((1, DV), lambda b, j, pt: (0, 0)),
            pl.BlockSpec((ncol, PAGE_SIZE), lambda b, j, pt: (0, 0)),
        ],
        out_specs=pl.BlockSpec((t_new, width), lambda b, j, pt: (b, 0)),
        scratch_shapes=[
            pltpu.VMEM((ncol, width), BF16),
            pltpu.VMEM((ncol, 1), F32),
            pltpu.VMEM((ncol, 1), F32),
            pltpu.VMEM((ncol, width), F32),
        ],
    )
    return pl.pallas_call(
        functools.partial(_decode_body, pp=pp, n_steps=n_steps, t_new=t_new, past=past, lam_init=lam_init),
        grid_spec=grid_spec,
        out_shape=jax.ShapeDtypeStruct((bd * t_new, width), F32),
        compiler_params=_cparams(("parallel", "arbitrary"), 40),
        name="decode_attention",
    )(page_table.reshape(-1), q, k_new, v_new, *([cache_k] * pp), *([cache_v] * pp),
      lam_vecs, subln_g.reshape(1, DV), slope_rows)


def _merge_body(x_ref, ya_ref, yb_ref, ga_ref, gb_ref, wpa_ref, wpb_ref, wo_ref, g2_ref, rw_ref, rb_ref,
                x1_ref, h2_ref, lg_ref):
    a = jnp.dot(ya_ref[...].astype(BF16), wpa_ref[...], preferred_element_type=F32)
    b = jnp.dot(yb_ref[...].astype(BF16), wpb_ref[...], preferred_element_type=F32)
    mixed = ga_ref[...].astype(F32) * a + gb_ref[...].astype(F32) * b
    x1 = x_ref[...] + jnp.dot(mixed.astype(BF16), wo_ref[...], preferred_element_type=F32)
    x1_ref[...] = x1
    h2 = _rmsnorm_rows(x1, g2_ref[...])
    h2_ref[...] = h2.astype(h2_ref.dtype)
    lg_ref[...] = jnp.dot(h2, rw_ref[...], preferred_element_type=F32,
                          precision=lax.Precision.HIGHEST) + rb_ref[...]


def merge(x, y_a, y_b, gates, w_pa, w_pb, w_o, norm2_g, router_w_pad, router_b_pad, tm):
    m, d = x.shape
    row = lambda i: (i, 0)
    return pl.pallas_call(
        _merge_body,
        grid=(m // tm,),
        in_specs=[
            pl.BlockSpec((tm, d), row),
            pl.BlockSpec((tm, W_SGU), row),
            pl.BlockSpec((tm, W_ATTN), row),
            pl.BlockSpec((tm, d), lambda i: (i, 0)),
            pl.BlockSpec((tm, d), lambda i: (i, 1)),
            _const_spec((W_SGU, d)),
            _const_spec((W_ATTN, d)),
            _const_spec((d, d)),
            _const_spec((1, d)),
            _const_spec((d, LANES)),
            _const_spec((1, LANES)),
        ],
        out_specs=[pl.BlockSpec((tm, d), row), pl.BlockSpec((tm, d), row), pl.BlockSpec((tm, LANES), row)],
        out_shape=[
            jax.ShapeDtypeStruct((m, d), F32),
            jax.ShapeDtypeStruct((m, d), BF16),
            jax.ShapeDtypeStruct((m, LANES), F32),
        ],
        compiler_params=_cparams(("parallel",), 48),
        name="merge",
    )(x, y_a, y_b, gates, gates, w_pa, w_pb, w_o, norm2_g.reshape(1, d), router_w_pad, router_b_pad)


def _moe_body(we_ref, wb_ref, wn_ref, x_ref, wg_ref, wl_ref, bg_ref, bl_ref, wd_ref, bd_ref, y_ref,
              wg_bf, wl_bf, wd_bf, *, sb):
    w = pl.program_id(0)
    f = pl.program_id(1)
    nsub = wn_ref[w]

    @pl.when(nsub > 0)
    def _():
        wg_bf[...] = wg_ref[...].astype(BF16)
        wl_bf[...] = wl_ref[...].astype(BF16)
        wd_bf[...] = wd_ref[...].astype(BF16)

        def sub(s, carry):
            rows = pl.ds(pl.multiple_of(s * sb, sb), sb)
            xb = x_ref[rows, :]
            glu = jnp.dot(xb, wg_bf[...], preferred_element_type=F32) + bg_ref[...]
            lin = jnp.dot(xb, wl_bf[...], preferred_element_type=F32) + bl_ref[...]
            glu = jnp.minimum(glu, SWIGLU_LIMIT)
            lin = jnp.clip(lin, -SWIGLU_LIMIT, SWIGLU_LIMIT)
            act = glu * jax.nn.sigmoid(SWIGLU_ALPHA * glu) * (lin + 1.0)
            yv = jnp.dot(act.astype(BF16), wd_bf[...], preferred_element_type=F32)

            @pl.when(f == 0)
            def _():
                y_ref[rows, :] = yv + bd_ref[...]

            @pl.when(f > 0)
            def _():
                y_ref[rows, :] += yv

            return carry

        lax.fori_loop(0, nsub, sub, 0)


def moe_experts(x_rows, work_e, work_blk, work_nsub, w_gu, b_gu, w_down, b_down):
    n_rows, d = x_rows.shape
    rb, sb, tf = MOE_RB, MOE_SB, MOE_TF
    nw = n_rows // rb
    nf = D_FF // tf
    grid_spec = pltpu.PrefetchScalarGridSpec(
        num_scalar_prefetch=3,
        grid=(nw, nf),
        in_specs=[
            pl.BlockSpec((rb, d), lambda w, f, we, wb, wn: (wb[w], 0)),
            pl.BlockSpec((None, d, tf), lambda w, f, we, wb, wn: (we[w], 0, f)),
            pl.BlockSpec((None, d, tf), lambda w, f, we, wb, wn: (we[w], 0, nf + f)),
            pl.BlockSpec((None, 1, tf), lambda w, f, we, wb, wn: (we[w], 0, f)),
            pl.BlockSpec((None, 1, tf), lambda w, f, we, wb, wn: (we[w], 0, nf + f)),
            pl.BlockSpec((None, tf, d), lambda w, f, we, wb, wn: (we[w], f, 0)),
            pl.BlockSpec((None, 1, d), lambda w, f, we, wb, wn: (we[w], 0, 0)),
        ],
        out_specs=pl.BlockSpec((rb, d), lambda w, f, we, wb, wn: (wb[w], 0)),
        scratch_shapes=[
            pltpu.VMEM((d, tf), BF16),
            pltpu.VMEM((d, tf), BF16),
            pltpu.VMEM((tf, d), BF16),
        ],
    )
    return pl.pallas_call(
        functools.partial(_moe_body, sb=sb),
        grid_spec=grid_spec,
        out_shape=jax.ShapeDtypeStruct((n_rows, d), F32),
        compiler_params=_cparams(("arbitrary", "arbitrary"), 56),
        name="moe_experts",
    )(work_e, work_blk, work_nsub, x_rows, w_gu, w_gu, b_gu.reshape(N_EXPERTS, 1, 2 * D_FF),
      b_gu.reshape(N_EXPERTS, 1, 2 * D_FF), w_down, b_down.reshape(N_EXPERTS, 1, d))


def _final_body(x_ref, y_ref, g_ref, o_ref):
    o_ref[...] = _rmsnorm_rows(x_ref[...] + y_ref[...], g_ref[...])


def final_norm(x1, moe, g, tm):
    m, d = x1.shape
    row = lambda i: (i, 0)
    return pl.pallas_call(
        _final_body,
        grid=(m // tm,),
        in_specs=[pl.BlockSpec((tm, d), row), pl.BlockSpec((tm, d), row), pl.BlockSpec((1, d), lambda i: (0, 0))],
        out_specs=pl.BlockSpec((tm, d), row),
        out_shape=jax.ShapeDtypeStruct((m, d), F32),
        compiler_params=_cparams(("parallel",), 32),
        name="final_norm",
    )(x1, moe, g.reshape(1, d))


def _routing(logits):
    n_tok = logits.shape[0]
    n_assign = n_tok * TOP_K
    rb, sb = MOE_RB, MOE_SB
    nw = n_assign // rb + N_EXPERTS
    top_logits, top_idx = lax.top_k(logits, TOP_K)
    gates = jax.nn.softmax(top_logits, axis=-1)
    flat_e = top_idx.reshape(-1).astype(jnp.int32)
    onehot = (flat_e[:, None] == jnp.arange(N_EXPERTS, dtype=jnp.int32)[None, :]).astype(jnp.int32)
    csum = jnp.cumsum(onehot, axis=0)
    rank = jnp.take_along_axis(csum, flat_e[:, None], axis=1)[:, 0] - 1
    counts = csum[-1]
    nsb = (counts + rb - 1) // rb
    sb_end = jnp.cumsum(nsb)
    sb_start = sb_end - nsb
    dest = sb_start[flat_e] * rb + rank
    widx = jnp.arange(nw, dtype=jnp.int32)
    n_used = sb_end[-1]
    valid = widx < n_used
    we = jnp.minimum(jnp.searchsorted(sb_end, widx, side="right"), N_EXPERTS - 1).astype(jnp.int32)
    within = widx - sb_start[we]
    rows_valid = jnp.clip(counts[we] - within * rb, 0, rb)
    nsub = jnp.where(valid, (rows_valid + sb - 1) // sb, 0).astype(jnp.int32)
    last = jnp.maximum(n_used - 1, 0)
    work_e = jnp.where(valid, we, we[last]).astype(jnp.int32)
    work_blk = jnp.where(valid, widx, last).astype(jnp.int32)
    return gates, dest, work_e, work_blk, nsub, nw


def _layer_group(x2d, tm, w_in_bf, p, act_dtype):
    h = rmsnorm_to(x2d, p["norm1_g"], tm, BF16)
    (u,) = in_proj(h, w_in_bf, 0, 1, "gelu", tm, [act_dtype])
    (v,) = in_proj(h, w_in_bf, 1, 1, "gelu_ln", tm, [F32],
                   extra=(p["sgu_ln_g"].reshape(1, -1), p["sgu_ln_b"].reshape(1, -1)))
    (q,) = in_proj(h, w_in_bf, 2, 1, "scale", tm, [act_dtype])
    k32, kbf = in_proj(h, w_in_bf, 3, 1, "dual", tm, [F32, BF16])
    va32, vabf = in_proj(h, w_in_bf, 4, 1, "dual", tm, [F32, BF16])
    (gates,) = in_proj(h, w_in_bf, 5, 4, "sigmoid", tm, [BF16])
    return u, v, q, k32, kbf, va32, vabf, gates


def kernel(x_prompt, x_sample, cache_k, cache_v, page_table, norm1_g, w_in, sgu_ln_g, sgu_ln_b, sgu_w, sgu_b,
           lambda_q1, lambda_k1, lambda_q2, lambda_k2, subln_g, w_branch_a, w_branch_b, w_out, norm2_g,
           router_w, router_b, expert_w_gu, expert_b_gu, expert_w_down, expert_b_down, final_norm_g):
    depth = w_in.shape[0]
    assert depth == 1
    l = 0
    bp, sp, d = x_prompt.shape
    bs, ts, _ = x_sample.shape
    lam_init = 0.8 - 0.6 * math.exp(-0.3 * l)
    slopes = jnp.exp2(-(8.0 / N_HEADS) * jnp.arange(1, N_HEADS + 1, dtype=F32))
    lam_vecs = jnp.stack([lambda_q1[l], lambda_k1[l], lambda_q2[l], lambda_k2[l]]).astype(F32)

    w_in_bf = w_in[l].astype(BF16)
    w_pa = w_branch_a[l].astype(BF16)
    w_pb = w_branch_b[l].astype(BF16)
    w_o = w_out[l].astype(BF16)
    rw_pad = jnp.zeros((d, LANES), F32).at[:, :N_EXPERTS].set(router_w[l])
    rb_pad = jnp.zeros((1, LANES), F32).at[0, :N_EXPERTS].set(router_b[l])
    p = {"norm1_g": norm1_g[l], "sgu_ln_g": sgu_ln_g[l], "sgu_ln_b": sgu_ln_b[l]}

    xp = x_prompt.reshape(bp * sp, d)
    u, v, q, k32p, kbf, va32p, vabf, gates = _layer_group(xp, 512, w_in_bf, p, BF16)
    y_a = sgu(u, v, sgu_w[l], sgu_b[l], CHUNK, 4, BF16)
    y_b = prompt_attention(q, kbf, vabf, lam_vecs, subln_g[l], slopes, bp, sp, lam_init)
    x1p, h2p, lgp = merge(xp, y_a, y_b, gates, w_pa, w_pb, w_o, norm2_g[l], rw_pad, rb_pad, MERGE_TM)

    xs = x_sample.reshape(bs * ts, d)
    u, v_s, q, k32s, _, va32s, _, gates = _layer_group(xs, bs * ts, w_in_bf, p, F32)
    y_a = sgu(u, v_s, sgu_w[l], sgu_b[l], ts, bs, F32)
    n_phys = cache_k.shape[1]
    y_b = decode_attention(q, k32s, va32s, cache_k[l].reshape(n_phys, PAGE_SIZE, N_HEADS * 2 * DK),
                           cache_v[l].reshape(n_phys, PAGE_SIZE, N_HEADS * DV), page_table,
                           lam_vecs, subln_g[l], slopes, lam_init)
    x1s, h2s, lgs = merge(xs, y_a, y_b, gates, w_pa, w_pb, w_o, norm2_g[l], rw_pad, rb_pad, bs * ts)

    h2 = jnp.concatenate([h2p, h2s], axis=0)
    logits = jnp.concatenate([lgp[:, :N_EXPERTS], lgs[:, :N_EXPERTS]], axis=0)
    n_tok = h2.shape[0]
    gate_w, dest, work_e, work_blk, work_nsub, nw = _routing(logits)
    x_rows = jnp.zeros((nw * MOE_RB, d), BF16).at[dest].set(jnp.repeat(h2, TOP_K, axis=0))
    y_rows = moe_experts(x_rows, work_e, work_blk, work_nsub, expert_w_gu[l], expert_b_gu[l],
                         expert_w_down[l], expert_b_down[l])
    moe = jnp.sum(y_rows[dest].reshape(n_tok, TOP_K, d) * gate_w[:, :, None], axis=1)

    y_prompt = final_norm(x1p, moe[:bp * sp], final_norm_g, 512).reshape(bp, sp, d)
    y_sample = final_norm(x1s, moe[bp * sp:], final_norm_g, bs * ts).reshape(bs, ts, d)

    k_prompt = k32p.reshape(1, bp, sp, N_HEADS, 2 * DK)
    v_prompt = va32p.reshape(1, bp, sp, N_HEADS, DV)
    k_sample = k32s.reshape(1, bs, ts, N_HEADS, 2 * DK)
    v_sample = va32s.reshape(1, bs, ts, N_HEADS, DV)
    state_sgu_v = v_s.reshape(1, bs, ts, W_SGU)
    return (y_prompt, y_sample, k_prompt, v_prompt, k_sample, v_sample, state_sgu_v)
```

```python
import functools
import math

import jax
import jax.numpy as jnp
from jax import lax
from jax.experimental import pallas as pl
from jax.experimental.pallas import tpu as pltpu

F32 = jnp.float32
BF16 = jnp.bfloat16

D_MODEL = 2048
N_HEADS = 8
DK = 64
DV = 2 * DK
W_ATTN = N_HEADS * DV
D_QK = 2 * N_HEADS * DK
W_SGU = D_MODEL // 2
N_SGU_GROUPS = 8
SGU_GROUP = W_SGU // N_SGU_GROUPS
CHUNK = 128
N_EXPERTS = 32
TOP_K = 4
D_FF = D_MODEL
SWIGLU_LIMIT = 7.0
SWIGLU_ALPHA = 1.702
EPS = 1e-6
PAGE_SIZE = 128
D_IN = 2 * W_SGU + 2 * D_QK + W_ATTN + 2 * D_MODEL
LOG2E = math.log2(math.e)

LANES = 128
SUBLANES = 8
MIB = 1024 * 1024

PROJ_TN = 1024
PROJ_RC = 256
ATT_TQ = 512
ATT_TK = 512
ATT_RB = 256
DEC_PP = 4
MERGE_TM = 256
MOE_RB = 1280
MOE_SB = 256
MOE_TF = 256
COMBINE_TT = 64
NEG_BIG = -1e30


def _cparams(semantics, vmem_mib):
    return pltpu.CompilerParams(dimension_semantics=semantics, vmem_limit_bytes=vmem_mib * MIB)


def _const_spec(shape):
    nd = len(shape)
    return pl.BlockSpec(shape, lambda *_: (0,) * nd, pipeline_mode=pl.Buffered(1))


def _div_pow2(x, n):
    assert n & (n - 1) == 0
    return lax.shift_right_logical(x, n.bit_length() - 1)


def _mod_pow2(x, n):
    assert n & (n - 1) == 0
    return jnp.bitwise_and(x, n - 1)


def _rmsnorm_rows(x, g):
    ms = jnp.mean(x * x, axis=-1, keepdims=True)
    return x * lax.rsqrt(ms + EPS) * g


def _rmsnorm_body(x_ref, g_ref, o_ref):
    o_ref[...] = _rmsnorm_rows(x_ref[...], g_ref[...]).astype(o_ref.dtype)


def rmsnorm_to(x2d, g, tm, dtype):
    m, d = x2d.shape
    return pl.pallas_call(
        _rmsnorm_body,
        grid=(m // tm,),
        in_specs=[pl.BlockSpec((tm, d), lambda i: (i, 0)), pl.BlockSpec((1, d), lambda i: (0, 0))],
        out_specs=pl.BlockSpec((tm, d), lambda i: (i, 0)),
        out_shape=jax.ShapeDtypeStruct((m, d), dtype),
        compiler_params=_cparams(("parallel",), 32),
        name="rmsnorm",
    )(x2d, g.reshape(1, d))


def _gelu(z):
    return 0.5 * z * (1.0 + lax.erf(z * (2.0 ** -0.5)))


def _proj_body(h_ref, w_ref, *refs, kind, rc):
    tm = h_ref.shape[0]

    def chunk(c, carry):
        r = pl.multiple_of(c * rc, rc)
        rows = pl.ds(r, rc)
        z = jnp.dot(h_ref[rows, :], w_ref[...], preferred_element_type=F32)
        if kind == "gelu":
            refs[0][rows, :] = _gelu(z).astype(refs[0].dtype)
        elif kind == "gelu_ln":
            g_ref, b_ref, o_ref = refs
            a = _gelu(z)
            mu = jnp.mean(a, axis=-1, keepdims=True)
            var = jnp.mean(jnp.square(a - mu), axis=-1, keepdims=True)
            y = (a - mu) * lax.rsqrt(var + EPS)
            o_ref[rows, :] = (y * g_ref[...] + b_ref[...]).astype(o_ref.dtype)
        elif kind == "scale":
            refs[0][rows, :] = (z * (DK ** -0.5 * LOG2E)).astype(refs[0].dtype)
        elif kind == "dual":
            refs[0][rows, :] = z
            refs[1][rows, :] = z.astype(refs[1].dtype)
        elif kind == "sigmoid":
            refs[0][rows, :] = jax.nn.sigmoid(z).astype(refs[0].dtype)
        return carry

    lax.fori_loop(0, tm // rc, chunk, 0)


def in_proj(h, w_bf, col_blk, n_blk, kind, tm, out_dtypes, extra=()):
    m, d = h.shape
    tn = PROJ_TN
    rc = min(PROJ_RC, tm)
    in_specs = [
        pl.BlockSpec((tm, d), lambda n, i: (i, 0)),
        pl.BlockSpec((d, tn), lambda n, i: (0, col_blk + n)),
    ] + [pl.BlockSpec((1, tn), lambda n, i: (0, 0)) for _ in extra]
    out_specs = [pl.BlockSpec((tm, tn), lambda n, i: (i, n)) for _ in out_dtypes]
    out_shape = [jax.ShapeDtypeStruct((m, n_blk * tn), dt) for dt in out_dtypes]
    res = pl.pallas_call(
        functools.partial(_proj_body, kind=kind, rc=rc),
        grid=(n_blk, m // tm),
        in_specs=in_specs,
        out_specs=out_specs,
        out_shape=out_shape,
        compiler_params=_cparams(("parallel", "parallel"), 48),
        name="in_proj_" + kind,
    )(h, w_bf, *extra)
    return res


def _sgu_body(u_ref, v_ref, w_ref, bt_ref, o_ref, *, rows, n_chunks):
    ii = lax.broadcasted_iota(jnp.int32, (CHUNK, CHUNK), 0)
    jj = lax.broadcasted_iota(jnp.int32, (CHUNK, CHUNK), 1)
    causal = jj <= ii
    for c in range(n_chunks):
        r0 = c * rows
        vb = v_ref[r0:r0 + rows, :]
        if rows < CHUNK:
            vb = jnp.concatenate([vb, jnp.zeros((CHUNK - rows, vb.shape[1]), vb.dtype)], axis=0)
        vb = vb.astype(BF16)
        for g in range(N_SGU_GROUPS):
            w = jnp.where(causal, w_ref[g], 0.0).astype(BF16)
            cols = slice(g * SGU_GROUP, (g + 1) * SGU_GROUP)
            f = jnp.dot(w, vb[:, cols], preferred_element_type=F32) + bt_ref[:, g:g + 1]
            u = u_ref[r0:r0 + rows, cols].astype(F32)
            o_ref[r0:r0 + rows, cols] = (u * f[:rows]).astype(o_ref.dtype)


def sgu(u, v, w_s, b_s, rows, n_chunks, out_dtype):
    m, w = u.shape
    tm = rows * n_chunks
    return pl.pallas_call(
        functools.partial(_sgu_body, rows=rows, n_chunks=n_chunks),
        grid=(m // tm,),
        in_specs=[
            pl.BlockSpec((tm, w), lambda i: (i, 0)),
            pl.BlockSpec((tm, w), lambda i: (i, 0)),
            pl.BlockSpec((N_SGU_GROUPS, CHUNK, CHUNK), lambda i: (0, 0, 0)),
            pl.BlockSpec((CHUNK, N_SGU_GROUPS), lambda i: (0, 0)),
        ],
        out_specs=pl.BlockSpec((tm, w), lambda i: (i, 0)),
        out_shape=jax.ShapeDtypeStruct((m, w), out_dtype),
        compiler_params=_cparams(("parallel",), 32),
        name="sgu",
    )(u, v, w_s, b_s.T)


def _lambda_value(lam_ref, lam_init):
    lv = lam_ref[...]
    d1 = jnp.sum(lv[0:1] * lv[1:2], axis=-1, keepdims=True)
    d2 = jnp.sum(lv[2:3] * lv[3:4], axis=-1, keepdims=True)
    return jnp.exp(d1) - jnp.exp(d2) + lam_init


def _head_out(o0, o1, lam, g, lam_init):
    o = o0 - lam * o1
    return _rmsnorm_rows(o, g) * (1.0 - lam_init)


def _flash_body(q_ref, k_ref, v_ref, lam_ref, g_ref, slope_ref, o_ref, vaug, m_s, acc, *, tq, tk, rb, lam_init):
    qi = pl.program_id(2)
    n_rb = tq // rb
    chains = [(c, r) for c in range(2) for r in range(n_rb)]

    @pl.when(qi == 0)
    def _():
        vaug[:, 0:DV] = v_ref[...]
        vaug[:, DV:2 * DV] = jnp.ones((vaug.shape[0], DV), BF16)

    q = q_ref[...]
    lane = lax.broadcasted_iota(jnp.int32, q.shape, 1)
    zero = jnp.zeros_like(q)
    qmap = [jnp.where(lane < DK, q, zero), jnp.where(lane >= DK, q, zero)]
    slope = slope_ref[...]
    col = lax.broadcasted_iota(jnp.int32, (1, tk), 1)
    q0 = qi * tq

    m_s[...] = jnp.full(m_s.shape, NEG_BIG, F32)
    acc[...] = jnp.zeros(acc.shape, F32)

    def step(j, masked):
        k0 = pl.multiple_of(j * tk, tk)
        kj = k_ref[pl.ds(k0, tk), :]
        vj = vaug[pl.ds(k0, tk), :]
        bias = slope * (col + (k0 - q0)).astype(F32)
        for ci, (c, r) in enumerate(chains):
            rows = slice(ci * rb, (ci + 1) * rb)
            qc = qmap[c][r * rb:(r + 1) * rb]
            s = lax.dot_general(qc, kj, (((1,), (1,)), ((), ())), preferred_element_type=F32) + bias
            if masked:
                rr = lax.broadcasted_iota(jnp.int32, s.shape, 0) + (r * rb)
                cc = lax.broadcasted_iota(jnp.int32, s.shape, 1)
                s = jnp.where(cc <= rr, s, NEG_BIG)
            m_old = m_s[rows, :]
            m_new = jnp.maximum(m_old, jnp.max(s, axis=-1, keepdims=True))
            alpha = jnp.exp2(m_old - m_new)
            p = jnp.exp2(s - jnp.tile(m_new, (1, tk // LANES)))
            pv = jnp.dot(p.astype(BF16), vj, preferred_element_type=F32)
            acc[rows, :] = jnp.tile(alpha, (1, 2 * DV // LANES)) * acc[rows, :] + pv
            m_s[rows, :] = m_new

    def full_step(j, carry):
        step(j, False)
        return carry

    lax.fori_loop(0, qi, full_step, 0)
    step(qi, True)

    lam = _lambda_value(lam_ref, lam_init)
    for r in range(n_rb):
        a0 = acc[r * rb:(r + 1) * rb, :]
        a1 = acc[(n_rb + r) * rb:(n_rb + r + 1) * rb, :]
        o0 = a0[:, 0:DV] / a0[:, DV:2 * DV]
        o1 = a1[:, 0:DV] / a1[:, DV:2 * DV]
        o_ref[r * rb:(r + 1) * rb, :] = _head_out(o0, o1, lam, g_ref[...], lam_init).astype(o_ref.dtype)


def prompt_attention(q, k, v, lam_vecs, subln_g, slopes2, batch, seq, lam_init):
    tq, tk, rb = ATT_TQ, ATT_TK, ATT_RB
    assert tq == tk and seq % tq == 0 and tq % rb == 0
    nq = seq // tq
    n_chain_rows = 2 * tq
    slope_b = jnp.broadcast_to(slopes2[:, None, None], (N_HEADS, 1, tk)).astype(F32)
    return pl.pallas_call(
        functools.partial(_flash_body, tq=tq, tk=tk, rb=rb, lam_init=lam_init),
        grid=(batch, N_HEADS, nq),
        in_specs=[
            pl.BlockSpec((tq, DV), lambda b, h, i: (b * nq + i, h)),
            pl.BlockSpec((seq, DV), lambda b, h, i: (b, h)),
            pl.BlockSpec((seq, DV), lambda b, h, i: (b, h)),
            pl.BlockSpec((4, DK), lambda b, h, i: (0, 0)),
            pl.BlockSpec((1, DV), lambda b, h, i: (0, 0)),
            pl.BlockSpec((None, 1, tk), lambda b, h, i: (h, 0, 0)),
        ],
        out_specs=pl.BlockSpec((tq, DV), lambda b, h, i: (b * nq + i, h)),
        out_shape=jax.ShapeDtypeStruct((batch * seq, W_ATTN), BF16),
        scratch_shapes=[
            pltpu.VMEM((seq, 2 * DV), BF16),
            pltpu.VMEM((n_chain_rows, LANES), F32),
            pltpu.VMEM((n_chain_rows, 2 * DV), F32),
        ],
        compiler_params=_cparams(("parallel", "parallel", "arbitrary"), 40),
        name="prompt_attention",
    )(q, k, v, lam_vecs, subln_g.reshape(1, DV), slope_b)


def _decode_body(pt_ref, q_ref, kn_ref, vn_ref, *rest, pp, n_steps, t_new, past, lam_init):
    kp = rest[:pp]
    vp = rest[pp:2 * pp]
    lam_ref, g_ref, slope_ref, o_ref, qx, bias0, m_s, l_s, acc = rest[2 * pp:]
    j = pl.program_id(1)
    nt = (((1,), (1,)), ((), ()))
    nrow = 2 * N_HEADS * t_new
    ncol = PAGE_SIZE * N_HEADS
    slope = slope_ref[...]
    rep = ncol // LANES

    @pl.when(j == 0)
    def _():
        q = q_ref[...]
        lane = lax.broadcasted_iota(jnp.int32, (t_new, 2 * DK), 1)
        parts = []
        for h in range(N_HEADS):
            qh = q[:, h * 2 * DK:(h + 1) * 2 * DK]
            parts.append(jnp.where(lane < DK, qh, 0.0))
            parts.append(jnp.where(lane >= DK, qh, 0.0))
        qx[...] = jnp.concatenate(parts, axis=0).astype(BF16)
        rr = lax.broadcasted_iota(jnp.int32, (nrow, ncol), 0)
        cc = lax.broadcasted_iota(jnp.int32, (nrow, ncol), 1)
        head_ok = _mod_pow2(cc, N_HEADS) == _div_pow2(rr, 2 * t_new)
        pos = _div_pow2(cc, N_HEADS).astype(F32)
        bias0[...] = jnp.where(head_ok, jnp.tile(slope, (1, rep)) * pos, NEG_BIG)
        n_new = t_new * N_HEADS
        pad = jnp.zeros((LANES - n_new, 2 * DK), F32)
        kn = jnp.concatenate([kn_ref[...].reshape(n_new, 2 * DK), pad], axis=0).astype(BF16)
        vn = jnp.concatenate([vn_ref[...].reshape(n_new, DV), pad], axis=0).astype(BF16)
        s = lax.dot_general(qx[...], kn, nt, preferred_element_type=F32)
        r1 = lax.broadcasted_iota(jnp.int32, s.shape, 0)
        c1 = lax.broadcasted_iota(jnp.int32, s.shape, 1)
        tk = _div_pow2(c1, N_HEADS)
        ok = (c1 < n_new) & (_mod_pow2(c1, N_HEADS) == _div_pow2(r1, 2 * t_new)) & (tk <= _mod_pow2(r1, t_new))
        s = jnp.where(ok, s + slope * tk.astype(F32), NEG_BIG)
        m = jnp.max(s, axis=-1, keepdims=True)
        p = jnp.exp2(s - m)
        m_s[...] = jnp.broadcast_to(m, m_s.shape)
        l_s[...] = jnp.broadcast_to(jnp.sum(p, axis=-1, keepdims=True), l_s.shape)
        acc[...] = jnp.dot(p.astype(BF16), vn, preferred_element_type=F32)

    q2 = qx[...]
    b0 = bias0[...]
    ss = []
    for i in range(pp):
        kpage = kp[i][...].reshape(ncol, 2 * DK).astype(BF16)
        s_i = lax.dot_general(q2, kpage, nt, preferred_element_type=F32)
        off = ((j * pp + i) * PAGE_SIZE - past).astype(F32)
        ss.append(s_i + b0 + jnp.tile(slope * off, (1, rep)))
    s = jnp.concatenate(ss, axis=1)
    m_old = m_s[...]
    m_new = jnp.maximum(m_old, jnp.max(s, axis=-1, keepdims=True))
    alpha = jnp.exp2(m_old - m_new)
    p = jnp.exp2(s - jnp.tile(m_new, (1, pp * rep)))
    l_s[...] = alpha * l_s[...] + jnp.sum(p, axis=-1, keepdims=True)
    pv = None
    for i in range(pp):
        vpage = vp[i][...].reshape(ncol, DV).astype(BF16)
        d = jnp.dot(p[:, i * ncol:(i + 1) * ncol].astype(BF16), vpage, preferred_element_type=F32)
        pv = d if pv is None else pv + d
    acc[...] = alpha * acc[...] + pv
    m_s[...] = m_new

    @pl.when(j == n_steps - 1)
    def _():
        lam = _lambda_value(lam_ref, lam_init)
        on = acc[...] / l_s[...]
        for h in range(N_HEADS):
            r0 = h * 2 * t_new
            o = _head_out(on[r0:r0 + t_new], on[r0 + t_new:r0 + 2 * t_new], lam, g_ref[...], lam_init)
            o_ref[:, h * DV:(h + 1) * DV] = o.astype(o_ref.dtype)


def decode_attention(q, k_new, v_new, cache_k, cache_v, layer, page_table, lam_vecs, subln_g, slopes2, lam_init):
    bd, n_pages = page_table.shape
    t_new = q.shape[0] // bd
    past = n_pages * PAGE_SIZE
    pp = DEC_PP
    n_steps = n_pages // pp
    nrow = 2 * N_HEADS * t_new
    ncol = PAGE_SIZE * N_HEADS
    assert nrow == LANES and t_new * N_HEADS <= LANES and n_pages % pp == 0
    width = N_HEADS * DV
    slope_rows = jnp.broadcast_to(jnp.repeat(slopes2, 2 * t_new)[:, None], (nrow, LANES)).astype(F32)

    def page_spec(i):
        return pl.BlockSpec((None, None, PAGE_SIZE, N_HEADS, DV),
                            lambda b, j, pt: (layer, pt[b * n_pages + j * pp + i], 0, 0, 0))

    new_spec = pl.BlockSpec((None, t_new, N_HEADS, DV), lambda b, j, pt: (b, 0, 0, 0))
    grid_spec = pltpu.PrefetchScalarGridSpec(
        num_scalar_prefetch=1,
        grid=(bd, n_steps),
        in_specs=[pl.BlockSpec((t_new, width), lambda b, j, pt: (b, 0)), new_spec, new_spec]
        + [page_spec(i) for i in range(pp)] + [page_spec(i) for i in range(pp)] + [
            pl.BlockSpec((4, DK), lambda b, j, pt: (0, 0)),
            pl.BlockSpec((1, DV), lambda b, j, pt: (0, 0)),
            pl.BlockSpec((nrow, LANES), lambda b, j, pt: (0, 0)),
        ],
        out_specs=pl.BlockSpec((t_new, width), lambda b, j, pt: (b, 0)),
        scratch_shapes=[
            pltpu.VMEM((nrow, 2 * DK), BF16),
            pltpu.VMEM((nrow, ncol), F32),
            pltpu.VMEM((nrow, LANES), F32),
            pltpu.VMEM((nrow, LANES), F32),
            pltpu.VMEM((nrow, DV), F32),
        ],
    )
    return pl.pallas_call(
        functools.partial(_decode_body, pp=pp, n_steps=n_steps, t_new=t_new, past=past, lam_init=lam_init),
        grid_spec=grid_spec,
        out_shape=jax.ShapeDtypeStruct((bd * t_new, width), F32),
        compiler_params=_cparams(("parallel", "arbitrary"), 40),
        name="decode_attention",
    )(page_table.reshape(-1), q, k_new, v_new, *([cache_k] * pp), *([cache_v] * pp),
      lam_vecs, subln_g.reshape(1, DV), slope_rows)


def _merge_body(x_ref, ya_ref, yb_ref, ga_ref, gb_ref, wpa_ref, wpb_ref, wo_ref, g2_ref, rw_ref, rb_ref,
                x1_ref, h2_ref, lg_ref):
    a = jnp.dot(ya_ref[...].astype(BF16), wpa_ref[...], preferred_element_type=F32)
    b = jnp.dot(yb_ref[...].astype(BF16), wpb_ref[...], preferred_element_type=F32)
    mixed = ga_ref[...].astype(F32) * a + gb_ref[...].astype(F32) * b
    x1 = x_ref[...] + jnp.dot(mixed.astype(BF16), wo_ref[...], preferred_element_type=F32)
    x1_ref[...] = x1
    h2 = _rmsnorm_rows(x1, g2_ref[...])
    h2_ref[...] = h2
    lg_ref[...] = jnp.dot(h2, rw_ref[...], preferred_element_type=F32,
                          precision=lax.Precision.HIGHEST) + rb_ref[...]


def merge(x, y_a, y_b, gates, w_pa, w_pb, w_o, norm2_g, router_w_pad, router_b_pad, tm):
    m, d = x.shape
    row = lambda i: (i, 0)
    return pl.pallas_call(
        _merge_body,
        grid=(m // tm,),
        in_specs=[
            pl.BlockSpec((tm, d), row),
            pl.BlockSpec((tm, W_SGU), row),
            pl.BlockSpec((tm, W_ATTN), row),
            pl.BlockSpec((tm, d), lambda i: (i, 0)),
            pl.BlockSpec((tm, d), lambda i: (i, 1)),
            _const_spec((W_SGU, d)),
            _const_spec((W_ATTN, d)),
            _const_spec((d, d)),
            _const_spec((1, d)),
            _const_spec((d, LANES)),
            _const_spec((1, LANES)),
        ],
        out_specs=[pl.BlockSpec((tm, d), row), pl.BlockSpec((tm, d), row), pl.BlockSpec((tm, LANES), row)],
        out_shape=[
            jax.ShapeDtypeStruct((m, d), F32),
            jax.ShapeDtypeStruct((m, d), F32),
            jax.ShapeDtypeStruct((m, LANES), F32),
        ],
        compiler_params=_cparams(("parallel",), 56),
        name="merge",
    )(x, y_a, y_b, gates, gates, w_pa, w_pb, w_o, norm2_g.reshape(1, d), router_w_pad, router_b_pad)


def _row_copies(idx_ref, base, n, src_hbm, buf, slot, sem, start):
    def body(r, carry):
        cp = pltpu.make_async_copy(src_hbm.at[pl.ds(idx_ref[base + r], 1)], buf.at[slot, pl.ds(r, 1)],
                                   sem.at[slot])
        if start:
            cp.start()
        else:
            cp.wait()
        return carry

    lax.fori_loop(0, n, body, 0, unroll=8)


def _dispatch_body(valid_ref, cidx_ref, idx_ref, h_hbm, o_ref, buf, sem, *, sb, n_steps):
    i = pl.program_id(0)
    slot = _mod_pow2(i, 2)
    nxt = jnp.minimum(i + 1, n_steps - 1)

    @pl.when((i == 0) & (valid_ref[0] > 0))
    def _():
        _row_copies(idx_ref, cidx_ref[0] * sb, sb, h_hbm, buf, 0, sem, True)

    @pl.when((i + 1 < n_steps) & (valid_ref[nxt] > 0))
    def _():
        _row_copies(idx_ref, cidx_ref[nxt] * sb, sb, h_hbm, buf, 1 - slot, sem, True)

    @pl.when(valid_ref[i] > 0)
    def _():
        _row_copies(idx_ref, cidx_ref[i] * sb, sb, h_hbm, buf, slot, sem, False)
        o_ref[...] = buf[slot].astype(o_ref.dtype)

    @pl.when(valid_ref[i] == 0)
    def _():
        o_ref[...] = jnp.zeros(o_ref.shape, o_ref.dtype)


def moe_dispatch(h2, sub_valid, sub_cidx, src_tok, n_rows):
    sb = MOE_SB
    d = h2.shape[1]
    n_steps = n_rows // sb
    grid_spec = pltpu.PrefetchScalarGridSpec(
        num_scalar_prefetch=3,
        grid=(n_steps,),
        in_specs=[pl.BlockSpec(memory_space=pl.ANY)],
        out_specs=pl.BlockSpec((sb, d), lambda i, va, ci, idx: (i, 0)),
        scratch_shapes=[pltpu.VMEM((2, sb, d), F32), pltpu.SemaphoreType.DMA((2,))],
    )
    return pl.pallas_call(
        functools.partial(_dispatch_body, sb=sb, n_steps=n_steps),
        grid_spec=grid_spec,
        out_shape=jax.ShapeDtypeStruct((n_rows, d), BF16),
        compiler_params=_cparams(("arbitrary",), 32),
        name="moe_dispatch",
    )(sub_valid, sub_cidx, src_tok, h2)


def _combine_body(pos_ref, x1_ref, g_ref, fg_ref, y_hbm, o_ref, buf, sem, *, tt, n_steps):
    i = pl.program_id(0)
    n = tt * TOP_K
    slot = _mod_pow2(i, 2)

    @pl.when(i == 0)
    def _():
        _row_copies(pos_ref, 0, n, y_hbm, buf, 0, sem, True)

    @pl.when(i + 1 < n_steps)
    def _():
        _row_copies(pos_ref, (i + 1) * n, n, y_hbm, buf, 1 - slot, sem, True)

    _row_copies(pos_ref, i * n, n, y_hbm, buf, slot, sem, False)
    rows = buf[slot]
    g = g_ref[...]
    moe = g[:, 0:1] * rows[0:tt]
    for k in range(1, TOP_K):
        moe = moe + g[:, k:k + 1] * rows[k * tt:(k + 1) * tt]
    o_ref[...] = _rmsnorm_rows(x1_ref[...] + moe, fg_ref[...])


def moe_combine_final(x1, pos, gates, y_rows, final_g):
    m, d = x1.shape
    tt = min(COMBINE_TT, m)
    n_steps = m // tt
    pos_km = pos.reshape(n_steps, tt, TOP_K).transpose(0, 2, 1).reshape(-1)
    grid_spec = pltpu.PrefetchScalarGridSpec(
        num_scalar_prefetch=1,
        grid=(n_steps,),
        in_specs=[
            pl.BlockSpec((tt, d), lambda i, p: (i, 0)),
            pl.BlockSpec((tt, TOP_K), lambda i, p: (i, 0)),
            pl.BlockSpec((1, d), lambda i, p: (0, 0)),
            pl.BlockSpec(memory_space=pl.ANY),
        ],
        out_specs=pl.BlockSpec((tt, d), lambda i, p: (i, 0)),
        scratch_shapes=[pltpu.VMEM((2, tt * TOP_K, d), F32), pltpu.SemaphoreType.DMA((2,))],
    )
    return pl.pallas_call(
        functools.partial(_combine_body, tt=tt, n_steps=n_steps),
        grid_spec=grid_spec,
        out_shape=jax.ShapeDtypeStruct((m, d), F32),
        compiler_params=_cparams(("arbitrary",), 32),
        name="moe_combine_final",
    )(pos_km, x1, gates, final_g.reshape(1, d), y_rows)


def _moe_body(we_ref, wb_ref, wn_ref, x_ref, wg_ref, wl_ref, bg_ref, bl_ref, wd_ref, bd_ref, y_ref,
              wgl_bf, wd_bf, *, sb, tf):
    w = pl.program_id(0)
    f = pl.program_id(1)
    nsub = wn_ref[w]
    rb, d = y_ref.shape

    @pl.when(f == 0)
    def _():
        def init(s, carry):
            rows = pl.ds(pl.multiple_of(s * sb, sb), sb)
            y_ref[rows, :] = jnp.broadcast_to(bd_ref[...], (sb, d))
            return carry

        def clear(s, carry):
            rows = pl.ds(pl.multiple_of(s * sb, sb), sb)
            y_ref[rows, :] = jnp.zeros((sb, d), F32)
            return carry

        lax.fori_loop(0, nsub, init, 0)
        lax.fori_loop(nsub, rb // sb, clear, 0)

    @pl.when(nsub > 0)
    def _():
        wgl_bf[:, 0:tf] = wg_ref[...].astype(BF16)
        wgl_bf[:, tf:2 * tf] = wl_ref[...].astype(BF16)
        wd_bf[...] = wd_ref[...].astype(BF16)
        bgl = jnp.concatenate([bg_ref[...], bl_ref[...]], axis=1)

        def sub(s, carry):
            rows = pl.ds(pl.multiple_of(s * sb, sb), sb)
            gl = jnp.dot(x_ref[rows, :], wgl_bf[...], preferred_element_type=F32) + bgl
            glu = jnp.minimum(gl[:, 0:tf], SWIGLU_LIMIT)
            lin = jnp.clip(gl[:, tf:2 * tf], -SWIGLU_LIMIT, SWIGLU_LIMIT)
            act = glu * jax.nn.sigmoid(SWIGLU_ALPHA * glu) * (lin + 1.0)
            y_ref[rows, :] += jnp.dot(act.astype(BF16), wd_bf[...], preferred_element_type=F32)
            return carry

        lax.fori_loop(0, nsub, sub, 0)


def moe_experts(x_rows, work_e, work_blk, work_nsub, w_gu, b_gu, w_down, b_down):
    n_rows, d = x_rows.shape
    rb, sb, tf = MOE_RB, MOE_SB, MOE_TF
    nw = n_rows // rb
    nf = D_FF // tf
    grid_spec = pltpu.PrefetchScalarGridSpec(
        num_scalar_prefetch=3,
        grid=(nw, nf),
        in_specs=[
            pl.BlockSpec((rb, d), lambda w, f, we, wb, wn: (wb[w], 0)),
            pl.BlockSpec((None, d, tf), lambda w, f, we, wb, wn: (we[w], 0, f)),
            pl.BlockSpec((None, d, tf), lambda w, f, we, wb, wn: (we[w], 0, nf + f)),
            pl.BlockSpec((None, 1, tf), lambda w, f, we, wb, wn: (we[w], 0, f)),
            pl.BlockSpec((None, 1, tf), lambda w, f, we, wb, wn: (we[w], 0, nf + f)),
            pl.BlockSpec((None, tf, d), lambda w, f, we, wb, wn: (we[w], f, 0)),
            pl.BlockSpec((None, 1, d), lambda w, f, we, wb, wn: (we[w], 0, 0)),
        ],
        out_specs=pl.BlockSpec((rb, d), lambda w, f, we, wb, wn: (w, 0)),
        scratch_shapes=[
            pltpu.VMEM((d, 2 * tf), BF16),
            pltpu.VMEM((tf, d), BF16),
        ],
    )
    return pl.pallas_call(
        functools.partial(_moe_body, sb=sb, tf=tf),
        grid_spec=grid_spec,
        out_shape=jax.ShapeDtypeStruct((n_rows, d), F32),
        compiler_params=_cparams(("arbitrary", "arbitrary"), 56),
        name="moe_experts",
    )(work_e, work_blk, work_nsub, x_rows, w_gu, w_gu, b_gu.reshape(N_EXPERTS, 1, 2 * D_FF),
      b_gu.reshape(N_EXPERTS, 1, 2 * D_FF), w_down, b_down.reshape(N_EXPERTS, 1, d))


def _routing(logits):
    n_tok = logits.shape[0]
    n_assign = n_tok * TOP_K
    rb, sb = MOE_RB, MOE_SB
    spw = rb // sb
    nw = n_assign // rb + N_EXPERTS
    n_sub_max = n_assign // sb + N_EXPERTS + 1
    top_logits, top_idx = lax.top_k(logits, TOP_K)
    gates = jax.nn.softmax(top_logits, axis=-1)
    flat_e = top_idx.reshape(-1).astype(jnp.int32)
    onehot = (flat_e[:, None] == jnp.arange(N_EXPERTS, dtype=jnp.int32)[None, :]).astype(jnp.int32)
    csum = jnp.cumsum(onehot, axis=0)
    rank = jnp.take_along_axis(csum, flat_e[:, None], axis=1)[:, 0] - 1
    counts = csum[-1]
    nwe = (counts + rb - 1) // rb
    w_end = jnp.cumsum(nwe)
    w_start = w_end - nwe
    dest = w_start[flat_e] * rb + rank
    widx = jnp.arange(nw, dtype=jnp.int32)
    n_used = w_end[-1]
    valid = widx < n_used
    we = jnp.minimum(jnp.searchsorted(w_end, widx, side="right"), N_EXPERTS - 1).astype(jnp.int32)
    rows_valid = jnp.clip(counts[we] - (widx - w_start[we]) * rb, 0, rb)
    nsub = jnp.where(valid, (rows_valid + sb - 1) // sb, 0).astype(jnp.int32)
    last = jnp.maximum(n_used - 1, 0)
    work_e = jnp.where(valid, we, we[last]).astype(jnp.int32)
    work_blk = jnp.where(valid, widx, last).astype(jnp.int32)
    sub_valid = (jnp.arange(spw, dtype=jnp.int32)[None, :] < nsub[:, None]).reshape(-1)
    sub_valid = sub_valid.astype(jnp.int32)
    sub_cidx = jnp.maximum(jnp.cumsum(sub_valid) - 1, 0).astype(jnp.int32)
    order = jnp.argsort(1 - sub_valid, stable=True)[:n_sub_max].astype(jnp.int32)
    src_full = jnp.zeros((nw * rb,), jnp.int32).at[dest].set(jnp.arange(n_assign, dtype=jnp.int32) // TOP_K)
    src_tok = src_full.reshape(nw * spw, sb)[order].reshape(-1)
    return gates, dest, work_e, work_blk, nsub, nw, sub_valid, sub_cidx, src_tok


def _layer_group(x2d, tm, w_in_bf, p, act_dtype):
    h = rmsnorm_to(x2d, p["norm1_g"], tm, BF16)
    (u,) = in_proj(h, w_in_bf, 0, 1, "gelu", tm, [act_dtype])
    (v,) = in_proj(h, w_in_bf, 1, 1, "gelu_ln", tm, [F32],
                   extra=(p["sgu_ln_g"].reshape(1, -1), p["sgu_ln_b"].reshape(1, -1)))
    (q,) = in_proj(h, w_in_bf, 2, 1, "scale", tm, [act_dtype])
    k32, kbf = in_proj(h, w_in_bf, 3, 1, "dual", tm, [F32, BF16])
    va32, vabf = in_proj(h, w_in_bf, 4, 1, "dual", tm, [F32, BF16])
    (gates,) = in_proj(h, w_in_bf, 5, 4, "sigmoid", tm, [BF16])
    return u, v, q, k32, kbf, va32, vabf, gates


def kernel(x_prompt, x_sample, cache_k, cache_v, page_table, norm1_g, w_in, sgu_ln_g, sgu_ln_b, sgu_w, sgu_b,
           lambda_q1, lambda_k1, lambda_q2, lambda_k2, subln_g, w_branch_a, w_branch_b, w_out, norm2_g,
           router_w, router_b, expert_w_gu, expert_b_gu, expert_w_down, expert_b_down, final_norm_g):
    depth = w_in.shape[0]
    assert depth == 1
    l = 0
    bp, sp, d = x_prompt.shape
    bs, ts, _ = x_sample.shape
    lam_init = 0.8 - 0.6 * math.exp(-0.3 * l)
    slopes2 = jnp.exp2(-(8.0 / N_HEADS) * jnp.arange(1, N_HEADS + 1, dtype=F32)) * LOG2E
    lam_vecs = jnp.stack([lambda_q1[l], lambda_k1[l], lambda_q2[l], lambda_k2[l]]).astype(F32)

    w_in_bf = w_in[l].astype(BF16)
    w_pa = w_branch_a[l].astype(BF16)
    w_pb = w_branch_b[l].astype(BF16)
    w_o = w_out[l].astype(BF16)
    rw_pad = jnp.zeros((d, LANES), F32).at[:, :N_EXPERTS].set(router_w[l])
    rb_pad = jnp.zeros((1, LANES), F32).at[0, :N_EXPERTS].set(router_b[l])
    p = {"norm1_g": norm1_g[l], "sgu_ln_g": sgu_ln_g[l], "sgu_ln_b": sgu_ln_b[l]}

    xp = x_prompt.reshape(bp * sp, d)
    u, v, q, k32p, kbf, va32p, vabf, gates = _layer_group(xp, 512, w_in_bf, p, BF16)
    y_a = sgu(u, v, sgu_w[l], sgu_b[l], CHUNK, 4, BF16)
    y_b = prompt_attention(q, kbf, vabf, lam_vecs, subln_g[l], slopes2, bp, sp, lam_init)
    x1p, h2p, lgp = merge(xp, y_a, y_b, gates, w_pa, w_pb, w_o, norm2_g[l], rw_pad, rb_pad, MERGE_TM)

    xs = x_sample.reshape(bs * ts, d)
    u, v_s, q, k32s, _, va32s, _, gates = _layer_group(xs, bs * ts, w_in_bf, p, F32)
    y_a = sgu(u, v_s, sgu_w[l], sgu_b[l], ts, bs, F32)
    k_sample = k32s.reshape(bs, ts, N_HEADS, 2 * DK)
    v_sample = va32s.reshape(bs, ts, N_HEADS, DV)
    y_b = decode_attention(q, k_sample, v_sample, cache_k, cache_v, l, page_table,
                           lam_vecs, subln_g[l], slopes2, lam_init)
    x1s, h2s, lgs = merge(xs, y_a, y_b, gates, w_pa, w_pb, w_o, norm2_g[l], rw_pad, rb_pad, bs * ts)

    h2 = jnp.concatenate([h2p, h2s], axis=0)
    logits = jnp.concatenate([lgp[:, :N_EXPERTS], lgs[:, :N_EXPERTS]], axis=0)
    gate_w, dest, work_e, work_blk, work_nsub, nw, sub_valid, sub_cidx, src_tok = _routing(logits)
    x_rows = moe_dispatch(h2, sub_valid, sub_cidx, src_tok, nw * MOE_RB)
    y_rows = moe_experts(x_rows, work_e, work_blk, work_nsub, expert_w_gu[l], expert_b_gu[l],
                         expert_w_down[l], expert_b_down[l])
    n_p = bp * sp
    pos = dest.reshape(-1, TOP_K)
    y_prompt = moe_combine_final(x1p, pos[:n_p], gate_w[:n_p], y_rows, final_norm_g).reshape(bp, sp, d)
    y_sample = moe_combine_final(x1s, pos[n_p:], gate_w[n_p:], y_rows, final_norm_g).reshape(bs, ts, d)

    k_prompt = k32p.reshape(1, bp, sp, N_HEADS, 2 * DK)
    v_prompt = va32p.reshape(1, bp, sp, N_HEADS, DV)
    state_sgu_v = v_s.reshape(1, bs, ts, W_SGU)
    return (y_prompt, y_sample, k_prompt, v_prompt, k_sample[None], v_sample[None], state_sgu_v)
```

```python
import functools
import math

import jax
import jax.numpy as jnp
from jax import lax
from jax.experimental import pallas as pl
from jax.experimental.pallas import tpu as pltpu

F32 = jnp.float32
BF16 = jnp.bfloat16

D_MODEL = 2048
N_HEADS = 8
DK = 64
DV = 2 * DK
W_ATTN = N_HEADS * DV
D_QK = 2 * N_HEADS * DK
W_SGU = D_MODEL // 2
N_SGU_GROUPS = 8
SGU_GROUP = W_SGU // N_SGU_GROUPS
CHUNK = 128
N_EXPERTS = 32
TOP_K = 4
D_FF = D_MODEL
SWIGLU_LIMIT = 7.0
SWIGLU_ALPHA = 1.702
EPS = 1e-6
PAGE_SIZE = 128
D_IN = 2 * W_SGU + 2 * D_QK + W_ATTN + 2 * D_MODEL
LOG2E = math.log2(math.e)

LANES = 128
SUBLANES = 8
MIB = 1024 * 1024

PROJ_TN = 1024
PROJ_RC = 256
ATT_TQ = 512
ATT_TK = 512
ATT_RB = 256
DEC_PP = 4
MERGE_TM = 256
MOE_RB = 1280
MOE_SB = 256
MOE_UNIT = 128
MOE_TF = 256
COMBINE_TT = 64
NEG_BIG = -1e30


def _cparams(semantics, vmem_mib):
    return pltpu.CompilerParams(dimension_semantics=semantics, vmem_limit_bytes=vmem_mib * MIB)


def _const_spec(shape):
    nd = len(shape)
    return pl.BlockSpec(shape, lambda *_: (0,) * nd, pipeline_mode=pl.Buffered(1))


def _div_pow2(x, n):
    assert n & (n - 1) == 0
    return lax.shift_right_logical(x, n.bit_length() - 1)


def _mod_pow2(x, n):
    assert n & (n - 1) == 0
    return jnp.bitwise_and(x, n - 1)


def _rmsnorm_rows(x, g):
    ms = jnp.mean(x * x, axis=-1, keepdims=True)
    return x * lax.rsqrt(ms + EPS) * g


def _rmsnorm_body(x_ref, g_ref, o_ref):
    o_ref[...] = _rmsnorm_rows(x_ref[...], g_ref[...]).astype(o_ref.dtype)


def rmsnorm_to(x2d, g, tm, dtype):
    m, d = x2d.shape
    return pl.pallas_call(
        _rmsnorm_body,
        grid=(m // tm,),
        in_specs=[pl.BlockSpec((tm, d), lambda i: (i, 0)), pl.BlockSpec((1, d), lambda i: (0, 0))],
        out_specs=pl.BlockSpec((tm, d), lambda i: (i, 0)),
        out_shape=jax.ShapeDtypeStruct((m, d), dtype),
        compiler_params=_cparams(("parallel",), 32),
        name="rmsnorm",
    )(x2d, g.reshape(1, d))


def _gelu(z):
    return 0.5 * z * (1.0 + lax.erf(z * (2.0 ** -0.5)))


def _proj_body(h_ref, w_ref, *refs, kind, rc):
    tm = h_ref.shape[0]

    def chunk(c, carry):
        r = pl.multiple_of(c * rc, rc)
        rows = pl.ds(r, rc)
        z = jnp.dot(h_ref[rows, :], w_ref[...], preferred_element_type=F32)
        if kind == "gelu":
            refs[0][rows, :] = _gelu(z).astype(refs[0].dtype)
        elif kind == "gelu_ln":
            g_ref, b_ref, o_ref = refs
            a = _gelu(z)
            mu = jnp.mean(a, axis=-1, keepdims=True)
            var = jnp.mean(jnp.square(a - mu), axis=-1, keepdims=True)
            y = (a - mu) * lax.rsqrt(var + EPS)
            o_ref[rows, :] = (y * g_ref[...] + b_ref[...]).astype(o_ref.dtype)
        elif kind == "scale":
            refs[0][rows, :] = (z * (DK ** -0.5 * LOG2E)).astype(refs[0].dtype)
        elif kind == "dual":
            refs[0][rows, :] = z
            refs[1][rows, :] = z.astype(refs[1].dtype)
        elif kind == "sigmoid":
            refs[0][rows, :] = jax.nn.sigmoid(z).astype(refs[0].dtype)
        return carry

    lax.fori_loop(0, tm // rc, chunk, 0)


def in_proj(h, w_bf, col_blk, n_blk, kind, tm, out_dtypes, extra=()):
    m, d = h.shape
    tn = PROJ_TN
    rc = min(PROJ_RC, tm)
    in_specs = [
        pl.BlockSpec((tm, d), lambda n, i: (i, 0)),
        pl.BlockSpec((d, tn), lambda n, i: (0, col_blk + n)),
    ] + [pl.BlockSpec((1, tn), lambda n, i: (0, 0)) for _ in extra]
    out_specs = [pl.BlockSpec((tm, tn), lambda n, i: (i, n)) for _ in out_dtypes]
    out_shape = [jax.ShapeDtypeStruct((m, n_blk * tn), dt) for dt in out_dtypes]
    res = pl.pallas_call(
        functools.partial(_proj_body, kind=kind, rc=rc),
        grid=(n_blk, m // tm),
        in_specs=in_specs,
        out_specs=out_specs,
        out_shape=out_shape,
        compiler_params=_cparams(("parallel", "parallel"), 48),
        name="in_proj_" + kind,
    )(h, w_bf, *extra)
    return res


def _sgu_body(u_ref, v_ref, w_ref, bt_ref, o_ref, *, rows, n_chunks):
    ii = lax.broadcasted_iota(jnp.int32, (CHUNK, CHUNK), 0)
    jj = lax.broadcasted_iota(jnp.int32, (CHUNK, CHUNK), 1)
    causal = jj <= ii
    for c in range(n_chunks):
        r0 = c * rows
        vb = v_ref[r0:r0 + rows, :]
        if rows < CHUNK:
            vb = jnp.concatenate([vb, jnp.zeros((CHUNK - rows, vb.shape[1]), vb.dtype)], axis=0)
        vb = vb.astype(BF16)
        for g in range(N_SGU_GROUPS):
            w = jnp.where(causal, w_ref[g], 0.0).astype(BF16)
            cols = slice(g * SGU_GROUP, (g + 1) * SGU_GROUP)
            f = jnp.dot(w, vb[:, cols], preferred_element_type=F32) + bt_ref[:, g:g + 1]
            u = u_ref[r0:r0 + rows, cols].astype(F32)
            o_ref[r0:r0 + rows, cols] = (u * f[:rows]).astype(o_ref.dtype)


def sgu(u, v, w_s, b_s, rows, n_chunks, out_dtype):
    m, w = u.shape
    tm = rows * n_chunks
    return pl.pallas_call(
        functools.partial(_sgu_body, rows=rows, n_chunks=n_chunks),
        grid=(m // tm,),
        in_specs=[
            pl.BlockSpec((tm, w), lambda i: (i, 0)),
            pl.BlockSpec((tm, w), lambda i: (i, 0)),
            pl.BlockSpec((N_SGU_GROUPS, CHUNK, CHUNK), lambda i: (0, 0, 0)),
            pl.BlockSpec((CHUNK, N_SGU_GROUPS), lambda i: (0, 0)),
        ],
        out_specs=pl.BlockSpec((tm, w), lambda i: (i, 0)),
        out_shape=jax.ShapeDtypeStruct((m, w), out_dtype),
        compiler_params=_cparams(("parallel",), 32),
        name="sgu",
    )(u, v, w_s, b_s.T)


def _lambda_value(lam_ref, lam_init):
    lv = lam_ref[...]
    d1 = jnp.sum(lv[0:1] * lv[1:2], axis=-1, keepdims=True)
    d2 = jnp.sum(lv[2:3] * lv[3:4], axis=-1, keepdims=True)
    return jnp.exp(d1) - jnp.exp(d2) + lam_init


def _head_out(o0, o1, lam, g, lam_init):
    o = o0 - lam * o1
    return _rmsnorm_rows(o, g) * (1.0 - lam_init)


def _flash_body(q_ref, k_ref, v_ref, lam_ref, g_ref, slope_ref, o_ref, vaug, m_s, acc, *, tq, tk, rb, lam_init):
    qi = pl.program_id(2)
    n_rb = tq // rb
    chains = [(c, r) for c in range(2) for r in range(n_rb)]

    @pl.when(qi == 0)
    def _():
        vaug[:, 0:DV] = v_ref[...]
        vaug[:, DV:2 * DV] = jnp.ones((vaug.shape[0], DV), BF16)

    q = q_ref[...]
    lane = lax.broadcasted_iota(jnp.int32, q.shape, 1)
    zero = jnp.zeros_like(q)
    qmap = [jnp.where(lane < DK, q, zero), jnp.where(lane >= DK, q, zero)]
    slope = slope_ref[...]
    col = lax.broadcasted_iota(jnp.int32, (1, tk), 1)
    q0 = qi * tq

    m_s[...] = jnp.full(m_s.shape, NEG_BIG, F32)
    acc[...] = jnp.zeros(acc.shape, F32)

    def step(j, masked):
        k0 = pl.multiple_of(j * tk, tk)
        kj = k_ref[pl.ds(k0, tk), :]
        vj = vaug[pl.ds(k0, tk), :]
        bias = slope * (col + (k0 - q0)).astype(F32)
        for ci, (c, r) in enumerate(chains):
            rows = slice(ci * rb, (ci + 1) * rb)
            qc = qmap[c][r * rb:(r + 1) * rb]
            nk = (r + 1) * rb if masked else tk
            s = lax.dot_general(qc, kj[0:nk], (((1,), (1,)), ((), ())), preferred_element_type=F32)
            s = s + bias[:, 0:nk]
            if masked:
                rr = lax.broadcasted_iota(jnp.int32, s.shape, 0) + (r * rb)
                cc = lax.broadcasted_iota(jnp.int32, s.shape, 1)
                s = jnp.where(cc <= rr, s, NEG_BIG)
            m_old = m_s[rows, :]
            m_new = jnp.maximum(m_old, jnp.max(s, axis=-1, keepdims=True))
            alpha = jnp.exp2(m_old - m_new)
            p = jnp.exp2(s - jnp.tile(m_new, (1, nk // LANES)))
            pv = jnp.dot(p.astype(BF16), vj[0:nk], preferred_element_type=F32)
            acc[rows, :] = jnp.tile(alpha, (1, 2 * DV // LANES)) * acc[rows, :] + pv
            m_s[rows, :] = m_new

    def full_step(j, carry):
        step(j, False)
        return carry

    lax.fori_loop(0, qi, full_step, 0)
    step(qi, True)

    lam = _lambda_value(lam_ref, lam_init)
    for r in range(n_rb):
        a0 = acc[r * rb:(r + 1) * rb, :]
        a1 = acc[(n_rb + r) * rb:(n_rb + r + 1) * rb, :]
        o0 = a0[:, 0:DV] / a0[:, DV:2 * DV]
        o1 = a1[:, 0:DV] / a1[:, DV:2 * DV]
        o_ref[r * rb:(r + 1) * rb, :] = _head_out(o0, o1, lam, g_ref[...], lam_init).astype(o_ref.dtype)


def prompt_attention(q, k, v, lam_vecs, subln_g, slopes2, batch, seq, lam_init):
    tq, tk, rb = ATT_TQ, ATT_TK, ATT_RB
    assert tq == tk and seq % tq == 0 and tq % rb == 0
    nq = seq // tq
    n_chain_rows = 2 * tq
    slope_b = jnp.broadcast_to(slopes2[:, None, None], (N_HEADS, 1, tk)).astype(F32)
    return pl.pallas_call(
        functools.partial(_flash_body, tq=tq, tk=tk, rb=rb, lam_init=lam_init),
        grid=(batch, N_HEADS, nq),
        in_specs=[
            pl.BlockSpec((tq, DV), lambda b, h, i: (b * nq + i, h)),
            pl.BlockSpec((seq, DV), lambda b, h, i: (b, h)),
            pl.BlockSpec((seq, DV), lambda b, h, i: (b, h)),
            pl.BlockSpec((4, DK), lambda b, h, i: (0, 0)),
            pl.BlockSpec((1, DV), lambda b, h, i: (0, 0)),
            pl.BlockSpec((None, 1, tk), lambda b, h, i: (h, 0, 0)),
        ],
        out_specs=pl.BlockSpec((tq, DV), lambda b, h, i: (b * nq + i, h)),
        out_shape=jax.ShapeDtypeStruct((batch * seq, W_ATTN), BF16),
        scratch_shapes=[
            pltpu.VMEM((seq, 2 * DV), BF16),
            pltpu.VMEM((n_chain_rows, LANES), F32),
            pltpu.VMEM((n_chain_rows, 2 * DV), F32),
        ],
        compiler_params=_cparams(("parallel", "parallel", "arbitrary"), 40),
        name="prompt_attention",
    )(q, k, v, lam_vecs, subln_g.reshape(1, DV), slope_b)


def _decode_body(pt_ref, q_ref, kn_ref, vn_ref, *rest, pp, n_steps, t_new, past, lam_init):
    kp = rest[:pp]
    vp = rest[pp:2 * pp]
    lam_ref, g_ref, slope_ref, o_ref, qx, bias0, m_s, l_s, acc = rest[2 * pp:]
    j = pl.program_id(1)
    nt = (((1,), (1,)), ((), ()))
    nrow = 2 * N_HEADS * t_new
    ncol = PAGE_SIZE * N_HEADS
    slope = slope_ref[...]
    rep = ncol // LANES

    @pl.when(j == 0)
    def _():
        q = q_ref[...]
        lane = lax.broadcasted_iota(jnp.int32, (t_new, 2 * DK), 1)
        parts = []
        for h in range(N_HEADS):
            qh = q[:, h * 2 * DK:(h + 1) * 2 * DK]
            parts.append(jnp.where(lane < DK, qh, 0.0))
            parts.append(jnp.where(lane >= DK, qh, 0.0))
        qx[...] = jnp.concatenate(parts, axis=0).astype(BF16)
        rr = lax.broadcasted_iota(jnp.int32, (nrow, ncol), 0)
        cc = lax.broadcasted_iota(jnp.int32, (nrow, ncol), 1)
        head_ok = _mod_pow2(cc, N_HEADS) == _div_pow2(rr, 2 * t_new)
        pos = _div_pow2(cc, N_HEADS).astype(F32)
        bias0[...] = jnp.where(head_ok, jnp.tile(slope, (1, rep)) * pos, NEG_BIG)
        n_new = t_new * N_HEADS
        pad = jnp.zeros((LANES - n_new, 2 * DK), F32)
        kn = jnp.concatenate([kn_ref[...].reshape(n_new, 2 * DK), pad], axis=0).astype(BF16)
        vn = jnp.concatenate([vn_ref[...].reshape(n_new, DV), pad], axis=0).astype(BF16)
        s = lax.dot_general(qx[...], kn, nt, preferred_element_type=F32)
        r1 = lax.broadcasted_iota(jnp.int32, s.shape, 0)
        c1 = lax.broadcasted_iota(jnp.int32, s.shape, 1)
        tk = _div_pow2(c1, N_HEADS)
        ok = (c1 < n_new) & (_mod_pow2(c1, N_HEADS) == _div_pow2(r1, 2 * t_new)) & (tk <= _mod_pow2(r1, t_new))
        s = jnp.where(ok, s + slope * tk.astype(F32), NEG_BIG)
        m = jnp.max(s, axis=-1, keepdims=True)
        p = jnp.exp2(s - m)
        m_s[...] = jnp.broadcast_to(m, m_s.shape)
        l_s[...] = jnp.broadcast_to(jnp.sum(p, axis=-1, keepdims=True), l_s.shape)
        acc[...] = jnp.dot(p.astype(BF16), vn, preferred_element_type=F32)

    q2 = qx[...]
    b0 = bias0[...]
    ss = []
    for i in range(pp):
        kpage = kp[i][...].reshape(ncol, 2 * DK).astype(BF16)
        s_i = lax.dot_general(q2, kpage, nt, preferred_element_type=F32)
        off = ((j * pp + i) * PAGE_SIZE - past).astype(F32)
        ss.append(s_i + b0 + jnp.tile(slope * off, (1, rep)))
    s = jnp.concatenate(ss, axis=1)
    m_old = m_s[...]
    m_new = jnp.maximum(m_old, jnp.max(s, axis=-1, keepdims=True))
    alpha = jnp.exp2(m_old - m_new)
    p = jnp.exp2(s - jnp.tile(m_new, (1, pp * rep)))
    l_s[...] = alpha * l_s[...] + jnp.sum(p, axis=-1, keepdims=True)
    pv = None
    for i in range(pp):
        vpage = vp[i][...].reshape(ncol, DV).astype(BF16)
        d = jnp.dot(p[:, i * ncol:(i + 1) * ncol].astype(BF16), vpage, preferred_element_type=F32)
        pv = d if pv is None else pv + d
    acc[...] = alpha * acc[...] + pv
    m_s[...] = m_new

    @pl.when(j == n_steps - 1)
    def _():
        lam = _lambda_value(lam_ref, lam_init)
        on = acc[...] / l_s[...]
        for h in range(N_HEADS):
            r0 = h * 2 * t_new
            o = _head_out(on[r0:r0 + t_new], on[r0 + t_new:r0 + 2 * t_new], lam, g_ref[...], lam_init)
            o_ref[:, h * DV:(h + 1) * DV] = o.astype(o_ref.dtype)


def decode_attention(q, k_new, v_new, cache_k, cache_v, layer, page_table, lam_vecs, subln_g, slopes2, lam_init):
    bd, n_pages = page_table.shape
    t_new = q.shape[0] // bd
    past = n_pages * PAGE_SIZE
    pp = DEC_PP
    n_steps = n_pages // pp
    nrow = 2 * N_HEADS * t_new
    ncol = PAGE_SIZE * N_HEADS
    assert nrow == LANES and t_new * N_HEADS <= LANES and n_pages % pp == 0
    width = N_HEADS * DV
    slope_rows = jnp.broadcast_to(jnp.repeat(slopes2, 2 * t_new)[:, None], (nrow, LANES)).astype(F32)

    def page_spec(i):
        return pl.BlockSpec((None, None, PAGE_SIZE, N_HEADS, DV),
                            lambda b, j, pt: (layer, pt[b * n_pages + j * pp + i], 0, 0, 0))

    new_spec = pl.BlockSpec((None, t_new, N_HEADS, DV), lambda b, j, pt: (b, 0, 0, 0))
    grid_spec = pltpu.PrefetchScalarGridSpec(
        num_scalar_prefetch=1,
        grid=(bd, n_steps),
        in_specs=[pl.BlockSpec((t_new, width), lambda b, j, pt: (b, 0)), new_spec, new_spec]
        + [page_spec(i) for i in range(pp)] + [page_spec(i) for i in range(pp)] + [
            pl.BlockSpec((4, DK), lambda b, j, pt: (0, 0)),
            pl.BlockSpec((1, DV), lambda b, j, pt: (0, 0)),
            pl.BlockSpec((nrow, LANES), lambda b, j, pt: (0, 0)),
        ],
        out_specs=pl.BlockSpec((t_new, width), lambda b, j, pt: (b, 0)),
        scratch_shapes=[
            pltpu.VMEM((nrow, 2 * DK), BF16),
            pltpu.VMEM((nrow, ncol), F32),
            pltpu.VMEM((nrow, LANES), F32),
            pltpu.VMEM((nrow, LANES), F32),
            pltpu.VMEM((nrow, DV), F32),
        ],
    )
    return pl.pallas_call(
        functools.partial(_decode_body, pp=pp, n_steps=n_steps, t_new=t_new, past=past, lam_init=lam_init),
        grid_spec=grid_spec,
        out_shape=jax.ShapeDtypeStruct((bd * t_new, width), F32),
        compiler_params=_cparams(("parallel", "arbitrary"), 40),
        name="decode_attention",
    )(page_table.reshape(-1), q, k_new, v_new, *([cache_k] * pp), *([cache_v] * pp),
      lam_vecs, subln_g.reshape(1, DV), slope_rows)


def _merge_body(x_ref, ya_ref, yb_ref, ga_ref, gb_ref, wpa_ref, wpb_ref, wo_ref, g2_ref, rw_ref, rb_ref,
                x1_ref, h2_ref, lg_ref):
    a = jnp.dot(ya_ref[...].astype(BF16), wpa_ref[...], preferred_element_type=F32)
    b = jnp.dot(yb_ref[...].astype(BF16), wpb_ref[...], preferred_element_type=F32)
    mixed = ga_ref[...].astype(F32) * a + gb_ref[...].astype(F32) * b
    x1 = x_ref[...] + jnp.dot(mixed.astype(BF16), wo_ref[...], preferred_element_type=F32)
    x1_ref[...] = x1
    h2 = _rmsnorm_rows(x1, g2_ref[...])
    h2_ref[...] = h2
    h_hi = h2.astype(BF16)
    h_lo = (h2 - h_hi.astype(F32)).astype(BF16)
    hw = jnp.dot(h_hi, rw_ref[...], preferred_element_type=F32)
    lw = jnp.dot(h_lo, rw_ref[:, 0:LANES], preferred_element_type=F32)
    lg_ref[...] = hw[:, 0:LANES] + (hw[:, LANES:2 * LANES] + lw) + rb_ref[...]


def merge(x, y_a, y_b, gates, w_pa, w_pb, w_o, norm2_g, router_w_pad, router_b_pad, tm):
    m, d = x.shape
    row = lambda i: (i, 0)
    return pl.pallas_call(
        _merge_body,
        grid=(m // tm,),
        in_specs=[
            pl.BlockSpec((tm, d), row),
            pl.BlockSpec((tm, W_SGU), row),
            pl.BlockSpec((tm, W_ATTN), row),
            pl.BlockSpec((tm, d), lambda i: (i, 0)),
            pl.BlockSpec((tm, d), lambda i: (i, 1)),
            _const_spec((W_SGU, d)),
            _const_spec((W_ATTN, d)),
            _const_spec((d, d)),
            _const_spec((1, d)),
            _const_spec((d, 2 * LANES)),
            _const_spec((1, LANES)),
        ],
        out_specs=[pl.BlockSpec((tm, d), row), pl.BlockSpec((tm, d), row), pl.BlockSpec((tm, LANES), row)],
        out_shape=[
            jax.ShapeDtypeStruct((m, d), F32),
            jax.ShapeDtypeStruct((m, d), F32),
            jax.ShapeDtypeStruct((m, LANES), F32),
        ],
        compiler_params=_cparams(("parallel",), 56),
        name="merge",
    )(x, y_a, y_b, gates, gates, w_pa, w_pb, w_o, norm2_g.reshape(1, d), router_w_pad, router_b_pad)


def _row_copies(idx_ref, base, n, src_hbm, buf, slot, sem, start):
    for r in range(n):
        cp = pltpu.make_async_copy(src_hbm.at[pl.ds(idx_ref[base + r], 1)], buf.at[slot, pl.ds(r, 1)],
                                   sem.at[slot])
        if start:
            cp.start()
        else:
            cp.wait()


def _dispatch_body(valid_ref, cidx_ref, idx_ref, h_hbm, o_ref, buf, sem, *, sb, n_steps):
    i = pl.program_id(0)
    slot = _mod_pow2(i, 2)
    nxt = jnp.minimum(i + 1, n_steps - 1)

    @pl.when((i == 0) & (valid_ref[0] > 0))
    def _():
        _row_copies(idx_ref, cidx_ref[0] * sb, sb, h_hbm, buf, 0, sem, True)

    @pl.when((i + 1 < n_steps) & (valid_ref[nxt] > 0))
    def _():
        _row_copies(idx_ref, cidx_ref[nxt] * sb, sb, h_hbm, buf, 1 - slot, sem, True)

    @pl.when(valid_ref[i] > 0)
    def _():
        _row_copies(idx_ref, cidx_ref[i] * sb, sb, h_hbm, buf, slot, sem, False)
        o_ref[...] = buf[slot].astype(o_ref.dtype)

    @pl.when(valid_ref[i] == 0)
    def _():
        o_ref[...] = jnp.zeros(o_ref.shape, o_ref.dtype)


def moe_dispatch(h2, sub_valid, sub_cidx, src_tok, n_rows):
    sb = MOE_SB
    d = h2.shape[1]
    n_steps = n_rows // sb
    grid_spec = pltpu.PrefetchScalarGridSpec(
        num_scalar_prefetch=3,
        grid=(n_steps,),
        in_specs=[pl.BlockSpec(memory_space=pl.ANY)],
        out_specs=pl.BlockSpec((sb, d), lambda i, va, ci, idx: (i, 0)),
        scratch_shapes=[pltpu.VMEM((2, sb, d), F32), pltpu.SemaphoreType.DMA((2,))],
    )
    return pl.pallas_call(
        functools.partial(_dispatch_body, sb=sb, n_steps=n_steps),
        grid_spec=grid_spec,
        out_shape=jax.ShapeDtypeStruct((n_rows, d), BF16),
        compiler_params=_cparams(("arbitrary",), 32),
        name="moe_dispatch",
    )(sub_valid, sub_cidx, src_tok, h2)


def _combine_body(pos_ref, x1_ref, g_ref, fg_ref, y_hbm, o_ref, buf, sem, *, tt, n_steps):
    i = pl.program_id(0)
    n = tt * TOP_K
    slot = _mod_pow2(i, 2)

    @pl.when(i == 0)
    def _():
        _row_copies(pos_ref, 0, n, y_hbm, buf, 0, sem, True)

    @pl.when(i + 1 < n_steps)
    def _():
        _row_copies(pos_ref, (i + 1) * n, n, y_hbm, buf, 1 - slot, sem, True)

    _row_copies(pos_ref, i * n, n, y_hbm, buf, slot, sem, False)
    rows = buf[slot]
    g = g_ref[...]
    moe = g[:, 0:1] * rows[0:tt]
    for k in range(1, TOP_K):
        moe = moe + g[:, k:k + 1] * rows[k * tt:(k + 1) * tt]
    o_ref[...] = _rmsnorm_rows(x1_ref[...] + moe, fg_ref[...])


def moe_combine_final(x1, pos, gates, y_rows, final_g):
    m, d = x1.shape
    tt = min(COMBINE_TT, m)
    n_steps = m // tt
    pos_km = pos.reshape(n_steps, tt, TOP_K).transpose(0, 2, 1).reshape(-1)
    grid_spec = pltpu.PrefetchScalarGridSpec(
        num_scalar_prefetch=1,
        grid=(n_steps,),
        in_specs=[
            pl.BlockSpec((tt, d), lambda i, p: (i, 0)),
            pl.BlockSpec((tt, TOP_K), lambda i, p: (i, 0)),
            pl.BlockSpec((1, d), lambda i, p: (0, 0)),
            pl.BlockSpec(memory_space=pl.ANY),
        ],
        out_specs=pl.BlockSpec((tt, d), lambda i, p: (i, 0)),
        scratch_shapes=[pltpu.VMEM((2, tt * TOP_K, d), F32), pltpu.SemaphoreType.DMA((2,))],
    )
    return pl.pallas_call(
        functools.partial(_combine_body, tt=tt, n_steps=n_steps),
        grid_spec=grid_spec,
        out_shape=jax.ShapeDtypeStruct((m, d), F32),
        compiler_params=_cparams(("arbitrary",), 32),
        name="moe_combine_final",
    )(pos_km, x1, gates, final_g.reshape(1, d), y_rows)


def _moe_body(we_ref, wb_ref, wn_ref, x_ref, wg_ref, wl_ref, bg_ref, bl_ref, wd_ref, bd_ref, y_ref,
              wgl_bf, wd_bf, *, unit, tf):
    w = pl.program_id(0)
    f = pl.program_id(1)
    n_u = wn_ref[w]
    rb, d = y_ref.shape

    def rows_at(off, n):
        return pl.ds(pl.multiple_of(off, unit), n)

    @pl.when(f == 0)
    def _():
        def init(s, carry):
            y_ref[rows_at(s * unit, unit), :] = jnp.broadcast_to(bd_ref[...], (unit, d))
            return carry

        def clear(s, carry):
            y_ref[rows_at(s * unit, unit), :] = jnp.zeros((unit, d), F32)
            return carry

        lax.fori_loop(0, n_u, init, 0)
        lax.fori_loop(n_u, rb // unit, clear, 0)

    @pl.when(n_u > 0)
    def _():
        wgl_bf[:, 0:tf] = wg_ref[...].astype(BF16)
        wgl_bf[:, tf:2 * tf] = wl_ref[...].astype(BF16)
        wd_bf[...] = wd_ref[...].astype(BF16)
        bgl = jnp.concatenate([bg_ref[...], bl_ref[...]], axis=1)

        def chunk(off, n):
            rows = rows_at(off, n)
            gl = jnp.dot(x_ref[rows, :], wgl_bf[...], preferred_element_type=F32) + bgl
            glu = jnp.minimum(gl[:, 0:tf], SWIGLU_LIMIT)
            lin = jnp.clip(gl[:, tf:2 * tf], -SWIGLU_LIMIT, SWIGLU_LIMIT)
            act = glu * jax.nn.sigmoid(SWIGLU_ALPHA * glu) * (lin + 1.0)
            y_ref[rows, :] += jnp.dot(act.astype(BF16), wd_bf[...], preferred_element_type=F32)

        n4 = lax.shift_right_logical(n_u, 2)

        def big(s, carry):
            chunk(s * (4 * unit), 4 * unit)
            return carry

        lax.fori_loop(0, n4, big, 0)
        off2 = n4 * (4 * unit)

        @pl.when(jnp.bitwise_and(n_u, 2) != 0)
        def _():
            chunk(off2, 2 * unit)

        @pl.when(jnp.bitwise_and(n_u, 1) != 0)
        def _():
            chunk(off2 + jnp.bitwise_and(n_u, 2) * unit, unit)


def moe_experts(x_rows, work_e, work_blk, work_nsub, w_gu, b_gu, w_down, b_down):
    n_rows, d = x_rows.shape
    rb, tf = MOE_RB, MOE_TF
    nw = n_rows // rb
    nf = D_FF // tf
    grid_spec = pltpu.PrefetchScalarGridSpec(
        num_scalar_prefetch=3,
        grid=(nw, nf),
        in_specs=[
            pl.BlockSpec((rb, d), lambda w, f, we, wb, wn: (wb[w], 0)),
            pl.BlockSpec((None, d, tf), lambda w, f, we, wb, wn: (we[w], 0, f)),
            pl.BlockSpec((None, d, tf), lambda w, f, we, wb, wn: (we[w], 0, nf + f)),
            pl.BlockSpec((None, 1, tf), lambda w, f, we, wb, wn: (we[w], 0, f)),
            pl.BlockSpec((None, 1, tf), lambda w, f, we, wb, wn: (we[w], 0, nf + f)),
            pl.BlockSpec((None, tf, d), lambda w, f, we, wb, wn: (we[w], f, 0)),
            pl.BlockSpec((None, 1, d), lambda w, f, we, wb, wn: (we[w], 0, 0)),
        ],
        out_specs=pl.BlockSpec((rb, d), lambda w, f, we, wb, wn: (w, 0)),
        scratch_shapes=[
            pltpu.VMEM((d, 2 * tf), BF16),
            pltpu.VMEM((tf, d), BF16),
        ],
    )
    return pl.pallas_call(
        functools.partial(_moe_body, unit=MOE_UNIT, tf=tf),
        grid_spec=grid_spec,
        out_shape=jax.ShapeDtypeStruct((n_rows, d), F32),
        compiler_params=_cparams(("arbitrary", "arbitrary"), 56),
        name="moe_experts",
    )(work_e, work_blk, work_nsub, x_rows, w_gu, w_gu, b_gu.reshape(N_EXPERTS, 1, 2 * D_FF),
      b_gu.reshape(N_EXPERTS, 1, 2 * D_FF), w_down, b_down.reshape(N_EXPERTS, 1, d))


def _routing(logits):
    n_tok = logits.shape[0]
    n_assign = n_tok * TOP_K
    rb, sb, unit = MOE_RB, MOE_SB, MOE_UNIT
    spw = rb // sb
    nw = n_assign // rb + N_EXPERTS
    n_sub_max = n_assign // sb + N_EXPERTS + 1
    top_logits, top_idx = lax.top_k(logits, TOP_K)
    gates = jax.nn.softmax(top_logits, axis=-1)
    flat_e = top_idx.reshape(-1).astype(jnp.int32)
    onehot = (flat_e[:, None] == jnp.arange(N_EXPERTS, dtype=jnp.int32)[None, :]).astype(jnp.int32)
    csum = jnp.cumsum(onehot, axis=0)
    rank = jnp.take_along_axis(csum, flat_e[:, None], axis=1)[:, 0] - 1
    counts = csum[-1]
    nwe = (counts + rb - 1) // rb
    w_end = jnp.cumsum(nwe)
    w_start = w_end - nwe
    dest = w_start[flat_e] * rb + rank
    widx = jnp.arange(nw, dtype=jnp.int32)
    n_used = w_end[-1]
    valid = widx < n_used
    we = jnp.minimum(jnp.searchsorted(w_end, widx, side="right"), N_EXPERTS - 1).astype(jnp.int32)
    rows_valid = jnp.clip(counts[we] - (widx - w_start[we]) * rb, 0, rb)
    n_units = jnp.where(valid, (rows_valid + unit - 1) // unit, 0).astype(jnp.int32)
    last = jnp.maximum(n_used - 1, 0)
    work_e = jnp.where(valid, we, we[last]).astype(jnp.int32)
    work_blk = jnp.where(valid, widx, last).astype(jnp.int32)
    sub_valid = (jnp.arange(spw, dtype=jnp.int32)[None, :] * (sb // unit) < n_units[:, None]).reshape(-1)
    sub_valid = sub_valid.astype(jnp.int32)
    sub_cidx = jnp.maximum(jnp.cumsum(sub_valid) - 1, 0).astype(jnp.int32)
    blocks = jnp.argsort(1 - sub_valid, stable=True)[:n_sub_max].astype(jnp.int32)
    bw = blocks // spw
    be = we[bw]
    k0 = (bw - w_start[be]) * rb + (blocks % spw) * sb
    sorted_a = jnp.argsort(flat_e, stable=True).astype(jnp.int32)
    sorted_pad = jnp.concatenate([sorted_a, jnp.zeros((sb,), jnp.int32)])
    a_start = jnp.cumsum(counts) - counts
    first = jnp.clip(a_start[be] + k0, 0, n_assign)
    seg = jax.vmap(lambda s: lax.dynamic_slice(sorted_pad, (s,), (sb,)))(first)
    in_expert = (k0[:, None] + jnp.arange(sb, dtype=jnp.int32)[None, :]) < counts[be][:, None]
    src_tok = jnp.where(in_expert, seg // TOP_K, 0).astype(jnp.int32).reshape(-1)
    return gates, dest, work_e, work_blk, n_units, nw, sub_valid, sub_cidx, src_tok


def _layer_group(x2d, tm, w_in_bf, p, act_dtype):
    h = rmsnorm_to(x2d, p["norm1_g"], tm, BF16)
    (u,) = in_proj(h, w_in_bf, 0, 1, "gelu", tm, [act_dtype])
    (v,) = in_proj(h, w_in_bf, 1, 1, "gelu_ln", tm, [F32],
                   extra=(p["sgu_ln_g"].reshape(1, -1), p["sgu_ln_b"].reshape(1, -1)))
    (q,) = in_proj(h, w_in_bf, 2, 1, "scale", tm, [act_dtype])
    k32, kbf = in_proj(h, w_in_bf, 3, 1, "dual", tm, [F32, BF16])
    va32, vabf = in_proj(h, w_in_bf, 4, 1, "dual", tm, [F32, BF16])
    (gates,) = in_proj(h, w_in_bf, 5, 4, "sigmoid", tm, [BF16])
    return u, v, q, k32, kbf, va32, vabf, gates


def kernel(x_prompt, x_sample, cache_k, cache_v, page_table, norm1_g, w_in, sgu_ln_g, sgu_ln_b, sgu_w, sgu_b,
           lambda_q1, lambda_k1, lambda_q2, lambda_k2, subln_g, w_branch_a, w_branch_b, w_out, norm2_g,
           router_w, router_b, expert_w_gu, expert_b_gu, expert_w_down, expert_b_down, final_norm_g):
    depth = w_in.shape[0]
    assert depth == 1
    l = 0
    bp, sp, d = x_prompt.shape
    bs, ts, _ = x_sample.shape
    lam_init = 0.8 - 0.6 * math.exp(-0.3 * l)
    slopes2 = jnp.exp2(-(8.0 / N_HEADS) * jnp.arange(1, N_HEADS + 1, dtype=F32)) * LOG2E
    lam_vecs = jnp.stack([lambda_q1[l], lambda_k1[l], lambda_q2[l], lambda_k2[l]]).astype(F32)

    w_in_bf = w_in[l].astype(BF16)
    w_pa = w_branch_a[l].astype(BF16)
    w_pb = w_branch_b[l].astype(BF16)
    w_o = w_out[l].astype(BF16)
    rw32 = jnp.pad(router_w[l].astype(F32), ((0, 0), (0, LANES - N_EXPERTS)))
    rw_hi = rw32.astype(BF16)
    rw_pad = jnp.concatenate([rw_hi, (rw32 - rw_hi.astype(F32)).astype(BF16)], axis=1)
    rb_pad = jnp.pad(router_b[l].astype(F32), (0, LANES - N_EXPERTS)).reshape(1, LANES)
    p = {"norm1_g": norm1_g[l], "sgu_ln_g": sgu_ln_g[l], "sgu_ln_b": sgu_ln_b[l]}

    xp = x_prompt.reshape(bp * sp, d)
    u, v, q, k32p, kbf, va32p, vabf, gates = _layer_group(xp, 512, w_in_bf, p, BF16)
    y_a = sgu(u, v, sgu_w[l], sgu_b[l], CHUNK, 4, BF16)
    y_b = prompt_attention(q, kbf, vabf, lam_vecs, subln_g[l], slopes2, bp, sp, lam_init)
    x1p, h2p, lgp = merge(xp, y_a, y_b, gates, w_pa, w_pb, w_o, norm2_g[l], rw_pad, rb_pad, MERGE_TM)

    xs = x_sample.reshape(bs * ts, d)
    u, v_s, q, k32s, _, va32s, _, gates = _layer_group(xs, bs * ts, w_in_bf, p, F32)
    y_a = sgu(u, v_s, sgu_w[l], sgu_b[l], ts, bs, F32)
    k_sample = k32s.reshape(bs, ts, N_HEADS, 2 * DK)
    v_sample = va32s.reshape(bs, ts, N_HEADS, DV)
    y_b = decode_attention(q, k_sample, v_sample, cache_k, cache_v, l, page_table,
                           lam_vecs, subln_g[l], slopes2, lam_init)
    x1s, h2s, lgs = merge(xs, y_a, y_b, gates, w_pa, w_pb, w_o, norm2_g[l], rw_pad, rb_pad, bs * ts)

    h2 = jnp.concatenate([h2p, h2s], axis=0)
    logits = jnp.concatenate([lgp[:, :N_EXPERTS], lgs[:, :N_EXPERTS]], axis=0)
    gate_w, dest, work_e, work_blk, work_nsub, nw, sub_valid, sub_cidx, src_tok = _routing(logits)
    x_rows = moe_dispatch(h2, sub_valid, sub_cidx, src_tok, nw * MOE_RB)
    y_rows = moe_experts(x_rows, work_e, work_blk, work_nsub, expert_w_gu[l], expert_b_gu[l],
                         expert_w_down[l], expert_b_down[l])
    n_p = bp * sp
    pos = dest.reshape(-1, TOP_K)
    y_prompt = moe_combine_final(x1p, pos[:n_p], gate_w[:n_p], y_rows, final_norm_g).reshape(bp, sp, d)
    y_sample = moe_combine_final(x1s, pos[n_p:], gate_w[n_p:], y_rows, final_norm_g).reshape(bs, ts, d)

    k_prompt = k32p.reshape(1, bp, sp, N_HEADS, 2 * DK)
    v_prompt = va32p.reshape(1, bp, sp, N_HEADS, DV)
    state_sgu_v = v_s.reshape(1, bs, ts, W_SGU)
    return (y_prompt, y_sample, k_prompt, v_prompt, k_sample[None], v_sample[None], state_sgu_v)
```

```python
import functools
import math

import jax
import jax.numpy as jnp
from jax import lax
from jax.experimental import pallas as pl
from jax.experimental.pallas import tpu as pltpu

F32 = jnp.float32
BF16 = jnp.bfloat16

D_MODEL = 2048
N_HEADS = 8
DK = 64
DV = 2 * DK
W_ATTN = N_HEADS * DV
D_QK = 2 * N_HEADS * DK
W_SGU = D_MODEL // 2
N_SGU_GROUPS = 8
SGU_GROUP = W_SGU // N_SGU_GROUPS
CHUNK = 128
N_EXPERTS = 32
TOP_K = 4
D_FF = D_MODEL
SWIGLU_LIMIT = 7.0
SWIGLU_ALPHA = 1.702
EPS = 1e-6
PAGE_SIZE = 128
D_IN = 2 * W_SGU + 2 * D_QK + W_ATTN + 2 * D_MODEL
LOG2E = math.log2(math.e)

LANES = 128
SUBLANES = 8
MIB = 1024 * 1024

PROJ_TN = 1024
PROJ_RC = 256
ATT_TQ = 512
ATT_TK = 512
ATT_RB = 256
DEC_PP = 4
MERGE_TM = 256
MOE_RB = 1280
MOE_SB = 256
MOE_UNIT = 128
MOE_TF = 512
COMBINE_TT = 64
NEG_BIG = -1e30


def _cparams(semantics, vmem_mib):
    return pltpu.CompilerParams(dimension_semantics=semantics, vmem_limit_bytes=vmem_mib * MIB)


def _const_spec(shape):
    nd = len(shape)
    return pl.BlockSpec(shape, lambda *_: (0,) * nd, pipeline_mode=pl.Buffered(1))


def _div_pow2(x, n):
    assert n & (n - 1) == 0
    return lax.shift_right_logical(x, n.bit_length() - 1)


def _mod_pow2(x, n):
    assert n & (n - 1) == 0
    return jnp.bitwise_and(x, n - 1)


def _rmsnorm_rows(x, g):
    ms = jnp.mean(x * x, axis=-1, keepdims=True)
    return x * lax.rsqrt(ms + EPS) * g


def _rmsnorm_body(x_ref, g_ref, o_ref):
    o_ref[...] = _rmsnorm_rows(x_ref[...], g_ref[...]).astype(o_ref.dtype)


def rmsnorm_to(x2d, g, tm, dtype):
    m, d = x2d.shape
    return pl.pallas_call(
        _rmsnorm_body,
        grid=(m // tm,),
        in_specs=[pl.BlockSpec((tm, d), lambda i: (i, 0)), pl.BlockSpec((1, d), lambda i: (0, 0))],
        out_specs=pl.BlockSpec((tm, d), lambda i: (i, 0)),
        out_shape=jax.ShapeDtypeStruct((m, d), dtype),
        compiler_params=_cparams(("parallel",), 32),
        name="rmsnorm",
    )(x2d, g.reshape(1, d))


def _gelu(z):
    return 0.5 * z * (1.0 + lax.erf(z * (2.0 ** -0.5)))


def _proj_body(h_ref, w_ref, *refs, kind, rc):
    tm = h_ref.shape[0]

    def chunk(c, carry):
        r = pl.multiple_of(c * rc, rc)
        rows = pl.ds(r, rc)
        z = jnp.dot(h_ref[rows, :], w_ref[...], preferred_element_type=F32)
        if kind == "gelu":
            refs[0][rows, :] = _gelu(z).astype(refs[0].dtype)
        elif kind == "gelu_ln":
            g_ref, b_ref, o_ref = refs
            a = _gelu(z)
            mu = jnp.mean(a, axis=-1, keepdims=True)
            var = jnp.mean(jnp.square(a - mu), axis=-1, keepdims=True)
            y = (a - mu) * lax.rsqrt(var + EPS)
            o_ref[rows, :] = (y * g_ref[...] + b_ref[...]).astype(o_ref.dtype)
        elif kind == "scale":
            refs[0][rows, :] = (z * (DK ** -0.5 * LOG2E)).astype(refs[0].dtype)
        elif kind == "dual":
            refs[0][rows, :] = z
            refs[1][rows, :] = z.astype(refs[1].dtype)
        elif kind == "sigmoid":
            refs[0][rows, :] = jax.nn.sigmoid(z).astype(refs[0].dtype)
        return carry

    lax.fori_loop(0, tm // rc, chunk, 0)


def in_proj(h, w_bf, col_blk, n_blk, kind, tm, out_dtypes, extra=()):
    m, d = h.shape
    tn = PROJ_TN
    rc = min(PROJ_RC, tm)
    in_specs = [
        pl.BlockSpec((tm, d), lambda n, i: (i, 0)),
        pl.BlockSpec((d, tn), lambda n, i: (0, col_blk + n)),
    ] + [pl.BlockSpec((1, tn), lambda n, i: (0, 0)) for _ in extra]
    out_specs = [pl.BlockSpec((tm, tn), lambda n, i: (i, n)) for _ in out_dtypes]
    out_shape = [jax.ShapeDtypeStruct((m, n_blk * tn), dt) for dt in out_dtypes]
    res = pl.pallas_call(
        functools.partial(_proj_body, kind=kind, rc=rc),
        grid=(n_blk, m // tm),
        in_specs=in_specs,
        out_specs=out_specs,
        out_shape=out_shape,
        compiler_params=_cparams(("parallel", "parallel"), 48),
        name="in_proj_" + kind,
    )(h, w_bf, *extra)
    return res


def _sgu_body(u_ref, v_ref, w_ref, bt_ref, o_ref, *, rows, n_chunks):
    ii = lax.broadcasted_iota(jnp.int32, (CHUNK, CHUNK), 0)
    jj = lax.broadcasted_iota(jnp.int32, (CHUNK, CHUNK), 1)
    causal = jj <= ii
    for c in range(n_chunks):
        r0 = c * rows
        vb = v_ref[r0:r0 + rows, :]
        if rows < CHUNK:
            vb = jnp.concatenate([vb, jnp.zeros((CHUNK - rows, vb.shape[1]), vb.dtype)], axis=0)
        vb = vb.astype(BF16)
        for g in range(N_SGU_GROUPS):
            w = jnp.where(causal, w_ref[g], 0.0).astype(BF16)
            cols = slice(g * SGU_GROUP, (g + 1) * SGU_GROUP)
            f = jnp.dot(w, vb[:, cols], preferred_element_type=F32) + bt_ref[:, g:g + 1]
            u = u_ref[r0:r0 + rows, cols].astype(F32)
            o_ref[r0:r0 + rows, cols] = (u * f[:rows]).astype(o_ref.dtype)


def sgu(u, v, w_s, b_s, rows, n_chunks, out_dtype):
    m, w = u.shape
    tm = rows * n_chunks
    return pl.pallas_call(
        functools.partial(_sgu_body, rows=rows, n_chunks=n_chunks),
        grid=(m // tm,),
        in_specs=[
            pl.BlockSpec((tm, w), lambda i: (i, 0)),
            pl.BlockSpec((tm, w), lambda i: (i, 0)),
            pl.BlockSpec((N_SGU_GROUPS, CHUNK, CHUNK), lambda i: (0, 0, 0)),
            pl.BlockSpec((CHUNK, N_SGU_GROUPS), lambda i: (0, 0)),
        ],
        out_specs=pl.BlockSpec((tm, w), lambda i: (i, 0)),
        out_shape=jax.ShapeDtypeStruct((m, w), out_dtype),
        compiler_params=_cparams(("parallel",), 32),
        name="sgu",
    )(u, v, w_s, b_s.T)


def _lambda_value(lam_ref, lam_init):
    lv = lam_ref[...]
    d1 = jnp.sum(lv[0:1] * lv[1:2], axis=-1, keepdims=True)
    d2 = jnp.sum(lv[2:3] * lv[3:4], axis=-1, keepdims=True)
    return jnp.exp(d1) - jnp.exp(d2) + lam_init


def _head_out(o0, o1, lam, g, lam_init):
    o = o0 - lam * o1
    return _rmsnorm_rows(o, g) * (1.0 - lam_init)


def _flash_body(q_ref, k_ref, v_ref, lam_ref, g_ref, slope_ref, o_ref, vaug, m_s, acc, *, tq, tk, rb, lam_init):
    qi = pl.program_id(2)
    n_rb = tq // rb
    chains = [(c, r) for c in range(2) for r in range(n_rb)]

    @pl.when(qi == 0)
    def _():
        vaug[:, 0:DV] = v_ref[...]
        vaug[:, DV:2 * DV] = jnp.ones((vaug.shape[0], DV), BF16)

    q = q_ref[...]
    lane = lax.broadcasted_iota(jnp.int32, q.shape, 1)
    zero = jnp.zeros_like(q)
    qmap = [jnp.where(lane < DK, q, zero), jnp.where(lane >= DK, q, zero)]
    slope = slope_ref[...]
    col = lax.broadcasted_iota(jnp.int32, (1, tk), 1)
    q0 = qi * tq

    m_s[...] = jnp.full(m_s.shape, NEG_BIG, F32)
    acc[...] = jnp.zeros(acc.shape, F32)

    def step(j, masked):
        k0 = pl.multiple_of(j * tk, tk)
        kj = k_ref[pl.ds(k0, tk), :]
        vj = vaug[pl.ds(k0, tk), :]
        bias = slope * (col + (k0 - q0)).astype(F32)
        for ci, (c, r) in enumerate(chains):
            rows = slice(ci * rb, (ci + 1) * rb)
            qc = qmap[c][r * rb:(r + 1) * rb]
            nk = (r + 1) * rb if masked else tk
            s = lax.dot_general(qc, kj[0:nk], (((1,), (1,)), ((), ())), preferred_element_type=F32)
            s = s + bias[:, 0:nk]
            if masked:
                rr = lax.broadcasted_iota(jnp.int32, s.shape, 0) + (r * rb)
                cc = lax.broadcasted_iota(jnp.int32, s.shape, 1)
                s = jnp.where(cc <= rr, s, NEG_BIG)
            m_old = m_s[rows, :]
            m_new = jnp.maximum(m_old, jnp.max(s, axis=-1, keepdims=True))
            alpha = jnp.exp2(m_old - m_new)
            p = jnp.exp2(s - jnp.tile(m_new, (1, nk // LANES)))
            pv = jnp.dot(p.astype(BF16), vj[0:nk], preferred_element_type=F32)
            acc[rows, :] = jnp.tile(alpha, (1, 2 * DV // LANES)) * acc[rows, :] + pv
            m_s[rows, :] = m_new

    def full_step(j, carry):
        step(j, False)
        return carry

    lax.fori_loop(0, qi, full_step, 0)
    step(qi, True)

    lam = _lambda_value(lam_ref, lam_init)
    for r in range(n_rb):
        a0 = acc[r * rb:(r + 1) * rb, :]
        a1 = acc[(n_rb + r) * rb:(n_rb + r + 1) * rb, :]
        o0 = a0[:, 0:DV] / a0[:, DV:2 * DV]
        o1 = a1[:, 0:DV] / a1[:, DV:2 * DV]
        o_ref[r * rb:(r + 1) * rb, :] = _head_out(o0, o1, lam, g_ref[...], lam_init).astype(o_ref.dtype)


def prompt_attention(q, k, v, lam_vecs, subln_g, slopes2, batch, seq, lam_init):
    tq, tk, rb = ATT_TQ, ATT_TK, ATT_RB
    assert tq == tk and seq % tq == 0 and tq % rb == 0
    nq = seq // tq
    n_chain_rows = 2 * tq
    slope_b = jnp.broadcast_to(slopes2[:, None, None], (N_HEADS, 1, tk)).astype(F32)
    return pl.pallas_call(
        functools.partial(_flash_body, tq=tq, tk=tk, rb=rb, lam_init=lam_init),
        grid=(batch, N_HEADS, nq),
        in_specs=[
            pl.BlockSpec((tq, DV), lambda b, h, i: (b * nq + i, h)),
            pl.BlockSpec((seq, DV), lambda b, h, i: (b, h)),
            pl.BlockSpec((seq, DV), lambda b, h, i: (b, h)),
            pl.BlockSpec((4, DK), lambda b, h, i: (0, 0)),
            pl.BlockSpec((1, DV), lambda b, h, i: (0, 0)),
            pl.BlockSpec((None, 1, tk), lambda b, h, i: (h, 0, 0)),
        ],
        out_specs=pl.BlockSpec((tq, DV), lambda b, h, i: (b * nq + i, h)),
        out_shape=jax.ShapeDtypeStruct((batch * seq, W_ATTN), BF16),
        scratch_shapes=[
            pltpu.VMEM((seq, 2 * DV), BF16),
            pltpu.VMEM((n_chain_rows, LANES), F32),
            pltpu.VMEM((n_chain_rows, 2 * DV), F32),
        ],
        compiler_params=_cparams(("parallel", "parallel", "arbitrary"), 40),
        name="prompt_attention",
    )(q, k, v, lam_vecs, subln_g.reshape(1, DV), slope_b)


def _decode_body(pt_ref, q_ref, kn_ref, vn_ref, *rest, pp, n_steps, t_new, past, lam_init):
    kp = rest[:pp]
    vp = rest[pp:2 * pp]
    lam_ref, g_ref, slope_ref, o_ref, qx, bias0, m_s, l_s, acc = rest[2 * pp:]
    j = pl.program_id(1)
    nt = (((1,), (1,)), ((), ()))
    nrow = 2 * N_HEADS * t_new
    ncol = PAGE_SIZE * N_HEADS
    slope = slope_ref[...]
    rep = ncol // LANES

    @pl.when(j == 0)
    def _():
        q = q_ref[...]
        lane = lax.broadcasted_iota(jnp.int32, (t_new, 2 * DK), 1)
        parts = []
        for h in range(N_HEADS):
            qh = q[:, h * 2 * DK:(h + 1) * 2 * DK]
            parts.append(jnp.where(lane < DK, qh, 0.0))
            parts.append(jnp.where(lane >= DK, qh, 0.0))
        qx[...] = jnp.concatenate(parts, axis=0).astype(BF16)
        rr = lax.broadcasted_iota(jnp.int32, (nrow, ncol), 0)
        cc = lax.broadcasted_iota(jnp.int32, (nrow, ncol), 1)
        head_ok = _mod_pow2(cc, N_HEADS) == _div_pow2(rr, 2 * t_new)
        pos = _div_pow2(cc, N_HEADS).astype(F32)
        bias0[...] = jnp.where(head_ok, jnp.tile(slope, (1, rep)) * pos, NEG_BIG)
        n_new = t_new * N_HEADS
        pad = jnp.zeros((LANES - n_new, 2 * DK), F32)
        kn = jnp.concatenate([kn_ref[...].reshape(n_new, 2 * DK), pad], axis=0).astype(BF16)
        vn = jnp.concatenate([vn_ref[...].reshape(n_new, DV), pad], axis=0).astype(BF16)
        s = lax.dot_general(qx[...], kn, nt, preferred_element_type=F32)
        r1 = lax.broadcasted_iota(jnp.int32, s.shape, 0)
        c1 = lax.broadcasted_iota(jnp.int32, s.shape, 1)
        tk = _div_pow2(c1, N_HEADS)
        ok = (c1 < n_new) & (_mod_pow2(c1, N_HEADS) == _div_pow2(r1, 2 * t_new)) & (tk <= _mod_pow2(r1, t_new))
        s = jnp.where(ok, s + slope * tk.astype(F32), NEG_BIG)
        m = jnp.max(s, axis=-1, keepdims=True)
        p = jnp.exp2(s - m)
        m_s[...] = jnp.broadcast_to(m, m_s.shape)
        l_s[...] = jnp.broadcast_to(jnp.sum(p, axis=-1, keepdims=True), l_s.shape)
        acc[...] = jnp.dot(p.astype(BF16), vn, preferred_element_type=F32)

    q2 = qx[...]
    b0 = bias0[...]
    ss = []
    for i in range(pp):
        kpage = kp[i][...].reshape(ncol, 2 * DK).astype(BF16)
        s_i = lax.dot_general(q2, kpage, nt, preferred_element_type=F32)
        off = ((j * pp + i) * PAGE_SIZE - past).astype(F32)
        ss.append(s_i + b0 + jnp.tile(slope * off, (1, rep)))
    s = jnp.concatenate(ss, axis=1)
    m_old = m_s[...]
    m_new = jnp.maximum(m_old, jnp.max(s, axis=-1, keepdims=True))
    alpha = jnp.exp2(m_old - m_new)
    p = jnp.exp2(s - jnp.tile(m_new, (1, pp * rep)))
    l_s[...] = alpha * l_s[...] + jnp.sum(p, axis=-1, keepdims=True)
    pv = None
    for i in range(pp):
        vpage = vp[i][...].reshape(ncol, DV).astype(BF16)
        d = jnp.dot(p[:, i * ncol:(i + 1) * ncol].astype(BF16), vpage, preferred_element_type=F32)
        pv = d if pv is None else pv + d
    acc[...] = alpha * acc[...] + pv
    m_s[...] = m_new

    @pl.when(j == n_steps - 1)
    def _():
        lam = _lambda_value(lam_ref, lam_init)
        on = acc[...] / l_s[...]
        for h in range(N_HEADS):
            r0 = h * 2 * t_new
            o = _head_out(on[r0:r0 + t_new], on[r0 + t_new:r0 + 2 * t_new], lam, g_ref[...], lam_init)
            o_ref[:, h * DV:(h + 1) * DV] = o.astype(o_ref.dtype)


def decode_attention(q, k_new, v_new, cache_k, cache_v, layer, page_table, lam_vecs, subln_g, slopes2, lam_init):
    bd, n_pages = page_table.shape
    t_new = q.shape[0] // bd
    past = n_pages * PAGE_SIZE
    pp = DEC_PP
    n_steps = n_pages // pp
    nrow = 2 * N_HEADS * t_new
    ncol = PAGE_SIZE * N_HEADS
    assert nrow == LANES and t_new * N_HEADS <= LANES and n_pages % pp == 0
    width = N_HEADS * DV
    slope_rows = jnp.broadcast_to(jnp.repeat(slopes2, 2 * t_new)[:, None], (nrow, LANES)).astype(F32)

    def page_spec(i):
        return pl.BlockSpec((None, None, PAGE_SIZE, N_HEADS, DV),
                            lambda b, j, pt: (layer, pt[b * n_pages + j * pp + i], 0, 0, 0))

    new_spec = pl.BlockSpec((None, t_new, N_HEADS, DV), lambda b, j, pt: (b, 0, 0, 0))
    grid_spec = pltpu.PrefetchScalarGridSpec(
        num_scalar_prefetch=1,
        grid=(bd, n_steps),
        in_specs=[pl.BlockSpec((t_new, width), lambda b, j, pt: (b, 0)), new_spec, new_spec]
        + [page_spec(i) for i in range(pp)] + [page_spec(i) for i in range(pp)] + [
            pl.BlockSpec((4, DK), lambda b, j, pt: (0, 0)),
            pl.BlockSpec((1, DV), lambda b, j, pt: (0, 0)),
            pl.BlockSpec((nrow, LANES), lambda b, j, pt: (0, 0)),
        ],
        out_specs=pl.BlockSpec((t_new, width), lambda b, j, pt: (b, 0)),
        scratch_shapes=[
            pltpu.VMEM((nrow, 2 * DK), BF16),
            pltpu.VMEM((nrow, ncol), F32),
            pltpu.VMEM((nrow, LANES), F32),
            pltpu.VMEM((nrow, LANES), F32),
            pltpu.VMEM((nrow, DV), F32),
        ],
    )
    return pl.pallas_call(
        functools.partial(_decode_body, pp=pp, n_steps=n_steps, t_new=t_new, past=past, lam_init=lam_init),
        grid_spec=grid_spec,
        out_shape=jax.ShapeDtypeStruct((bd * t_new, width), F32),
        compiler_params=_cparams(("parallel", "arbitrary"), 40),
        name="decode_attention",
    )(page_table.reshape(-1), q, k_new, v_new, *([cache_k] * pp), *([cache_v] * pp),
      lam_vecs, subln_g.reshape(1, DV), slope_rows)


def _merge_body(x_ref, ya_ref, yb_ref, ga_ref, gb_ref, wpa_ref, wpb_ref, wo_ref, g2_ref, rw_ref, rb_ref,
                x1_ref, h2_ref, lg_ref):
    a = jnp.dot(ya_ref[...].astype(BF16), wpa_ref[...], preferred_element_type=F32)
    b = jnp.dot(yb_ref[...].astype(BF16), wpb_ref[...], preferred_element_type=F32)
    mixed = ga_ref[...].astype(F32) * a + gb_ref[...].astype(F32) * b
    x1 = x_ref[...] + jnp.dot(mixed.astype(BF16), wo_ref[...], preferred_element_type=F32)
    x1_ref[...] = x1
    h2 = _rmsnorm_rows(x1, g2_ref[...])
    h2_ref[...] = h2
    h_hi = h2.astype(BF16)
    h_lo = (h2 - h_hi.astype(F32)).astype(BF16)
    hw = jnp.dot(h_hi, rw_ref[...], preferred_element_type=F32)
    lw = jnp.dot(h_lo, rw_ref[:, 0:LANES], preferred_element_type=F32)
    lg_ref[...] = hw[:, 0:LANES] + (hw[:, LANES:2 * LANES] + lw) + rb_ref[...]


def merge(x, y_a, y_b, gates, w_pa, w_pb, w_o, norm2_g, router_w_pad, router_b_pad, tm):
    m, d = x.shape
    row = lambda i: (i, 0)
    return pl.pallas_call(
        _merge_body,
        grid=(m // tm,),
        in_specs=[
            pl.BlockSpec((tm, d), row),
            pl.BlockSpec((tm, W_SGU), row),
            pl.BlockSpec((tm, W_ATTN), row),
            pl.BlockSpec((tm, d), lambda i: (i, 0)),
            pl.BlockSpec((tm, d), lambda i: (i, 1)),
            _const_spec((W_SGU, d)),
            _const_spec((W_ATTN, d)),
            _const_spec((d, d)),
            _const_spec((1, d)),
            _const_spec((d, 2 * LANES)),
            _const_spec((1, LANES)),
        ],
        out_specs=[pl.BlockSpec((tm, d), row), pl.BlockSpec((tm, d), row), pl.BlockSpec((tm, LANES), row)],
        out_shape=[
            jax.ShapeDtypeStruct((m, d), F32),
            jax.ShapeDtypeStruct((m, d), F32),
            jax.ShapeDtypeStruct((m, LANES), F32),
        ],
        compiler_params=_cparams(("parallel",), 56),
        name="merge",
    )(x, y_a, y_b, gates, gates, w_pa, w_pb, w_o, norm2_g.reshape(1, d), router_w_pad, router_b_pad)


def _row_copies(idx_ref, base, n, src_hbm, buf, slot, sem, start):
    for r in range(n):
        cp = pltpu.make_async_copy(src_hbm.at[pl.ds(idx_ref[base + r], 1)], buf.at[slot, pl.ds(r, 1)],
                                   sem.at[slot])
        if start:
            cp.start()
        else:
            cp.wait()


def _dispatch_body(valid_ref, first_ref, idx_ref, h_hbm, o_ref, buf, sem, *, sb, n_steps):
    i = pl.program_id(0)
    slot = _mod_pow2(i, 2)
    nxt = jnp.minimum(i + 1, n_steps - 1)

    @pl.when((i == 0) & (valid_ref[0] > 0))
    def _():
        _row_copies(idx_ref, first_ref[0], sb, h_hbm, buf, 0, sem, True)

    @pl.when((i + 1 < n_steps) & (valid_ref[nxt] > 0))
    def _():
        _row_copies(idx_ref, first_ref[nxt], sb, h_hbm, buf, 1 - slot, sem, True)

    @pl.when(valid_ref[i] > 0)
    def _():
        _row_copies(idx_ref, first_ref[i], sb, h_hbm, buf, slot, sem, False)
        o_ref[...] = buf[slot].astype(o_ref.dtype)

    @pl.when(valid_ref[i] == 0)
    def _():
        o_ref[...] = jnp.zeros(o_ref.shape, o_ref.dtype)


def moe_dispatch(h2, sub_valid, sub_first, sorted_tok, n_rows):
    sb = MOE_SB
    d = h2.shape[1]
    n_steps = n_rows // sb
    grid_spec = pltpu.PrefetchScalarGridSpec(
        num_scalar_prefetch=3,
        grid=(n_steps,),
        in_specs=[pl.BlockSpec(memory_space=pl.ANY)],
        out_specs=pl.BlockSpec((sb, d), lambda i, va, ci, idx: (i, 0)),
        scratch_shapes=[pltpu.VMEM((2, sb, d), F32), pltpu.SemaphoreType.DMA((2,))],
    )
    return pl.pallas_call(
        functools.partial(_dispatch_body, sb=sb, n_steps=n_steps),
        grid_spec=grid_spec,
        out_shape=jax.ShapeDtypeStruct((n_rows, d), BF16),
        compiler_params=_cparams(("arbitrary",), 32),
        name="moe_dispatch",
    )(sub_valid, sub_first, sorted_tok, h2)


def _combine_body(pos_ref, x1_ref, g_ref, fg_ref, y_hbm, o_ref, buf, sem, *, tt, n_steps):
    i = pl.program_id(0)
    n = tt * TOP_K
    slot = _mod_pow2(i, 2)

    @pl.when(i == 0)
    def _():
        _row_copies(pos_ref, 0, n, y_hbm, buf, 0, sem, True)

    @pl.when(i + 1 < n_steps)
    def _():
        _row_copies(pos_ref, (i + 1) * n, n, y_hbm, buf, 1 - slot, sem, True)

    _row_copies(pos_ref, i * n, n, y_hbm, buf, slot, sem, False)
    rows = buf[slot]
    g = g_ref[...]
    moe = g[:, 0:1] * rows[0:tt]
    for k in range(1, TOP_K):
        moe = moe + g[:, k:k + 1] * rows[k * tt:(k + 1) * tt]
    o_ref[...] = _rmsnorm_rows(x1_ref[...] + moe, fg_ref[...])


def moe_combine_final(x1, pos, gates, y_rows, final_g):
    m, d = x1.shape
    tt = min(COMBINE_TT, m)
    n_steps = m // tt
    pos_km = pos.reshape(n_steps, tt, TOP_K).transpose(0, 2, 1).reshape(-1)
    grid_spec = pltpu.PrefetchScalarGridSpec(
        num_scalar_prefetch=1,
        grid=(n_steps,),
        in_specs=[
            pl.BlockSpec((tt, d), lambda i, p: (i, 0)),
            pl.BlockSpec((tt, TOP_K), lambda i, p: (i, 0)),
            pl.BlockSpec((1, d), lambda i, p: (0, 0)),
            pl.BlockSpec(memory_space=pl.ANY),
        ],
        out_specs=pl.BlockSpec((tt, d), lambda i, p: (i, 0)),
        scratch_shapes=[pltpu.VMEM((2, tt * TOP_K, d), F32), pltpu.SemaphoreType.DMA((2,))],
    )
    return pl.pallas_call(
        functools.partial(_combine_body, tt=tt, n_steps=n_steps),
        grid_spec=grid_spec,
        out_shape=jax.ShapeDtypeStruct((m, d), F32),
        compiler_params=_cparams(("arbitrary",), 32),
        name="moe_combine_final",
    )(pos_km, x1, gates, final_g.reshape(1, d), y_rows)


def _moe_body(we_ref, wb_ref, wn_ref, x_ref, wg_ref, wl_ref, bg_ref, bl_ref, wd_ref, bd_ref, y_ref,
              wgl_bf, wd_bf, act_s, *, unit, tf, nf):
    w = pl.program_id(0)
    s = pl.program_id(1)
    n_u = wn_ref[w]
    rb = y_ref.shape[0]

    def rows_at(off, n):
        return pl.ds(pl.multiple_of(off, unit), n)

    def for_each_chunk(fn):
        n4 = lax.shift_right_logical(n_u, 2)

        def big(i, carry):
            fn(i * (4 * unit), 4 * unit)
            return carry

        lax.fori_loop(0, n4, big, 0)
        off2 = n4 * (4 * unit)

        @pl.when(jnp.bitwise_and(n_u, 2) != 0)
        def _():
            fn(off2, 2 * unit)

        @pl.when(jnp.bitwise_and(n_u, 1) != 0)
        def _():
            fn(off2 + jnp.bitwise_and(n_u, 2) * unit, unit)

    @pl.when((s < nf) & (n_u > 0))
    def _():
        wgl_bf[:, 0:tf] = wg_ref[...].astype(BF16)
        wgl_bf[:, tf:2 * tf] = wl_ref[...].astype(BF16)
        bgl = jnp.concatenate([bg_ref[...], bl_ref[...]], axis=1)

        def chunk(off, n):
            rows = rows_at(off, n)
            gl = jnp.dot(x_ref[rows, :], wgl_bf[...], preferred_element_type=F32) + bgl
            glu = jnp.minimum(gl[:, 0:tf], SWIGLU_LIMIT)
            lin = jnp.clip(gl[:, tf:2 * tf], -SWIGLU_LIMIT, SWIGLU_LIMIT)
            act = glu * jax.nn.sigmoid(SWIGLU_ALPHA * glu) * (lin + 1.0)
            act_s[s, rows, :] = act.astype(BF16)

        for_each_chunk(chunk)

    @pl.when(s >= nf)
    def _():
        @pl.when(n_u > 0)
        def _():
            wd_bf[...] = wd_ref[...].astype(BF16)

            def chunk(off, n):
                rows = rows_at(off, n)
                a = jnp.concatenate([act_s[k, rows, :] for k in range(nf)], axis=1)
                y_ref[rows, :] = jnp.dot(a, wd_bf[...], preferred_element_type=F32) + bd_ref[...]

            for_each_chunk(chunk)

        def clear(i, carry):
            y_ref[rows_at(i * unit, unit), :] = jnp.zeros((unit, tf), F32)
            return carry

        lax.fori_loop(n_u, rb // unit, clear, 0)


def moe_experts(x_rows, work_e, work_blk, work_nsub, w_gu, b_gu, w_down, b_down):
    n_rows, d = x_rows.shape
    rb, tf = MOE_RB, MOE_TF
    nw = n_rows // rb
    nf = D_FF // tf
    assert d % tf == 0 and d // tf == nf

    def up(s):
        return jnp.minimum(s, nf - 1)

    def down(s):
        return jnp.maximum(s - nf, 0)

    grid_spec = pltpu.PrefetchScalarGridSpec(
        num_scalar_prefetch=3,
        grid=(nw, 2 * nf),
        in_specs=[
            pl.BlockSpec((rb, d), lambda w, s, we, wb, wn: (wb[w], 0), pipeline_mode=pl.Buffered(1)),
            pl.BlockSpec((None, d, tf), lambda w, s, we, wb, wn: (we[w], 0, up(s))),
            pl.BlockSpec((None, d, tf), lambda w, s, we, wb, wn: (we[w], 0, nf + up(s))),
            pl.BlockSpec((None, 1, tf), lambda w, s, we, wb, wn: (we[w], 0, up(s))),
            pl.BlockSpec((None, 1, tf), lambda w, s, we, wb, wn: (we[w], 0, nf + up(s))),
            pl.BlockSpec((None, D_FF, tf), lambda w, s, we, wb, wn: (we[w], 0, down(s))),
            pl.BlockSpec((None, 1, tf), lambda w, s, we, wb, wn: (we[w], 0, down(s))),
        ],
        out_specs=pl.BlockSpec((rb, tf), lambda w, s, we, wb, wn: (w, down(s))),
        scratch_shapes=[
            pltpu.VMEM((d, 2 * tf), BF16),
            pltpu.VMEM((D_FF, tf), BF16),
            pltpu.VMEM((nf, rb, tf), BF16),
        ],
    )
    return pl.pallas_call(
        functools.partial(_moe_body, unit=MOE_UNIT, tf=tf, nf=nf),
        grid_spec=grid_spec,
        out_shape=jax.ShapeDtypeStruct((n_rows, d), F32),
        compiler_params=_cparams(("arbitrary", "arbitrary"), 58),
        name="moe_experts",
    )(work_e, work_blk, work_nsub, x_rows, w_gu, w_gu, b_gu.reshape(N_EXPERTS, 1, 2 * D_FF),
      b_gu.reshape(N_EXPERTS, 1, 2 * D_FF), w_down, b_down.reshape(N_EXPERTS, 1, d))


def _routing(logits):
    n_tok = logits.shape[0]
    n_assign = n_tok * TOP_K
    rb, sb, unit = MOE_RB, MOE_SB, MOE_UNIT
    spw = rb // sb
    nw = n_assign // rb + N_EXPERTS
    top_logits, top_idx = lax.top_k(logits, TOP_K)
    gates = jax.nn.softmax(top_logits, axis=-1)
    flat_e = top_idx.reshape(-1).astype(jnp.int32)
    onehot = (flat_e[:, None] == jnp.arange(N_EXPERTS, dtype=jnp.int32)[None, :]).astype(jnp.int32)
    csum = jnp.cumsum(onehot, axis=0)
    rank = jnp.take_along_axis(csum, flat_e[:, None], axis=1)[:, 0] - 1
    counts = csum[-1]
    nwe = (counts + rb - 1) // rb
    w_end = jnp.cumsum(nwe)
    w_start = w_end - nwe
    dest = w_start[flat_e] * rb + rank
    widx = jnp.arange(nw, dtype=jnp.int32)
    n_used = w_end[-1]
    valid = widx < n_used
    we = jnp.minimum(jnp.searchsorted(w_end, widx, side="right"), N_EXPERTS - 1).astype(jnp.int32)
    rows_valid = jnp.clip(counts[we] - (widx - w_start[we]) * rb, 0, rb)
    n_units = jnp.where(valid, (rows_valid + unit - 1) // unit, 0).astype(jnp.int32)
    last = jnp.maximum(n_used - 1, 0)
    work_e = jnp.where(valid, we, we[last]).astype(jnp.int32)
    work_blk = jnp.where(valid, widx, last).astype(jnp.int32)
    jblk = jnp.arange(spw, dtype=jnp.int32)[None, :]
    sub_valid = (jblk * (sb // unit) < n_units[:, None]).astype(jnp.int32).reshape(-1)
    a_start = jnp.cumsum(counts) - counts
    first_w = a_start[we] + (widx - w_start[we]) * rb
    sub_first = jnp.clip(first_w[:, None] + jblk * sb, 0, n_assign).astype(jnp.int32).reshape(-1)
    sorted_a = jnp.argsort(flat_e, stable=True).astype(jnp.int32)
    sorted_tok = jnp.concatenate([sorted_a // TOP_K, jnp.zeros((sb,), jnp.int32)])
    return gates, dest, work_e, work_blk, n_units, nw, sub_valid, sub_first, sorted_tok


def _layer_group(x2d, tm, w_in_bf, p, act_dtype):
    h = rmsnorm_to(x2d, p["norm1_g"], tm, BF16)
    (u,) = in_proj(h, w_in_bf, 0, 1, "gelu", tm, [act_dtype])
    (v,) = in_proj(h, w_in_bf, 1, 1, "gelu_ln", tm, [F32],
                   extra=(p["sgu_ln_g"].reshape(1, -1), p["sgu_ln_b"].reshape(1, -1)))
    (q,) = in_proj(h, w_in_bf, 2, 1, "scale", tm, [act_dtype])
    k32, kbf = in_proj(h, w_in_bf, 3, 1, "dual", tm, [F32, BF16])
    va32, vabf = in_proj(h, w_in_bf, 4, 1, "dual", tm, [F32, BF16])
    (gates,) = in_proj(h, w_in_bf, 5, 4, "sigmoid", tm, [BF16])
    return u, v, q, k32, kbf, va32, vabf, gates


def kernel(x_prompt, x_sample, cache_k, cache_v, page_table, norm1_g, w_in, sgu_ln_g, sgu_ln_b, sgu_w, sgu_b,
           lambda_q1, lambda_k1, lambda_q2, lambda_k2, subln_g, w_branch_a, w_branch_b, w_out, norm2_g,
           router_w, router_b, expert_w_gu, expert_b_gu, expert_w_down, expert_b_down, final_norm_g):
    depth = w_in.shape[0]
    assert depth == 1
    l = 0
    bp, sp, d = x_prompt.shape
    bs, ts, _ = x_sample.shape
    lam_init = 0.8 - 0.6 * math.exp(-0.3 * l)
    slopes2 = jnp.exp2(-(8.0 / N_HEADS) * jnp.arange(1, N_HEADS + 1, dtype=F32)) * LOG2E
    lam_vecs = jnp.stack([lambda_q1[l], lambda_k1[l], lambda_q2[l], lambda_k2[l]]).astype(F32)

    w_in_bf = w_in[l].astype(BF16)
    w_pa = w_branch_a[l].astype(BF16)
    w_pb = w_branch_b[l].astype(BF16)
    w_o = w_out[l].astype(BF16)
    rw32 = jnp.pad(router_w[l].astype(F32), ((0, 0), (0, LANES - N_EXPERTS)))
    rw_hi = rw32.astype(BF16)
    rw_pad = jnp.concatenate([rw_hi, (rw32 - rw_hi.astype(F32)).astype(BF16)], axis=1)
    rb_pad = jnp.pad(router_b[l].astype(F32), (0, LANES - N_EXPERTS)).reshape(1, LANES)
    p = {"norm1_g": norm1_g[l], "sgu_ln_g": sgu_ln_g[l], "sgu_ln_b": sgu_ln_b[l]}

    xp = x_prompt.reshape(bp * sp, d)
    u, v, q, k32p, kbf, va32p, vabf, gates = _layer_group(xp, 512, w_in_bf, p, BF16)
    y_a = sgu(u, v, sgu_w[l], sgu_b[l], CHUNK, 4, BF16)
    y_b = prompt_attention(q, kbf, vabf, lam_vecs, subln_g[l], slopes2, bp, sp, lam_init)
    x1p, h2p, lgp = merge(xp, y_a, y_b, gates, w_pa, w_pb, w_o, norm2_g[l], rw_pad, rb_pad, MERGE_TM)

    xs = x_sample.reshape(bs * ts, d)
    u, v_s, q, k32s, _, va32s, _, gates = _layer_group(xs, bs * ts, w_in_bf, p, F32)
    y_a = sgu(u, v_s, sgu_w[l], sgu_b[l], ts, bs, F32)
    k_sample = k32s.reshape(bs, ts, N_HEADS, 2 * DK)
    v_sample = va32s.reshape(bs, ts, N_HEADS, DV)
    y_b = decode_attention(q, k_sample, v_sample, cache_k, cache_v, l, page_table,
                           lam_vecs, subln_g[l], slopes2, lam_init)
    x1s, h2s, lgs = merge(xs, y_a, y_b, gates, w_pa, w_pb, w_o, norm2_g[l], rw_pad, rb_pad, bs * ts)

    h2 = jnp.concatenate([h2p, h2s], axis=0)
    logits = jnp.concatenate([lgp[:, :N_EXPERTS], lgs[:, :N_EXPERTS]], axis=0)
    gate_w, dest, work_e, work_blk, work_units, nw, sub_valid, sub_first, sorted_tok = _routing(logits)
    x_rows = moe_dispatch(h2, sub_valid, sub_first, sorted_tok, nw * MOE_RB)
    y_rows = moe_experts(x_rows, work_e, work_blk, work_units, expert_w_gu[l], expert_b_gu[l],
                         expert_w_down[l], expert_b_down[l])
    n_p = bp * sp
    pos = dest.reshape(-1, TOP_K)
    y_prompt = moe_combine_final(x1p, pos[:n_p], gate_w[:n_p], y_rows, final_norm_g).reshape(bp, sp, d)
    y_sample = moe_combine_final(x1s, pos[n_p:], gate_w[n_p:], y_rows, final_norm_g).reshape(bs, ts, d)

    k_prompt = k32p.reshape(1, bp, sp, N_HEADS, 2 * DK)
    v_prompt = va32p.reshape(1, bp, sp, N_HEADS, DV)
    state_sgu_v = v_s.reshape(1, bs, ts, W_SGU)
    return (y_prompt, y_sample, k_prompt, v_prompt, k_sample[None], v_sample[None], state_sgu_v)
```

```python
import functools
import math

import jax
import jax.numpy as jnp
from jax import lax
from jax.experimental import pallas as pl
from jax.experimental.pallas import tpu as pltpu

F32 = jnp.float32
BF16 = jnp.bfloat16

D_MODEL = 2048
N_HEADS = 8
DK = 64
DV = 2 * DK
W_ATTN = N_HEADS * DV
D_QK = 2 * N_HEADS * DK
W_SGU = D_MODEL // 2
N_SGU_GROUPS = 8
SGU_GROUP = W_SGU // N_SGU_GROUPS
CHUNK = 128
N_EXPERTS = 32
TOP_K = 4
D_FF = D_MODEL
SWIGLU_LIMIT = 7.0
SWIGLU_ALPHA = 1.702
EPS = 1e-6
PAGE_SIZE = 128
D_IN = 2 * W_SGU + 2 * D_QK + W_ATTN + 2 * D_MODEL
LOG2E = math.log2(math.e)

LANES = 128
SUBLANES = 8
MIB = 1024 * 1024

PROJ_TN = 1024
PROJ_RC = 256
ATT_TQ = 512
ATT_TK = 512
ATT_RB = 256
DEC_PP = 8
MERGE_TM = 256
MOE_RB = 1280
MOE_SB = 256
MOE_UNIT = 128
MOE_TF = 512
MOE_TN = 256
COMBINE_TT = 64
NEG_BIG = -1e30


def _cparams(semantics, vmem_mib):
    return pltpu.CompilerParams(dimension_semantics=semantics, vmem_limit_bytes=vmem_mib * MIB)


def _const_spec(shape):
    nd = len(shape)
    return pl.BlockSpec(shape, lambda *_: (0,) * nd, pipeline_mode=pl.Buffered(1))


def _div_pow2(x, n):
    assert n & (n - 1) == 0
    return lax.shift_right_logical(x, n.bit_length() - 1)


def _mod_pow2(x, n):
    assert n & (n - 1) == 0
    return jnp.bitwise_and(x, n - 1)


def _rmsnorm_rows(x, g):
    ms = jnp.mean(x * x, axis=-1, keepdims=True)
    return x * lax.rsqrt(ms + EPS) * g


def _rmsnorm_body(x_ref, g_ref, o_ref):
    o_ref[...] = _rmsnorm_rows(x_ref[...], g_ref[...]).astype(o_ref.dtype)


def rmsnorm_to(x2d, g, tm, dtype):
    m, d = x2d.shape
    return pl.pallas_call(
        _rmsnorm_body,
        grid=(m // tm,),
        in_specs=[pl.BlockSpec((tm, d), lambda i: (i, 0)), pl.BlockSpec((1, d), lambda i: (0, 0))],
        out_specs=pl.BlockSpec((tm, d), lambda i: (i, 0)),
        out_shape=jax.ShapeDtypeStruct((m, d), dtype),
        compiler_params=_cparams(("parallel",), 32),
        name="rmsnorm",
    )(x2d, g.reshape(1, d))


def _gelu(z):
    return 0.5 * z * (1.0 + lax.erf(z * (2.0 ** -0.5)))


def _proj_body(h_ref, w_ref, *refs, kind, rc):
    tm = h_ref.shape[0]

    def chunk(c, carry):
        r = pl.multiple_of(c * rc, rc)
        rows = pl.ds(r, rc)
        z = jnp.dot(h_ref[rows, :], w_ref[...], preferred_element_type=F32)
        if kind == "gelu":
            refs[0][rows, :] = _gelu(z).astype(refs[0].dtype)
        elif kind == "gelu_ln":
            g_ref, b_ref, o_ref = refs
            a = _gelu(z)
            mu = jnp.mean(a, axis=-1, keepdims=True)
            var = jnp.mean(jnp.square(a - mu), axis=-1, keepdims=True)
            y = (a - mu) * lax.rsqrt(var + EPS)
            o_ref[rows, :] = (y * g_ref[...] + b_ref[...]).astype(o_ref.dtype)
        elif kind == "scale":
            refs[0][rows, :] = (z * (DK ** -0.5 * LOG2E)).astype(refs[0].dtype)
        elif kind == "dual":
            refs[0][rows, :] = z
            refs[1][rows, :] = z.astype(refs[1].dtype)
        elif kind == "sigmoid":
            refs[0][rows, :] = jax.nn.sigmoid(z).astype(refs[0].dtype)
        return carry

    lax.fori_loop(0, tm // rc, chunk, 0)


def in_proj(h, w_bf, col_blk, n_blk, kind, tm, out_dtypes, extra=()):
    m, d = h.shape
    tn = PROJ_TN
    rc = min(PROJ_RC, tm)
    in_specs = [
        pl.BlockSpec((tm, d), lambda n, i: (i, 0)),
        pl.BlockSpec((d, tn), lambda n, i: (0, col_blk + n)),
    ] + [pl.BlockSpec((1, tn), lambda n, i: (0, 0)) for _ in extra]
    out_specs = [pl.BlockSpec((tm, tn), lambda n, i: (i, n)) for _ in out_dtypes]
    out_shape = [jax.ShapeDtypeStruct((m, n_blk * tn), dt) for dt in out_dtypes]
    res = pl.pallas_call(
        functools.partial(_proj_body, kind=kind, rc=rc),
        grid=(n_blk, m // tm),
        in_specs=in_specs,
        out_specs=out_specs,
        out_shape=out_shape,
        compiler_params=_cparams(("parallel", "parallel"), 48),
        name="in_proj_" + kind,
    )(h, w_bf, *extra)
    return res


def _sgu_body(u_ref, v_ref, w_ref, bt_ref, o_ref, *, rows, n_chunks):
    ii = lax.broadcasted_iota(jnp.int32, (CHUNK, CHUNK), 0)
    jj = lax.broadcasted_iota(jnp.int32, (CHUNK, CHUNK), 1)
    causal = jj <= ii
    for c in range(n_chunks):
        r0 = c * rows
        vb = v_ref[r0:r0 + rows, :]
        if rows < CHUNK:
            vb = jnp.concatenate([vb, jnp.zeros((CHUNK - rows, vb.shape[1]), vb.dtype)], axis=0)
        vb = vb.astype(BF16)
        for g in range(N_SGU_GROUPS):
            w = jnp.where(causal, w_ref[g], 0.0).astype(BF16)
            cols = slice(g * SGU_GROUP, (g + 1) * SGU_GROUP)
            f = jnp.dot(w, vb[:, cols], preferred_element_type=F32) + bt_ref[:, g:g + 1]
            u = u_ref[r0:r0 + rows, cols].astype(F32)
            o_ref[r0:r0 + rows, cols] = (u * f[:rows]).astype(o_ref.dtype)


def sgu(u, v, w_s, b_s, rows, n_chunks, out_dtype):
    m, w = u.shape
    tm = rows * n_chunks
    return pl.pallas_call(
        functools.partial(_sgu_body, rows=rows, n_chunks=n_chunks),
        grid=(m // tm,),
        in_specs=[
            pl.BlockSpec((tm, w), lambda i: (i, 0)),
            pl.BlockSpec((tm, w), lambda i: (i, 0)),
            pl.BlockSpec((N_SGU_GROUPS, CHUNK, CHUNK), lambda i: (0, 0, 0)),
            pl.BlockSpec((CHUNK, N_SGU_GROUPS), lambda i: (0, 0)),
        ],
        out_specs=pl.BlockSpec((tm, w), lambda i: (i, 0)),
        out_shape=jax.ShapeDtypeStruct((m, w), out_dtype),
        compiler_params=_cparams(("parallel",), 32),
        name="sgu",
    )(u, v, w_s, b_s.T)


def _lambda_value(lam_ref, lam_init):
    lv = lam_ref[...]
    d1 = jnp.sum(lv[0:1] * lv[1:2], axis=-1, keepdims=True)
    d2 = jnp.sum(lv[2:3] * lv[3:4], axis=-1, keepdims=True)
    return jnp.exp(d1) - jnp.exp(d2) + lam_init


def _head_out(o0, o1, lam, g, lam_init):
    o = o0 - lam * o1
    return _rmsnorm_rows(o, g) * (1.0 - lam_init)


def _flash_body(q_ref, k_ref, v_ref, lam_ref, g_ref, slope_ref, o_ref, vaug, m_s, acc, *, tq, tk, rb, lam_init):
    qi = pl.program_id(2)
    n_rb = tq // rb
    chains = [(c, r) for c in range(2) for r in range(n_rb)]

    @pl.when(qi == 0)
    def _():
        vaug[:, 0:DV] = v_ref[...]
        vaug[:, DV:2 * DV] = jnp.ones((vaug.shape[0], DV), BF16)

    q = q_ref[...]
    lane = lax.broadcasted_iota(jnp.int32, q.shape, 1)
    zero = jnp.zeros_like(q)
    qmap = [jnp.where(lane < DK, q, zero), jnp.where(lane >= DK, q, zero)]
    slope = slope_ref[...]
    col = lax.broadcasted_iota(jnp.int32, (1, tk), 1)
    q0 = qi * tq

    m_s[...] = jnp.full(m_s.shape, NEG_BIG, F32)
    acc[...] = jnp.zeros(acc.shape, F32)

    def step(j, masked):
        k0 = pl.multiple_of(j * tk, tk)
        kj = k_ref[pl.ds(k0, tk), :]
        vj = vaug[pl.ds(k0, tk), :]
        bias = slope * (col + (k0 - q0)).astype(F32)
        for ci, (c, r) in enumerate(chains):
            rows = slice(ci * rb, (ci + 1) * rb)
            qc = qmap[c][r * rb:(r + 1) * rb]
            nk = (r + 1) * rb if masked else tk
            s = lax.dot_general(qc, kj[0:nk], (((1,), (1,)), ((), ())), preferred_element_type=F32)
            s = s + bias[:, 0:nk]
            if masked:
                rr = lax.broadcasted_iota(jnp.int32, s.shape, 0) + (r * rb)
                cc = lax.broadcasted_iota(jnp.int32, s.shape, 1)
                s = jnp.where(cc <= rr, s, NEG_BIG)
            m_old = m_s[rows, :]
            m_new = jnp.maximum(m_old, jnp.max(s, axis=-1, keepdims=True))
            alpha = jnp.exp2(m_old - m_new)
            p = jnp.exp2(s - jnp.tile(m_new, (1, nk // LANES)))
            pv = jnp.dot(p.astype(BF16), vj[0:nk], preferred_element_type=F32)
            acc[rows, :] = jnp.tile(alpha, (1, 2 * DV // LANES)) * acc[rows, :] + pv
            m_s[rows, :] = m_new

    def full_step(j, carry):
        step(j, False)
        return carry

    lax.fori_loop(0, qi, full_step, 0)
    step(qi, True)

    lam = _lambda_value(lam_ref, lam_init)
    for r in range(n_rb):
        a0 = acc[r * rb:(r + 1) * rb, :]
        a1 = acc[(n_rb + r) * rb:(n_rb + r + 1) * rb, :]
        o0 = a0[:, 0:DV] / a0[:, DV:2 * DV]
        o1 = a1[:, 0:DV] / a1[:, DV:2 * DV]
        o_ref[r * rb:(r + 1) * rb, :] = _head_out(o0, o1, lam, g_ref[...], lam_init).astype(o_ref.dtype)


def prompt_attention(q, k, v, lam_vecs, subln_g, slopes2, batch, seq, lam_init):
    tq, tk, rb = ATT_TQ, ATT_TK, ATT_RB
    assert tq == tk and seq % tq == 0 and tq % rb == 0
    nq = seq // tq
    n_chain_rows = 2 * tq
    slope_b = jnp.broadcast_to(slopes2[:, None, None], (N_HEADS, 1, tk)).astype(F32)
    return pl.pallas_call(
        functools.partial(_flash_body, tq=tq, tk=tk, rb=rb, lam_init=lam_init),
        grid=(batch, N_HEADS, nq),
        in_specs=[
            pl.BlockSpec((tq, DV), lambda b, h, i: (b * nq + i, h)),
            pl.BlockSpec((seq, DV), lambda b, h, i: (b, h)),
            pl.BlockSpec((seq, DV), lambda b, h, i: (b, h)),
            pl.BlockSpec((4, DK), lambda b, h, i: (0, 0)),
            pl.BlockSpec((1, DV), lambda b, h, i: (0, 0)),
            pl.BlockSpec((None, 1, tk), lambda b, h, i: (h, 0, 0)),
        ],
        out_specs=pl.BlockSpec((tq, DV), lambda b, h, i: (b * nq + i, h)),
        out_shape=jax.ShapeDtypeStruct((batch * seq, W_ATTN), BF16),
        scratch_shapes=[
            pltpu.VMEM((seq, 2 * DV), BF16),
            pltpu.VMEM((n_chain_rows, LANES), F32),
            pltpu.VMEM((n_chain_rows, 2 * DV), F32),
        ],
        compiler_params=_cparams(("parallel", "parallel", "arbitrary"), 40),
        name="prompt_attention",
    )(q, k, v, lam_vecs, subln_g.reshape(1, DV), slope_b)


def _decode_body(pt_ref, q_ref, kn_ref, vn_ref, *rest, pp, n_steps, t_new, past, lam_init):
    kp = rest[:pp]
    vp = rest[pp:2 * pp]
    lam_ref, g_ref, slope_ref, o_ref, qx, bias0, m_s, l_s, acc = rest[2 * pp:]
    j = pl.program_id(1)
    nt = (((1,), (1,)), ((), ()))
    nrow = 2 * N_HEADS * t_new
    ncol = PAGE_SIZE * N_HEADS
    slope = slope_ref[...]
    rep = ncol // LANES

    @pl.when(j == 0)
    def _():
        q = q_ref[...]
        lane = lax.broadcasted_iota(jnp.int32, (t_new, 2 * DK), 1)
        parts = []
        for h in range(N_HEADS):
            qh = q[:, h * 2 * DK:(h + 1) * 2 * DK]
            parts.append(jnp.where(lane < DK, qh, 0.0))
            parts.append(jnp.where(lane >= DK, qh, 0.0))
        qx[...] = jnp.concatenate(parts, axis=0).astype(BF16)
        rr = lax.broadcasted_iota(jnp.int32, (nrow, ncol), 0)
        cc = lax.broadcasted_iota(jnp.int32, (nrow, ncol), 1)
        head_ok = _mod_pow2(cc, N_HEADS) == _div_pow2(rr, 2 * t_new)
        pos = _div_pow2(cc, N_HEADS).astype(F32)
        bias0[...] = jnp.where(head_ok, jnp.tile(slope, (1, rep)) * pos, NEG_BIG)
        n_new = t_new * N_HEADS
        pad = jnp.zeros((LANES - n_new, 2 * DK), F32)
        kn = jnp.concatenate([kn_ref[...].reshape(n_new, 2 * DK), pad], axis=0).astype(BF16)
        vn = jnp.concatenate([vn_ref[...].reshape(n_new, DV), pad], axis=0).astype(BF16)
        s = lax.dot_general(qx[...], kn, nt, preferred_element_type=F32)
        r1 = lax.broadcasted_iota(jnp.int32, s.shape, 0)
        c1 = lax.broadcasted_iota(jnp.int32, s.shape, 1)
        tk = _div_pow2(c1, N_HEADS)
        ok = (c1 < n_new) & (_mod_pow2(c1, N_HEADS) == _div_pow2(r1, 2 * t_new)) & (tk <= _mod_pow2(r1, t_new))
        s = jnp.where(ok, s + slope * tk.astype(F32), NEG_BIG)
        m = jnp.max(s, axis=-1, keepdims=True)
        p = jnp.exp2(s - m)
        m_s[...] = jnp.broadcast_to(m, m_s.shape)
        l_s[...] = jnp.broadcast_to(jnp.sum(p, axis=-1, keepdims=True), l_s.shape)
        acc[...] = jnp.dot(p.astype(BF16), vn, preferred_element_type=F32)

    q2 = qx[...]
    b0 = bias0[...]
    ts, offs = [], []
    m_old = m_s[...]
    m_new = m_old
    for i in range(pp):
        kpage = kp[i][...].reshape(ncol, 2 * DK).astype(BF16)
        t_i = lax.dot_general(q2, kpage, nt, preferred_element_type=F32) + b0
        off_i = slope * ((j * pp + i) * PAGE_SIZE - past).astype(F32)
        m_new = jnp.maximum(m_new, jnp.max(t_i, axis=-1, keepdims=True) + off_i)
        ts.append(t_i)
        offs.append(off_i)
    alpha = jnp.exp2(m_old - m_new)
    lsum = None
    pv = None
    for i in range(pp):
        p_i = jnp.exp2(ts[i] - jnp.tile(m_new - offs[i], (1, rep)))
        vpage = vp[i][...].reshape(ncol, DV).astype(BF16)
        d = jnp.dot(p_i.astype(BF16), vpage, preferred_element_type=F32)
        r = jnp.sum(p_i, axis=-1, keepdims=True)
        pv = d if pv is None else pv + d
        lsum = r if lsum is None else lsum + r
    l_s[...] = alpha * l_s[...] + lsum
    acc[...] = alpha * acc[...] + pv
    m_s[...] = m_new

    @pl.when(j == n_steps - 1)
    def _():
        lam = _lambda_value(lam_ref, lam_init)
        on = acc[...] / l_s[...]
        for h in range(N_HEADS):
            r0 = h * 2 * t_new
            o = _head_out(on[r0:r0 + t_new], on[r0 + t_new:r0 + 2 * t_new], lam, g_ref[...], lam_init)
            o_ref[:, h * DV:(h + 1) * DV] = o.astype(o_ref.dtype)


def decode_attention(q, k_new, v_new, cache_k, cache_v, layer, page_table, lam_vecs, subln_g, slopes2, lam_init):
    bd, n_pages = page_table.shape
    t_new = q.shape[0] // bd
    past = n_pages * PAGE_SIZE
    pp = DEC_PP
    n_steps = n_pages // pp
    nrow = 2 * N_HEADS * t_new
    ncol = PAGE_SIZE * N_HEADS
    assert nrow == LANES and t_new * N_HEADS <= LANES and n_pages % pp == 0
    width = N_HEADS * DV
    slope_rows = jnp.broadcast_to(jnp.repeat(slopes2, 2 * t_new)[:, None], (nrow, LANES)).astype(F32)

    def page_spec(i):
        return pl.BlockSpec((None, None, PAGE_SIZE, N_HEADS, DV),
                            lambda b, j, pt: (layer, pt[b * n_pages + j * pp + i], 0, 0, 0))

    new_spec = pl.BlockSpec((None, t_new, N_HEADS, DV), lambda b, j, pt: (b, 0, 0, 0))
    grid_spec = pltpu.PrefetchScalarGridSpec(
        num_scalar_prefetch=1,
        grid=(bd, n_steps),
        in_specs=[pl.BlockSpec((t_new, width), lambda b, j, pt: (b, 0)), new_spec, new_spec]
        + [page_spec(i) for i in range(pp)] + [page_spec(i) for i in range(pp)] + [
            pl.BlockSpec((4, DK), lambda b, j, pt: (0, 0)),
            pl.BlockSpec((1, DV), lambda b, j, pt: (0, 0)),
            pl.BlockSpec((nrow, LANES), lambda b, j, pt: (0, 0)),
        ],
        out_specs=pl.BlockSpec((t_new, width), lambda b, j, pt: (b, 0)),
        scratch_shapes=[
            pltpu.VMEM((nrow, 2 * DK), BF16),
            pltpu.VMEM((nrow, ncol), F32),
            pltpu.VMEM((nrow, LANES), F32),
            pltpu.VMEM((nrow, LANES), F32),
            pltpu.VMEM((nrow, DV), F32),
        ],
    )
    return pl.pallas_call(
        functools.partial(_decode_body, pp=pp, n_steps=n_steps, t_new=t_new, past=past, lam_init=lam_init),
        grid_spec=grid_spec,
        out_shape=jax.ShapeDtypeStruct((bd * t_new, width), F32),
        compiler_params=_cparams(("parallel", "arbitrary"), 40),
        name="decode_attention",
    )(page_table.reshape(-1), q, k_new, v_new, *([cache_k] * pp), *([cache_v] * pp),
      lam_vecs, subln_g.reshape(1, DV), slope_rows)


def _merge_body(x_ref, ya_ref, yb_ref, ga_ref, gb_ref, wpa_ref, wpb_ref, wo_ref, g2_ref, rw_ref, rb_ref,
                x1_ref, h2_ref, lg_ref):
    a = jnp.dot(ya_ref[...].astype(BF16), wpa_ref[...], preferred_element_type=F32)
    b = jnp.dot(yb_ref[...].astype(BF16), wpb_ref[...], preferred_element_type=F32)
    mixed = ga_ref[...].astype(F32) * a + gb_ref[...].astype(F32) * b
    x1 = x_ref[...] + jnp.dot(mixed.astype(BF16), wo_ref[...], preferred_element_type=F32)
    x1_ref[...] = x1
    h2 = _rmsnorm_rows(x1, g2_ref[...])
    h2_ref[...] = h2
    h_hi = h2.astype(BF16)
    h_lo = (h2 - h_hi.astype(F32)).astype(BF16)
    hw = jnp.dot(h_hi, rw_ref[...], preferred_element_type=F32)
    lw = jnp.dot(h_lo, rw_ref[:, 0:LANES], preferred_element_type=F32)
    lg_ref[...] = hw[:, 0:LANES] + (hw[:, LANES:2 * LANES] + lw) + rb_ref[...]


def merge(x, y_a, y_b, gates, w_pa, w_pb, w_o, norm2_g, router_w_pad, router_b_pad, tm):
    m, d = x.shape
    row = lambda i: (i, 0)
    return pl.pallas_call(
        _merge_body,
        grid=(m // tm,),
        in_specs=[
            pl.BlockSpec((tm, d), row),
            pl.BlockSpec((tm, W_SGU), row),
            pl.BlockSpec((tm, W_ATTN), row),
            pl.BlockSpec((tm, d), lambda i: (i, 0)),
            pl.BlockSpec((tm, d), lambda i: (i, 1)),
            _const_spec((W_SGU, d)),
            _const_spec((W_ATTN, d)),
            _const_spec((d, d)),
            _const_spec((1, d)),
            _const_spec((d, 2 * LANES)),
            _const_spec((1, LANES)),
        ],
        out_specs=[pl.BlockSpec((tm, d), row), pl.BlockSpec((tm, d), row), pl.BlockSpec((tm, LANES), row)],
        out_shape=[
            jax.ShapeDtypeStruct((m, d), F32),
            jax.ShapeDtypeStruct((m, d), F32),
            jax.ShapeDtypeStruct((m, LANES), F32),
        ],
        compiler_params=_cparams(("parallel",), 56),
        name="merge",
    )(x, y_a, y_b, gates, gates, w_pa, w_pb, w_o, norm2_g.reshape(1, d), router_w_pad, router_b_pad)


def _row_copies(idx_ref, base, n, src_hbm, buf, slot, sem, start):
    for r in range(n):
        cp = pltpu.make_async_copy(src_hbm.at[pl.ds(idx_ref[base + r], 1)], buf.at[slot, pl.ds(r, 1)],
                                   sem.at[slot])
        if start:
            cp.start(priority=r % 2)
        else:
            cp.wait()


def _dispatch_body(valid_ref, first_ref, idx_ref, h_hbm, o_ref, buf, sem, *, sb, n_steps):
    i = pl.program_id(0)
    slot = _mod_pow2(i, 2)
    nxt = jnp.minimum(i + 1, n_steps - 1)

    @pl.when((i == 0) & (valid_ref[0] > 0))
    def _():
        _row_copies(idx_ref, first_ref[0], sb, h_hbm, buf, 0, sem, True)

    @pl.when((i + 1 < n_steps) & (valid_ref[nxt] > 0))
    def _():
        _row_copies(idx_ref, first_ref[nxt], sb, h_hbm, buf, 1 - slot, sem, True)

    @pl.when(valid_ref[i] > 0)
    def _():
        _row_copies(idx_ref, first_ref[i], sb, h_hbm, buf, slot, sem, False)
        o_ref[...] = buf[slot].astype(o_ref.dtype)

    @pl.when(valid_ref[i] == 0)
    def _():
        o_ref[...] = jnp.zeros(o_ref.shape, o_ref.dtype)


def moe_dispatch(h2, sub_valid, sub_first, sorted_tok, n_rows):
    sb = MOE_SB
    d = h2.shape[1]
    n_steps = n_rows // sb
    grid_spec = pltpu.PrefetchScalarGridSpec(
        num_scalar_prefetch=3,
        grid=(n_steps,),
        in_specs=[pl.BlockSpec(memory_space=pl.ANY)],
        out_specs=pl.BlockSpec((sb, d), lambda i, va, ci, idx: (i, 0)),
        scratch_shapes=[pltpu.VMEM((2, sb, d), F32), pltpu.SemaphoreType.DMA((2,))],
    )
    return pl.pallas_call(
        functools.partial(_dispatch_body, sb=sb, n_steps=n_steps),
        grid_spec=grid_spec,
        out_shape=jax.ShapeDtypeStruct((n_rows, d), BF16),
        compiler_params=_cparams(("arbitrary",), 32),
        name="moe_dispatch",
    )(sub_valid, sub_first, sorted_tok, h2)


def _combine_body(pos_ref, x1_ref, g_ref, fg_ref, y_hbm, o_ref, buf, sem, *, tt, n_steps):
    i = pl.program_id(0)
    n = tt * TOP_K
    slot = _mod_pow2(i, 2)

    @pl.when(i == 0)
    def _():
        _row_copies(pos_ref, 0, n, y_hbm, buf, 0, sem, True)

    @pl.when(i + 1 < n_steps)
    def _():
        _row_copies(pos_ref, (i + 1) * n, n, y_hbm, buf, 1 - slot, sem, True)

    _row_copies(pos_ref, i * n, n, y_hbm, buf, slot, sem, False)
    rows = buf[slot]
    g = g_ref[...]
    moe = g[:, 0:1] * rows[0:tt]
    for k in range(1, TOP_K):
        moe = moe + g[:, k:k + 1] * rows[k * tt:(k + 1) * tt]
    o_ref[...] = _rmsnorm_rows(x1_ref[...] + moe, fg_ref[...])


def moe_combine_final(x1, pos, gates, y_rows, final_g):
    m, d = x1.shape
    tt = min(COMBINE_TT, m)
    n_steps = m // tt
    pos_km = pos.reshape(n_steps, tt, TOP_K).transpose(0, 2, 1).reshape(-1)
    grid_spec = pltpu.PrefetchScalarGridSpec(
        num_scalar_prefetch=1,
        grid=(n_steps,),
        in_specs=[
            pl.BlockSpec((tt, d), lambda i, p: (i, 0)),
            pl.BlockSpec((tt, TOP_K), lambda i, p: (i, 0)),
            pl.BlockSpec((1, d), lambda i, p: (0, 0)),
            pl.BlockSpec(memory_space=pl.ANY),
        ],
        out_specs=pl.BlockSpec((tt, d), lambda i, p: (i, 0)),
        scratch_shapes=[pltpu.VMEM((2, tt * TOP_K, d), F32), pltpu.SemaphoreType.DMA((2,))],
    )
    return pl.pallas_call(
        functools.partial(_combine_body, tt=tt, n_steps=n_steps),
        grid_spec=grid_spec,
        out_shape=jax.ShapeDtypeStruct((m, d), F32),
        compiler_params=_cparams(("arbitrary",), 32),
        name="moe_combine_final",
    )(pos_km, x1, gates, final_g.reshape(1, d), y_rows)


def _moe_body(we_ref, wb_ref, wn_ref, x_ref, wg_ref, wl_ref, bg_ref, bl_ref, wd_ref, bd_ref, y_ref,
              wgl_bf, wd_bf, act_s, *, unit, tf, nf):
    w = pl.program_id(0)
    s = pl.program_id(1)
    n_u = wn_ref[w]
    rb = y_ref.shape[0]

    def rows_at(off, n):
        return pl.ds(pl.multiple_of(off, unit), n)

    def for_each_chunk(fn):
        n4 = lax.shift_right_logical(n_u, 2)

        def big(i, carry):
            fn(i * (4 * unit), 4 * unit)
            return carry

        lax.fori_loop(0, n4, big, 0)
        off2 = n4 * (4 * unit)

        @pl.when(jnp.bitwise_and(n_u, 2) != 0)
        def _():
            fn(off2, 2 * unit)

        @pl.when(jnp.bitwise_and(n_u, 1) != 0)
        def _():
            fn(off2 + jnp.bitwise_and(n_u, 2) * unit, unit)

    @pl.when((s < nf) & (n_u > 0))
    def _():
        wgl_bf[:, 0:tf] = wg_ref[...].astype(BF16)
        wgl_bf[:, tf:2 * tf] = wl_ref[...].astype(BF16)
        bgl = jnp.concatenate([bg_ref[...], bl_ref[...]], axis=1)

        def chunk(off, n):
            rows = rows_at(off, n)
            gl = jnp.dot(x_ref[rows, :], wgl_bf[...], preferred_element_type=F32) + bgl
            glu = jnp.minimum(gl[:, 0:tf], SWIGLU_LIMIT)
            lin = jnp.clip(gl[:, tf:2 * tf], -SWIGLU_LIMIT, SWIGLU_LIMIT)
            act = glu * jax.nn.sigmoid(SWIGLU_ALPHA * glu) * (lin + 1.0)
            act_s[s, rows, :] = act.astype(BF16)

        for_each_chunk(chunk)

    @pl.when(s >= nf)
    def _():
        @pl.when(n_u > 0)
        def _():
            wd_bf[...] = wd_ref[...].astype(BF16)

            def chunk(off, n):
                rows = rows_at(off, n)
                a = jnp.concatenate([act_s[k, rows, :] for k in range(nf)], axis=1)
                y_ref[rows, :] = jnp.dot(a, wd_bf[...], preferred_element_type=F32) + bd_ref[...]

            for_each_chunk(chunk)

        def clear(i, carry):
            y_ref[rows_at(i * unit, unit), :] = jnp.zeros((unit, y_ref.shape[1]), F32)
            return carry

        lax.fori_loop(n_u, rb // unit, clear, 0)


def moe_experts(x_rows, work_e, work_blk, work_nsub, w_gu, b_gu, w_down, b_down):
    n_rows, d = x_rows.shape
    rb, tf, tn = MOE_RB, MOE_TF, MOE_TN
    nw = n_rows // rb
    nf = D_FF // tf
    nd = d // tn

    def up(s):
        return jnp.minimum(s, nf - 1)

    def down(s):
        return jnp.maximum(s - nf, 0)

    grid_spec = pltpu.PrefetchScalarGridSpec(
        num_scalar_prefetch=3,
        grid=(nw, nf + nd),
        in_specs=[
            pl.BlockSpec((rb, d), lambda w, s, we, wb, wn: (wb[w], 0)),
            pl.BlockSpec((None, d, tf), lambda w, s, we, wb, wn: (we[w], 0, up(s))),
            pl.BlockSpec((None, d, tf), lambda w, s, we, wb, wn: (we[w], 0, nf + up(s))),
            pl.BlockSpec((None, 1, tf), lambda w, s, we, wb, wn: (we[w], 0, up(s))),
            pl.BlockSpec((None, 1, tf), lambda w, s, we, wb, wn: (we[w], 0, nf + up(s))),
            pl.BlockSpec((None, D_FF, tn), lambda w, s, we, wb, wn: (we[w], 0, down(s))),
            pl.BlockSpec((None, 1, tn), lambda w, s, we, wb, wn: (we[w], 0, down(s))),
        ],
        out_specs=pl.BlockSpec((rb, tn), lambda w, s, we, wb, wn: (w, down(s))),
        scratch_shapes=[
            pltpu.VMEM((d, 2 * tf), BF16),
            pltpu.VMEM((D_FF, tn), BF16),
            pltpu.VMEM((nf, rb, tf), BF16),
        ],
    )
    return pl.pallas_call(
        functools.partial(_moe_body, unit=MOE_UNIT, tf=tf, nf=nf),
        grid_spec=grid_spec,
        out_shape=jax.ShapeDtypeStruct((n_rows, d), F32),
        compiler_params=_cparams(("arbitrary", "arbitrary"), 58),
        name="moe_experts",
    )(work_e, work_blk, work_nsub, x_rows, w_gu, w_gu, b_gu.reshape(N_EXPERTS, 1, 2 * D_FF),
      b_gu.reshape(N_EXPERTS, 1, 2 * D_FF), w_down, b_down.reshape(N_EXPERTS, 1, d))


def _routing(logits):
    n_tok = logits.shape[0]
    n_assign = n_tok * TOP_K
    rb, sb, unit = MOE_RB, MOE_SB, MOE_UNIT
    spw = rb // sb
    nw = n_assign // rb + N_EXPERTS
    top_logits, top_idx = lax.top_k(logits, TOP_K)
    gates = jax.nn.softmax(top_logits, axis=-1)
    flat_e = top_idx.reshape(-1).astype(jnp.int32)
    onehot = (flat_e[:, None] == jnp.arange(N_EXPERTS, dtype=jnp.int32)[None, :]).astype(jnp.int32)
    csum = jnp.cumsum(onehot, axis=0)
    rank = jnp.take_along_axis(csum, flat_e[:, None], axis=1)[:, 0] - 1
    counts = csum[-1]
    nwe = (counts + rb - 1) // rb
    w_end = jnp.cumsum(nwe)
    w_start = w_end - nwe
    dest = w_start[flat_e] * rb + rank
    widx = jnp.arange(nw, dtype=jnp.int32)
    n_used = w_end[-1]
    valid = widx < n_used
    we = jnp.minimum(jnp.searchsorted(w_end, widx, side="right"), N_EXPERTS - 1).astype(jnp.int32)
    rows_valid = jnp.clip(counts[we] - (widx - w_start[we]) * rb, 0, rb)
    n_units = jnp.where(valid, (rows_valid + unit - 1) // unit, 0).astype(jnp.int32)
    last = jnp.maximum(n_used - 1, 0)
    work_e = jnp.where(valid, we, we[last]).astype(jnp.int32)
    work_blk = jnp.where(valid, widx, last).astype(jnp.int32)
    jblk = jnp.arange(spw, dtype=jnp.int32)[None, :]
    sub_valid = (jblk * (sb // unit) < n_units[:, None]).astype(jnp.int32).reshape(-1)
    a_start = jnp.cumsum(counts) - counts
    first_w = a_start[we] + (widx - w_start[we]) * rb
    sub_first = jnp.clip(first_w[:, None] + jblk * sb, 0, n_assign).astype(jnp.int32).reshape(-1)
    sorted_a = jnp.argsort(flat_e, stable=True).astype(jnp.int32)
    sorted_tok = jnp.concatenate([sorted_a // TOP_K, jnp.zeros((sb,), jnp.int32)])
    return gates, dest, work_e, work_blk, n_units, nw, sub_valid, sub_first, sorted_tok


def _layer_group(x2d, tm, w_in_bf, p, act_dtype):
    h = rmsnorm_to(x2d, p["norm1_g"], tm, BF16)
    (u,) = in_proj(h, w_in_bf, 0, 1, "gelu", tm, [act_dtype])
    (v,) = in_proj(h, w_in_bf, 1, 1, "gelu_ln", tm, [F32],
                   extra=(p["sgu_ln_g"].reshape(1, -1), p["sgu_ln_b"].reshape(1, -1)))
    (q,) = in_proj(h, w_in_bf, 2, 1, "scale", tm, [act_dtype])
    k32, kbf = in_proj(h, w_in_bf, 3, 1, "dual", tm, [F32, BF16])
    va32, vabf = in_proj(h, w_in_bf, 4, 1, "dual", tm, [F32, BF16])
    (gates,) = in_proj(h, w_in_bf, 5, 4, "sigmoid", tm, [BF16])
    return u, v, q, k32, kbf, va32, vabf, gates


def kernel(x_prompt, x_sample, cache_k, cache_v, page_table, norm1_g, w_in, sgu_ln_g, sgu_ln_b, sgu_w, sgu_b,
           lambda_q1, lambda_k1, lambda_q2, lambda_k2, subln_g, w_branch_a, w_branch_b, w_out, norm2_g,
           router_w, router_b, expert_w_gu, expert_b_gu, expert_w_down, expert_b_down, final_norm_g):
    depth = w_in.shape[0]
    assert depth == 1
    l = 0
    bp, sp, d = x_prompt.shape
    bs, ts, _ = x_sample.shape
    lam_init = 0.8 - 0.6 * math.exp(-0.3 * l)
    slopes2 = jnp.exp2(-(8.0 / N_HEADS) * jnp.arange(1, N_HEADS + 1, dtype=F32)) * LOG2E
    lam_vecs = jnp.stack([lambda_q1[l], lambda_k1[l], lambda_q2[l], lambda_k2[l]]).astype(F32)

    w_in_bf = w_in[l].astype(BF16)
    w_pa = w_branch_a[l].astype(BF16)
    w_pb = w_branch_b[l].astype(BF16)
    w_o = w_out[l].astype(BF16)
    rw32 = jnp.pad(router_w[l].astype(F32), ((0, 0), (0, LANES - N_EXPERTS)))
    rw_hi = rw32.astype(BF16)
    rw_pad = jnp.concatenate([rw_hi, (rw32 - rw_hi.astype(F32)).astype(BF16)], axis=1)
    rb_pad = jnp.pad(router_b[l].astype(F32), (0, LANES - N_EXPERTS)).reshape(1, LANES)
    p = {"norm1_g": norm1_g[l], "sgu_ln_g": sgu_ln_g[l], "sgu_ln_b": sgu_ln_b[l]}

    xp = x_prompt.reshape(bp * sp, d)
    u, v, q, k32p, kbf, va32p, vabf, gates = _layer_group(xp, 512, w_in_bf, p, BF16)
    y_a = sgu(u, v, sgu_w[l], sgu_b[l], CHUNK, 4, BF16)
    y_b = prompt_attention(q, kbf, vabf, lam_vecs, subln_g[l], slopes2, bp, sp, lam_init)
    x1p, h2p, lgp = merge(xp, y_a, y_b, gates, w_pa, w_pb, w_o, norm2_g[l], rw_pad, rb_pad, MERGE_TM)

    xs = x_sample.reshape(bs * ts, d)
    u, v_s, q, k32s, _, va32s, _, gates = _layer_group(xs, bs * ts, w_in_bf, p, F32)
    y_a = sgu(u, v_s, sgu_w[l], sgu_b[l], ts, bs, F32)
    k_sample = k32s.reshape(bs, ts, N_HEADS, 2 * DK)
    v_sample = va32s.reshape(bs, ts, N_HEADS, DV)
    y_b = decode_attention(q, k_sample, v_sample, cache_k, cache_v, l, page_table,
                           lam_vecs, subln_g[l], slopes2, lam_init)
    x1s, h2s, lgs = merge(xs, y_a, y_b, gates, w_pa, w_pb, w_o, norm2_g[l], rw_pad, rb_pad, bs * ts)

    h2 = jnp.concatenate([h2p, h2s], axis=0)
    logits = jnp.concatenate([lgp[:, :N_EXPERTS], lgs[:, :N_EXPERTS]], axis=0)
    gate_w, dest, work_e, work_blk, work_units, nw, sub_valid, sub_first, sorted_tok = _routing(logits)
    x_rows = moe_dispatch(h2, sub_valid, sub_first, sorted_tok, nw * MOE_RB)
    y_rows = moe_experts(x_rows, work_e, work_blk, work_units, expert_w_gu[l], expert_b_gu[l],
                         expert_w_down[l], expert_b_down[l])
    n_p = bp * sp
    pos = dest.reshape(-1, TOP_K)
    y_prompt = moe_combine_final(x1p, pos[:n_p], gate_w[:n_p], y_rows, final_norm_g).reshape(bp, sp, d)
    y_sample = moe_combine_final(x1s, pos[n_p:], gate_w[n_p:], y_rows, final_norm_g).reshape(bs, ts, d)

    k_prompt = k32p.reshape(1, bp, sp, N_HEADS, 2 * DK)
    v_prompt = va32p.reshape(1, bp, sp, N_HEADS, DV)
    state_sgu_v = v_s.reshape(1, bs, ts, W_SGU)
    return (y_prompt, y_sample, k_prompt, v_prompt, k_sample[None], v_sample[None], state_sgu_v)
```

```python
import functools
import math

import jax
import jax.numpy as jnp
from jax import lax
from jax.experimental import pallas as pl
from jax.experimental.pallas import tpu as pltpu

F32 = jnp.float32
BF16 = jnp.bfloat16

D_MODEL = 2048
N_HEADS = 8
DK = 64
DV = 2 * DK
W_ATTN = N_HEADS * DV
D_QK = 2 * N_HEADS * DK
W_SGU = D_MODEL // 2
N_SGU_GROUPS = 8
SGU_GROUP = W_SGU // N_SGU_GROUPS
CHUNK = 128
N_EXPERTS = 32
TOP_K = 4
D_FF = D_MODEL
SWIGLU_LIMIT = 7.0
SWIGLU_ALPHA = 1.702
EPS = 1e-6
PAGE_SIZE = 128
D_IN = 2 * W_SGU + 2 * D_QK + W_ATTN + 2 * D_MODEL
LOG2E = math.log2(math.e)

LANES = 128
SUBLANES = 8
MIB = 1024 * 1024

PROJ_TM = 1024
PROJ_TN = 1024
PROJ_RC = 256
ATT_TQ = 512
ATT_TK = 512
ATT_RB = 256
DEC_PP = 8
MERGE_TM = 256
MOE_RB = 1280
MOE_SB = 640
MOE_UNIT = 128
MOE_TF = 512
MOE_TN = 512
COMBINE_TT = 128
NEG_BIG = -1e30


def _cparams(semantics, vmem_mib):
    return pltpu.CompilerParams(dimension_semantics=semantics, vmem_limit_bytes=vmem_mib * MIB)


def _const_spec(shape):
    nd = len(shape)
    return pl.BlockSpec(shape, lambda *_: (0,) * nd, pipeline_mode=pl.Buffered(1))


def _div_pow2(x, n):
    assert n & (n - 1) == 0
    return lax.shift_right_logical(x, n.bit_length() - 1)


def _mod_pow2(x, n):
    assert n & (n - 1) == 0
    return jnp.bitwise_and(x, n - 1)


def _rmsnorm_rows(x, g):
    ms = jnp.mean(x * x, axis=-1, keepdims=True)
    return x * lax.rsqrt(ms + EPS) * g


def _rmsnorm_body(x_ref, g_ref, o_ref):
    o_ref[...] = _rmsnorm_rows(x_ref[...], g_ref[...]).astype(o_ref.dtype)


def rmsnorm_to(x2d, g, tm, dtype):
    m, d = x2d.shape
    return pl.pallas_call(
        _rmsnorm_body,
        grid=(m // tm,),
        in_specs=[pl.BlockSpec((tm, d), lambda i: (i, 0)), pl.BlockSpec((1, d), lambda i: (0, 0))],
        out_specs=pl.BlockSpec((tm, d), lambda i: (i, 0)),
        out_shape=jax.ShapeDtypeStruct((m, d), dtype),
        compiler_params=_cparams(("parallel",), 48),
        name="rmsnorm",
    )(x2d, g.reshape(1, d))


def _gelu(z):
    return 0.5 * z * (1.0 + lax.erf(z * (2.0 ** -0.5)))


def _proj_body(h_ref, w_ref, *refs, kind, rc):
    tm = h_ref.shape[0]

    def chunk(c, carry):
        r = pl.multiple_of(c * rc, rc)
        rows = pl.ds(r, rc)
        z = jnp.dot(h_ref[rows, :], w_ref[...], preferred_element_type=F32)
        if kind == "gelu":
            refs[0][rows, :] = _gelu(z).astype(refs[0].dtype)
        elif kind == "gelu_ln":
            g_ref, b_ref, o_ref = refs
            a = _gelu(z)
            mu = jnp.mean(a, axis=-1, keepdims=True)
            var = jnp.mean(jnp.square(a - mu), axis=-1, keepdims=True)
            y = (a - mu) * lax.rsqrt(var + EPS)
            o_ref[rows, :] = (y * g_ref[...] + b_ref[...]).astype(o_ref.dtype)
        elif kind == "scale":
            refs[0][rows, :] = (z * (DK ** -0.5 * LOG2E)).astype(refs[0].dtype)
        elif kind == "dual":
            refs[0][rows, :] = z
            refs[1][rows, :] = z.astype(refs[1].dtype)
        elif kind == "sigmoid":
            refs[0][rows, :] = jax.nn.sigmoid(z).astype(refs[0].dtype)
        return carry

    lax.fori_loop(0, tm // rc, chunk, 0)


def in_proj(h, w_bf, col_blk, n_blk, kind, tm, out_dtypes, extra=()):
    m, d = h.shape
    tn = PROJ_TN
    rc = min(PROJ_RC, tm)
    in_specs = [
        pl.BlockSpec((tm, d), lambda n, i: (i, 0)),
        pl.BlockSpec((d, tn), lambda n, i: (0, col_blk + n)),
    ] + [pl.BlockSpec((1, tn), lambda n, i: (0, 0)) for _ in extra]
    out_specs = [pl.BlockSpec((tm, tn), lambda n, i: (i, n)) for _ in out_dtypes]
    out_shape = [jax.ShapeDtypeStruct((m, n_blk * tn), dt) for dt in out_dtypes]
    res = pl.pallas_call(
        functools.partial(_proj_body, kind=kind, rc=rc),
        grid=(n_blk, m // tm),
        in_specs=in_specs,
        out_specs=out_specs,
        out_shape=out_shape,
        compiler_params=_cparams(("parallel", "parallel"), 48),
        name="in_proj_" + kind,
    )(h, w_bf, *extra)
    return res


def _sgu_body(u_ref, v_ref, w_ref, bt_ref, o_ref, *, rows, n_chunks):
    ii = lax.broadcasted_iota(jnp.int32, (CHUNK, CHUNK), 0)
    jj = lax.broadcasted_iota(jnp.int32, (CHUNK, CHUNK), 1)
    causal = jj <= ii
    for c in range(n_chunks):
        r0 = c * rows
        vb = v_ref[r0:r0 + rows, :]
        if rows < CHUNK:
            vb = jnp.concatenate([vb, jnp.zeros((CHUNK - rows, vb.shape[1]), vb.dtype)], axis=0)
        vb = vb.astype(BF16)
        for g in range(N_SGU_GROUPS):
            w = jnp.where(causal, w_ref[g], 0.0).astype(BF16)
            cols = slice(g * SGU_GROUP, (g + 1) * SGU_GROUP)
            f = jnp.dot(w, vb[:, cols], preferred_element_type=F32) + bt_ref[:, g:g + 1]
            u = u_ref[r0:r0 + rows, cols].astype(F32)
            o_ref[r0:r0 + rows, cols] = (u * f[:rows]).astype(o_ref.dtype)


def sgu(u, v, w_s, b_s, rows, n_chunks, out_dtype):
    m, w = u.shape
    tm = rows * n_chunks
    return pl.pallas_call(
        functools.partial(_sgu_body, rows=rows, n_chunks=n_chunks),
        grid=(m // tm,),
        in_specs=[
            pl.BlockSpec((tm, w), lambda i: (i, 0)),
            pl.BlockSpec((tm, w), lambda i: (i, 0)),
            pl.BlockSpec((N_SGU_GROUPS, CHUNK, CHUNK), lambda i: (0, 0, 0)),
            pl.BlockSpec((CHUNK, N_SGU_GROUPS), lambda i: (0, 0)),
        ],
        out_specs=pl.BlockSpec((tm, w), lambda i: (i, 0)),
        out_shape=jax.ShapeDtypeStruct((m, w), out_dtype),
        compiler_params=_cparams(("parallel",), 32),
        name="sgu",
    )(u, v, w_s, b_s.T)


def _lambda_value(lam_ref, lam_init):
    lv = lam_ref[...]
    d1 = jnp.sum(lv[0:1] * lv[1:2], axis=-1, keepdims=True)
    d2 = jnp.sum(lv[2:3] * lv[3:4], axis=-1, keepdims=True)
    return jnp.exp(d1) - jnp.exp(d2) + lam_init


def _head_out(o0, o1, lam, g, lam_init):
    o = o0 - lam * o1
    return _rmsnorm_rows(o, g) * (1.0 - lam_init)


def _flash_body(q_ref, k_ref, v_ref, lam_ref, g_ref, slope_ref, o_ref, vaug, m_s, acc, *, tq, tk, rb, lam_init):
    qi = pl.program_id(2)
    n_rb = tq // rb
    chains = [(c, r) for c in range(2) for r in range(n_rb)]

    @pl.when(qi == 0)
    def _():
        vaug[:, 0:DV] = v_ref[...]
        vaug[:, DV:2 * DV] = jnp.ones((vaug.shape[0], DV), BF16)

    q = q_ref[...]
    lane = lax.broadcasted_iota(jnp.int32, q.shape, 1)
    zero = jnp.zeros_like(q)
    qmap = [jnp.where(lane < DK, q, zero), jnp.where(lane >= DK, q, zero)]
    slope = slope_ref[...]
    col = lax.broadcasted_iota(jnp.int32, (1, tk), 1)
    q0 = qi * tq

    m_s[...] = jnp.full(m_s.shape, NEG_BIG, F32)
    acc[...] = jnp.zeros(acc.shape, F32)

    def step(j, masked):
        k0 = pl.multiple_of(j * tk, tk)
        kj = k_ref[pl.ds(k0, tk), :]
        vj = vaug[pl.ds(k0, tk), :]
        bias = slope * (col + (k0 - q0)).astype(F32)
        for ci, (c, r) in enumerate(chains):
            rows = slice(ci * rb, (ci + 1) * rb)
            qc = qmap[c][r * rb:(r + 1) * rb]
            nk = (r + 1) * rb if masked else tk
            s = lax.dot_general(qc, kj[0:nk], (((1,), (1,)), ((), ())), preferred_element_type=F32)
            s = s + bias[:, 0:nk]
            if masked:
                rr = lax.broadcasted_iota(jnp.int32, s.shape, 0) + (r * rb)
                cc = lax.broadcasted_iota(jnp.int32, s.shape, 1)
                s = jnp.where(cc <= rr, s, NEG_BIG)
            m_old = m_s[rows, :]
            m_new = jnp.maximum(m_old, jnp.max(s, axis=-1, keepdims=True))
            alpha = jnp.exp2(m_old - m_new)
            p = jnp.exp2(s - jnp.tile(m_new, (1, nk // LANES)))
            pv = jnp.dot(p.astype(BF16), vj[0:nk], preferred_element_type=F32)
            acc[rows, :] = jnp.tile(alpha, (1, 2 * DV // LANES)) * acc[rows, :] + pv
            m_s[rows, :] = m_new

    def full_step(j, carry):
        step(j, False)
        return carry

    lax.fori_loop(0, qi, full_step, 0)
    step(qi, True)

    lam = _lambda_value(lam_ref, lam_init)
    for r in range(n_rb):
        a0 = acc[r * rb:(r + 1) * rb, :]
        a1 = acc[(n_rb + r) * rb:(n_rb + r + 1) * rb, :]
        o0 = a0[:, 0:DV] / a0[:, DV:2 * DV]
        o1 = a1[:, 0:DV] / a1[:, DV:2 * DV]
        o_ref[r * rb:(r + 1) * rb, :] = _head_out(o0, o1, lam, g_ref[...], lam_init).astype(o_ref.dtype)


def prompt_attention(q, k, v, lam_vecs, subln_g, slopes2, batch, seq, lam_init):
    tq, tk, rb = ATT_TQ, ATT_TK, ATT_RB
    assert tq == tk and seq % tq == 0 and tq % rb == 0
    nq = seq // tq
    n_chain_rows = 2 * tq
    slope_b = jnp.broadcast_to(slopes2[:, None, None], (N_HEADS, 1, tk)).astype(F32)
    return pl.pallas_call(
        functools.partial(_flash_body, tq=tq, tk=tk, rb=rb, lam_init=lam_init),
        grid=(batch, N_HEADS, nq),
        in_specs=[
            pl.BlockSpec((tq, DV), lambda b, h, i: (b * nq + i, h)),
            pl.BlockSpec((seq, DV), lambda b, h, i: (b, h)),
            pl.BlockSpec((seq, DV), lambda b, h, i: (b, h)),
            pl.BlockSpec((4, DK), lambda b, h, i: (0, 0)),
            pl.BlockSpec((1, DV), lambda b, h, i: (0, 0)),
            pl.BlockSpec((None, 1, tk), lambda b, h, i: (h, 0, 0)),
        ],
        out_specs=pl.BlockSpec((tq, DV), lambda b, h, i: (b * nq + i, h)),
        out_shape=jax.ShapeDtypeStruct((batch * seq, W_ATTN), BF16),
        scratch_shapes=[
            pltpu.VMEM((seq, 2 * DV), BF16),
            pltpu.VMEM((n_chain_rows, LANES), F32),
            pltpu.VMEM((n_chain_rows, 2 * DV), F32),
        ],
        compiler_params=_cparams(("parallel", "parallel", "arbitrary"), 40),
        name="prompt_attention",
    )(q, k, v, lam_vecs, subln_g.reshape(1, DV), slope_b)


def _decode_body(pt_ref, q_ref, kn_ref, vn_ref, *rest, pp, n_steps, t_new, past, lam_init):
    kp = rest[:pp]
    vp = rest[pp:2 * pp]
    lam_ref, g_ref, slope_ref, o_ref, qx, bias0, m_s, l_s, acc = rest[2 * pp:]
    j = pl.program_id(1)
    nt = (((1,), (1,)), ((), ()))
    nrow = 2 * N_HEADS * t_new
    ncol = PAGE_SIZE * N_HEADS
    slope = slope_ref[...]
    rep = ncol // LANES

    @pl.when(j == 0)
    def _():
        q = q_ref[...]
        lane = lax.broadcasted_iota(jnp.int32, (t_new, 2 * DK), 1)
        parts = []
        for h in range(N_HEADS):
            qh = q[:, h * 2 * DK:(h + 1) * 2 * DK]
            parts.append(jnp.where(lane < DK, qh, 0.0))
            parts.append(jnp.where(lane >= DK, qh, 0.0))
        qx[...] = jnp.concatenate(parts, axis=0).astype(BF16)
        rr = lax.broadcasted_iota(jnp.int32, (nrow, ncol), 0)
        cc = lax.broadcasted_iota(jnp.int32, (nrow, ncol), 1)
        head_ok = _mod_pow2(cc, N_HEADS) == _div_pow2(rr, 2 * t_new)
        pos = _div_pow2(cc, N_HEADS).astype(F32)
        bias0[...] = jnp.where(head_ok, jnp.tile(slope, (1, rep)) * pos, NEG_BIG)
        n_new = t_new * N_HEADS
        pad = jnp.zeros((LANES - n_new, 2 * DK), F32)
        kn = jnp.concatenate([kn_ref[...].reshape(n_new, 2 * DK), pad], axis=0).astype(BF16)
        vn = jnp.concatenate([vn_ref[...].reshape(n_new, DV), pad], axis=0).astype(BF16)
        s = lax.dot_general(qx[...], kn, nt, preferred_element_type=F32)
        r1 = lax.broadcasted_iota(jnp.int32, s.shape, 0)
        c1 = lax.broadcasted_iota(jnp.int32, s.shape, 1)
        tk = _div_pow2(c1, N_HEADS)
        ok = (c1 < n_new) & (_mod_pow2(c1, N_HEADS) == _div_pow2(r1, 2 * t_new)) & (tk <= _mod_pow2(r1, t_new))
        s = jnp.where(ok, s + slope * tk.astype(F32), NEG_BIG)
        m = jnp.max(s, axis=-1, keepdims=True)
        p = jnp.exp2(s - m)
        m_s[...] = jnp.broadcast_to(m, m_s.shape)
        l_s[...] = jnp.broadcast_to(jnp.sum(p, axis=-1, keepdims=True), l_s.shape)
        acc[...] = jnp.dot(p.astype(BF16), vn, preferred_element_type=F32)

    q2 = qx[...]
    b0 = bias0[...]
    ts, offs = [], []
    m_old = m_s[...]
    m_new = m_old
    for i in range(pp):
        kpage = kp[i][...].reshape(ncol, 2 * DK).astype(BF16)
        t_i = lax.dot_general(q2, kpage, nt, preferred_element_type=F32) + b0
        off_i = slope * ((j * pp + i) * PAGE_SIZE - past).astype(F32)
        m_new = jnp.maximum(m_new, jnp.max(t_i, axis=-1, keepdims=True) + off_i)
        ts.append(t_i)
        offs.append(off_i)
    alpha = jnp.exp2(m_old - m_new)
    lsum = None
    pv = None
    for i in range(pp):
        p_i = jnp.exp2(ts[i] - jnp.tile(m_new - offs[i], (1, rep)))
        vpage = vp[i][...].reshape(ncol, DV).astype(BF16)
        d = jnp.dot(p_i.astype(BF16), vpage, preferred_element_type=F32)
        r = jnp.sum(p_i, axis=-1, keepdims=True)
        pv = d if pv is None else pv + d
        lsum = r if lsum is None else lsum + r
    l_s[...] = alpha * l_s[...] + lsum
    acc[...] = alpha * acc[...] + pv
    m_s[...] = m_new

    @pl.when(j == n_steps - 1)
    def _():
        lam = _lambda_value(lam_ref, lam_init)
        on = acc[...] / l_s[...]
        for h in range(N_HEADS):
            r0 = h * 2 * t_new
            o = _head_out(on[r0:r0 + t_new], on[r0 + t_new:r0 + 2 * t_new], lam, g_ref[...], lam_init)
            o_ref[:, h * DV:(h + 1) * DV] = o.astype(o_ref.dtype)


def decode_attention(q, k_new, v_new, cache_k, cache_v, layer, page_table, lam_vecs, subln_g, slopes2, lam_init):
    bd, n_pages = page_table.shape
    t_new = q.shape[0] // bd
    past = n_pages * PAGE_SIZE
    pp = DEC_PP
    n_steps = n_pages // pp
    nrow = 2 * N_HEADS * t_new
    ncol = PAGE_SIZE * N_HEADS
    assert nrow == LANES and t_new * N_HEADS <= LANES and n_pages % pp == 0
    width = N_HEADS * DV
    slope_rows = jnp.broadcast_to(jnp.repeat(slopes2, 2 * t_new)[:, None], (nrow, LANES)).astype(F32)

    def page_spec(i):
        return pl.BlockSpec((None, None, PAGE_SIZE, N_HEADS, DV),
                            lambda b, j, pt: (layer, pt[b * n_pages + j * pp + i], 0, 0, 0))

    new_spec = pl.BlockSpec((None, t_new, N_HEADS, DV), lambda b, j, pt: (b, 0, 0, 0))
    grid_spec = pltpu.PrefetchScalarGridSpec(
        num_scalar_prefetch=1,
        grid=(bd, n_steps),
        in_specs=[pl.BlockSpec((t_new, width), lambda b, j, pt: (b, 0)), new_spec, new_spec]
        + [page_spec(i) for i in range(pp)] + [page_spec(i) for i in range(pp)] + [
            pl.BlockSpec((4, DK), lambda b, j, pt: (0, 0)),
            pl.BlockSpec((1, DV), lambda b, j, pt: (0, 0)),
            pl.BlockSpec((nrow, LANES), lambda b, j, pt: (0, 0)),
        ],
        out_specs=pl.BlockSpec((t_new, width), lambda b, j, pt: (b, 0)),
        scratch_shapes=[
            pltpu.VMEM((nrow, 2 * DK), BF16),
            pltpu.VMEM((nrow, ncol), F32),
            pltpu.VMEM((nrow, LANES), F32),
            pltpu.VMEM((nrow, LANES), F32),
            pltpu.VMEM((nrow, DV), F32),
        ],
    )
    return pl.pallas_call(
        functools.partial(_decode_body, pp=pp, n_steps=n_steps, t_new=t_new, past=past, lam_init=lam_init),
        grid_spec=grid_spec,
        out_shape=jax.ShapeDtypeStruct((bd * t_new, width), F32),
        compiler_params=_cparams(("parallel", "arbitrary"), 40),
        name="decode_attention",
    )(page_table.reshape(-1), q, k_new, v_new, *([cache_k] * pp), *([cache_v] * pp),
      lam_vecs, subln_g.reshape(1, DV), slope_rows)


def _merge_body(x_ref, ya_ref, yb_ref, ga_ref, gb_ref, wpa_ref, wpb_ref, wo_ref, g2_ref, rw_ref, rb_ref,
                x1_ref, h2_ref, lg_ref):
    a = jnp.dot(ya_ref[...].astype(BF16), wpa_ref[...], preferred_element_type=F32)
    b = jnp.dot(yb_ref[...].astype(BF16), wpb_ref[...], preferred_element_type=F32)
    mixed = ga_ref[...].astype(F32) * a + gb_ref[...].astype(F32) * b
    x1 = x_ref[...] + jnp.dot(mixed.astype(BF16), wo_ref[...], preferred_element_type=F32)
    x1_ref[...] = x1
    h2 = _rmsnorm_rows(x1, g2_ref[...])
    h2_ref[...] = h2
    h_hi = h2.astype(BF16)
    h_lo = (h2 - h_hi.astype(F32)).astype(BF16)
    hw = jnp.dot(h_hi, rw_ref[...], preferred_element_type=F32)
    lw = jnp.dot(h_lo, rw_ref[:, 0:LANES], preferred_element_type=F32)
    lg_ref[...] = hw[:, 0:LANES] + (hw[:, LANES:2 * LANES] + lw) + rb_ref[...]


def merge(x, y_a, y_b, gates, w_pa, w_pb, w_o, norm2_g, router_w_pad, router_b_pad, tm):
    m, d = x.shape
    row = lambda i: (i, 0)
    return pl.pallas_call(
        _merge_body,
        grid=(m // tm,),
        in_specs=[
            pl.BlockSpec((tm, d), row),
            pl.BlockSpec((tm, W_SGU), row),
            pl.BlockSpec((tm, W_ATTN), row),
            pl.BlockSpec((tm, d), lambda i: (i, 0)),
            pl.BlockSpec((tm, d), lambda i: (i, 1)),
            _const_spec((W_SGU, d)),
            _const_spec((W_ATTN, d)),
            _const_spec((d, d)),
            _const_spec((1, d)),
            _const_spec((d, 2 * LANES)),
            _const_spec((1, LANES)),
        ],
        out_specs=[pl.BlockSpec((tm, d), row), pl.BlockSpec((tm, d), row), pl.BlockSpec((tm, LANES), row)],
        out_shape=[
            jax.ShapeDtypeStruct((m, d), F32),
            jax.ShapeDtypeStruct((m, d), F32),
            jax.ShapeDtypeStruct((m, LANES), F32),
        ],
        compiler_params=_cparams(("parallel",), 56),
        name="merge",
    )(x, y_a, y_b, gates, gates, w_pa, w_pb, w_o, norm2_g.reshape(1, d), router_w_pad, router_b_pad)


def _row_copies(idx_ref, base, n, src_hbm, buf, slot, sem, start):
    for r in range(n):
        cp = pltpu.make_async_copy(src_hbm.at[pl.ds(idx_ref[base + r], 1)], buf.at[slot, pl.ds(r, 1)],
                                   sem.at[slot])
        if start:
            cp.start(priority=r % 2)
        else:
            cp.wait()


def _dispatch_body(valid_ref, first_ref, idx_ref, h_hbm, o_ref, buf, sem, *, sb, n_steps):
    i = pl.program_id(0)
    slot = _mod_pow2(i, 2)
    nxt = jnp.minimum(i + 1, n_steps - 1)

    @pl.when((i == 0) & (valid_ref[0] > 0))
    def _():
        _row_copies(idx_ref, first_ref[0], sb, h_hbm, buf, 0, sem, True)

    @pl.when((i + 1 < n_steps) & (valid_ref[nxt] > 0))
    def _():
        _row_copies(idx_ref, first_ref[nxt], sb, h_hbm, buf, 1 - slot, sem, True)

    @pl.when(valid_ref[i] > 0)
    def _():
        _row_copies(idx_ref, first_ref[i], sb, h_hbm, buf, slot, sem, False)
        o_ref[...] = buf[slot].astype(o_ref.dtype)

    @pl.when(valid_ref[i] == 0)
    def _():
        o_ref[...] = jnp.zeros(o_ref.shape, o_ref.dtype)


def moe_dispatch(h2, sub_valid, sub_first, sorted_tok, n_rows):
    sb = MOE_SB
    d = h2.shape[1]
    n_steps = n_rows // sb
    grid_spec = pltpu.PrefetchScalarGridSpec(
        num_scalar_prefetch=3,
        grid=(n_steps,),
        in_specs=[pl.BlockSpec(memory_space=pl.ANY)],
        out_specs=pl.BlockSpec((sb, d), lambda i, va, ci, idx: (i, 0)),
        scratch_shapes=[pltpu.VMEM((2, sb, d), F32), pltpu.SemaphoreType.DMA((2,))],
    )
    return pl.pallas_call(
        functools.partial(_dispatch_body, sb=sb, n_steps=n_steps),
        grid_spec=grid_spec,
        out_shape=jax.ShapeDtypeStruct((n_rows, d), BF16),
        compiler_params=_cparams(("arbitrary",), 32),
        name="moe_dispatch",
    )(sub_valid, sub_first, sorted_tok, h2)


def _combine_body(pos_ref, x1_ref, g_ref, fg_ref, y_hbm, o_ref, buf, sem, *, tt, n_steps):
    i = pl.program_id(0)
    n = tt * TOP_K
    slot = _mod_pow2(i, 2)

    @pl.when(i == 0)
    def _():
        _row_copies(pos_ref, 0, n, y_hbm, buf, 0, sem, True)

    @pl.when(i + 1 < n_steps)
    def _():
        _row_copies(pos_ref, (i + 1) * n, n, y_hbm, buf, 1 - slot, sem, True)

    _row_copies(pos_ref, i * n, n, y_hbm, buf, slot, sem, False)
    rows = buf[slot]
    g = g_ref[...]
    moe = g[:, 0:1] * rows[0:tt]
    for k in range(1, TOP_K):
        moe = moe + g[:, k:k + 1] * rows[k * tt:(k + 1) * tt]
    o_ref[...] = _rmsnorm_rows(x1_ref[...] + moe, fg_ref[...])


def moe_combine_final(x1, pos, gates, y_rows, final_g):
    m, d = x1.shape
    tt = min(COMBINE_TT, m)
    n_steps = m // tt
    pos_km = pos.reshape(n_steps, tt, TOP_K).transpose(0, 2, 1).reshape(-1)
    grid_spec = pltpu.PrefetchScalarGridSpec(
        num_scalar_prefetch=1,
        grid=(n_steps,),
        in_specs=[
            pl.BlockSpec((tt, d), lambda i, p: (i, 0)),
            pl.BlockSpec((tt, TOP_K), lambda i, p: (i, 0)),
            pl.BlockSpec((1, d), lambda i, p: (0, 0)),
            pl.BlockSpec(memory_space=pl.ANY),
        ],
        out_specs=pl.BlockSpec((tt, d), lambda i, p: (i, 0)),
        scratch_shapes=[pltpu.VMEM((2, tt * TOP_K, d), F32), pltpu.SemaphoreType.DMA((2,))],
    )
    return pl.pallas_call(
        functools.partial(_combine_body, tt=tt, n_steps=n_steps),
        grid_spec=grid_spec,
        out_shape=jax.ShapeDtypeStruct((m, d), F32),
        compiler_params=_cparams(("arbitrary",), 32),
        name="moe_combine_final",
    )(pos_km, x1, gates, final_g.reshape(1, d), y_rows)


def _moe_body(we_ref, wb_ref, wn_ref, x_ref, wg_ref, wl_ref, bg_ref, bl_ref, wd_ref, bd_ref, y_ref,
              wgl_bf, wd_bf, act_s, *, unit, tf, nf):
    w = pl.program_id(0)
    s = pl.program_id(1)
    n_u = wn_ref[w]
    rb = y_ref.shape[0]

    def rows_at(off, n):
        return pl.ds(pl.multiple_of(off, unit), n)

    def for_each_chunk(fn):
        n4 = lax.shift_right_logical(n_u, 2)

        def big(i, carry):
            fn(i * (4 * unit), 4 * unit)
            return carry

        lax.fori_loop(0, n4, big, 0)
        off2 = n4 * (4 * unit)

        @pl.when(jnp.bitwise_and(n_u, 2) != 0)
        def _():
            fn(off2, 2 * unit)

        @pl.when(jnp.bitwise_and(n_u, 1) != 0)
        def _():
            fn(off2 + jnp.bitwise_and(n_u, 2) * unit, unit)

    @pl.when((s < nf) & (n_u > 0))
    def _():
        wgl_bf[:, 0:tf] = wg_ref[...].astype(BF16)
        wgl_bf[:, tf:2 * tf] = wl_ref[...].astype(BF16)
        bgl = jnp.concatenate([bg_ref[...], bl_ref[...]], axis=1)

        def chunk(off, n):
            rows = rows_at(off, n)
            gl = jnp.dot(x_ref[rows, :], wgl_bf[...], preferred_element_type=F32) + bgl
            glu = jnp.minimum(gl[:, 0:tf], SWIGLU_LIMIT)
            lin = jnp.clip(gl[:, tf:2 * tf], -SWIGLU_LIMIT, SWIGLU_LIMIT)
            act = glu * jax.nn.sigmoid(SWIGLU_ALPHA * glu) * (lin + 1.0)
            act_s[s, rows, :] = act.astype(BF16)

        for_each_chunk(chunk)

    @pl.when(s >= nf)
    def _():
        @pl.when(n_u > 0)
        def _():
            wd_bf[...] = wd_ref[...].astype(BF16)

            def chunk(off, n):
                rows = rows_at(off, n)
                a = jnp.concatenate([act_s[k, rows, :] for k in range(nf)], axis=1)
                y_ref[rows, :] = jnp.dot(a, wd_bf[...], preferred_element_type=F32) + bd_ref[...]

            for_each_chunk(chunk)

        def clear(i, carry):
            y_ref[rows_at(i * unit, unit), :] = jnp.zeros((unit, y_ref.shape[1]), F32)
            return carry

        lax.fori_loop(n_u, rb // unit, clear, 0)


def moe_experts(x_rows, work_e, work_blk, work_nsub, w_gu, b_gu, w_down, b_down):
    n_rows, d = x_rows.shape
    rb, tf, tn = MOE_RB, MOE_TF, MOE_TN
    nw = n_rows // rb
    nf = D_FF // tf
    nd = d // tn

    def down(s):
        return jnp.maximum(s - nf, 0)

    def up_w(w, s, wn):
        return jnp.where(wn[w] > 0, jnp.minimum(s, nf - 1), nf - 1)

    def down_w(w, s, wn):
        return jnp.where(wn[w] > 0, down(s), nd - 1)

    grid_spec = pltpu.PrefetchScalarGridSpec(
        num_scalar_prefetch=3,
        grid=(nw, nf + nd),
        in_specs=[
            pl.BlockSpec((rb, d), lambda w, s, we, wb, wn: (wb[w], 0), pipeline_mode=pl.Buffered(1)),
            pl.BlockSpec((None, d, tf), lambda w, s, we, wb, wn: (we[w], 0, up_w(w, s, wn))),
            pl.BlockSpec((None, d, tf), lambda w, s, we, wb, wn: (we[w], 0, nf + up_w(w, s, wn))),
            pl.BlockSpec((None, 1, tf), lambda w, s, we, wb, wn: (we[w], 0, up_w(w, s, wn))),
            pl.BlockSpec((None, 1, tf), lambda w, s, we, wb, wn: (we[w], 0, nf + up_w(w, s, wn))),
            pl.BlockSpec((None, D_FF, tn), lambda w, s, we, wb, wn: (we[w], 0, down_w(w, s, wn))),
            pl.BlockSpec((None, 1, tn), lambda w, s, we, wb, wn: (we[w], 0, down_w(w, s, wn))),
        ],
        out_specs=pl.BlockSpec((rb, tn), lambda w, s, we, wb, wn: (w, down(s))),
        scratch_shapes=[
            pltpu.VMEM((d, 2 * tf), BF16),
            pltpu.VMEM((D_FF, tn), BF16),
            pltpu.VMEM((nf, rb, tf), BF16),
        ],
    )
    return pl.pallas_call(
        functools.partial(_moe_body, unit=MOE_UNIT, tf=tf, nf=nf),
        grid_spec=grid_spec,
        out_shape=jax.ShapeDtypeStruct((n_rows, d), F32),
        compiler_params=_cparams(("arbitrary", "arbitrary"), 58),
        name="moe_experts",
    )(work_e, work_blk, work_nsub, x_rows, w_gu, w_gu, b_gu.reshape(N_EXPERTS, 1, 2 * D_FF),
      b_gu.reshape(N_EXPERTS, 1, 2 * D_FF), w_down, b_down.reshape(N_EXPERTS, 1, d))


def _routing(logits):
    n_tok = logits.shape[0]
    n_assign = n_tok * TOP_K
    rb, sb, unit = MOE_RB, MOE_SB, MOE_UNIT
    spw = rb // sb
    nw = n_assign // rb + N_EXPERTS
    top_logits, top_idx = lax.top_k(logits, TOP_K)
    gates = jax.nn.softmax(top_logits, axis=-1)
    flat_e = top_idx.reshape(-1).astype(jnp.int32)
    onehot = (flat_e[:, None] == jnp.arange(N_EXPERTS, dtype=jnp.int32)[None, :]).astype(jnp.int32)
    csum = jnp.cumsum(onehot, axis=0)
    rank = jnp.take_along_axis(csum, flat_e[:, None], axis=1)[:, 0] - 1
    counts = csum[-1]
    nwe = (counts + rb - 1) // rb
    w_end = jnp.cumsum(nwe)
    w_start = w_end - nwe
    dest = w_start[flat_e] * rb + rank
    widx = jnp.arange(nw, dtype=jnp.int32)
    n_used = w_end[-1]
    valid = widx < n_used
    we = jnp.minimum(jnp.searchsorted(w_end, widx, side="right"), N_EXPERTS - 1).astype(jnp.int32)
    rows_valid = jnp.clip(counts[we] - (widx - w_start[we]) * rb, 0, rb)
    n_units = jnp.where(valid, (rows_valid + unit - 1) // unit, 0).astype(jnp.int32)
    last = jnp.maximum(n_used - 1, 0)
    work_e = jnp.where(valid, we, we[last]).astype(jnp.int32)
    work_blk = jnp.where(valid, widx, last).astype(jnp.int32)
    jblk = jnp.arange(spw, dtype=jnp.int32)[None, :]
    sub_valid = (jblk * (sb // unit) < n_units[:, None]).astype(jnp.int32).reshape(-1)
    a_start = jnp.cumsum(counts) - counts
    first_w = a_start[we] + (widx - w_start[we]) * rb
    sub_first = jnp.clip(first_w[:, None] + jblk * sb, 0, n_assign).astype(jnp.int32).reshape(-1)
    sorted_a = jnp.argsort(flat_e, stable=True).astype(jnp.int32)
    sorted_tok = jnp.concatenate([sorted_a // TOP_K, jnp.zeros((sb,), jnp.int32)])
    return gates, dest, work_e, work_blk, n_units, nw, sub_valid, sub_first, sorted_tok


def _layer_group(x2d, tm, w_in_bf, p, act_dtype):
    h = rmsnorm_to(x2d, p["norm1_g"], tm, BF16)
    (u,) = in_proj(h, w_in_bf, 0, 1, "gelu", tm, [act_dtype])
    (v,) = in_proj(h, w_in_bf, 1, 1, "gelu_ln", tm, [F32],
                   extra=(p["sgu_ln_g"].reshape(1, -1), p["sgu_ln_b"].reshape(1, -1)))
    (q,) = in_proj(h, w_in_bf, 2, 1, "scale", tm, [act_dtype])
    k32, kbf = in_proj(h, w_in_bf, 3, 1, "dual", tm, [F32, BF16])
    va32, vabf = in_proj(h, w_in_bf, 4, 1, "dual", tm, [F32, BF16])
    (gates,) = in_proj(h, w_in_bf, 5, 4, "sigmoid", tm, [BF16])
    return u, v, q, k32, kbf, va32, vabf, gates


def kernel(x_prompt, x_sample, cache_k, cache_v, page_table, norm1_g, w_in, sgu_ln_g, sgu_ln_b, sgu_w, sgu_b,
           lambda_q1, lambda_k1, lambda_q2, lambda_k2, subln_g, w_branch_a, w_branch_b, w_out, norm2_g,
           router_w, router_b, expert_w_gu, expert_b_gu, expert_w_down, expert_b_down, final_norm_g):
    depth = w_in.shape[0]
    assert depth == 1
    l = 0
    bp, sp, d = x_prompt.shape
    bs, ts, _ = x_sample.shape
    lam_init = 0.8 - 0.6 * math.exp(-0.3 * l)
    slopes2 = jnp.exp2(-(8.0 / N_HEADS) * jnp.arange(1, N_HEADS + 1, dtype=F32)) * LOG2E
    lam_vecs = jnp.stack([lambda_q1[l], lambda_k1[l], lambda_q2[l], lambda_k2[l]]).astype(F32)

    w_in_bf = w_in[l].astype(BF16)
    w_pa = w_branch_a[l].astype(BF16)
    w_pb = w_branch_b[l].astype(BF16)
    w_o = w_out[l].astype(BF16)
    rw32 = jnp.pad(router_w[l].astype(F32), ((0, 0), (0, LANES - N_EXPERTS)))
    rw_hi = rw32.astype(BF16)
    rw_pad = jnp.concatenate([rw_hi, (rw32 - rw_hi.astype(F32)).astype(BF16)], axis=1)
    rb_pad = jnp.pad(router_b[l].astype(F32), (0, LANES - N_EXPERTS)).reshape(1, LANES)
    p = {"norm1_g": norm1_g[l], "sgu_ln_g": sgu_ln_g[l], "sgu_ln_b": sgu_ln_b[l]}

    xp = x_prompt.reshape(bp * sp, d)
    u, v, q, k32p, kbf, va32p, vabf, gates = _layer_group(xp, PROJ_TM, w_in_bf, p, BF16)
    y_a = sgu(u, v, sgu_w[l], sgu_b[l], CHUNK, 4, BF16)
    y_b = prompt_attention(q, kbf, vabf, lam_vecs, subln_g[l], slopes2, bp, sp, lam_init)
    x1p, h2p, lgp = merge(xp, y_a, y_b, gates, w_pa, w_pb, w_o, norm2_g[l], rw_pad, rb_pad, MERGE_TM)

    xs = x_sample.reshape(bs * ts, d)
    u, v_s, q, k32s, _, va32s, _, gates = _layer_group(xs, bs * ts, w_in_bf, p, F32)
    y_a = sgu(u, v_s, sgu_w[l], sgu_b[l], ts, bs, F32)
    k_sample = k32s.reshape(bs, ts, N_HEADS, 2 * DK)
    v_sample = va32s.reshape(bs, ts, N_HEADS, DV)
    y_b = decode_attention(q, k_sample, v_sample, cache_k, cache_v, l, page_table,
                           lam_vecs, subln_g[l], slopes2, lam_init)
    x1s, h2s, lgs = merge(xs, y_a, y_b, gates, w_pa, w_pb, w_o, norm2_g[l], rw_pad, rb_pad, bs * ts)

    h2 = jnp.concatenate([h2p, h2s], axis=0)
    logits = jnp.concatenate([lgp[:, :N_EXPERTS], lgs[:, :N_EXPERTS]], axis=0)
    gate_w, dest, work_e, work_blk, work_units, nw, sub_valid, sub_first, sorted_tok = _routing(logits)
    x_rows = moe_dispatch(h2, sub_valid, sub_first, sorted_tok, nw * MOE_RB)
    y_rows = moe_experts(x_rows, work_e, work_blk, work_units, expert_w_gu[l], expert_b_gu[l],
                         expert_w_down[l], expert_b_down[l])
    n_p = bp * sp
    pos = dest.reshape(-1, TOP_K)
    y_prompt = moe_combine_final(x1p, pos[:n_p], gate_w[:n_p], y_rows, final_norm_g).reshape(bp, sp, d)
    y_sample = moe_combine_final(x1s, pos[n_p:], gate_w[n_p:], y_rows, final_norm_g).reshape(bs, ts, d)

    k_prompt = k32p.reshape(1, bp, sp, N_HEADS, 2 * DK)
    v_prompt = va32p.reshape(1, bp, sp, N_HEADS, DV)
    state_sgu_v = v_s.reshape(1, bs, ts, W_SGU)
    return (y_prompt, y_sample, k_prompt, v_prompt, k_sample[None], v_sample[None], state_sgu_v)
```

```python
import functools
import math

import jax
import jax.numpy as jnp
from jax import lax
from jax.experimental import pallas as pl
from jax.experimental.pallas import tpu as pltpu

F32 = jnp.float32
BF16 = jnp.bfloat16

D_MODEL = 2048
N_HEADS = 8
DK = 64
DV = 2 * DK
W_ATTN = N_HEADS * DV
D_QK = 2 * N_HEADS * DK
W_SGU = D_MODEL // 2
N_SGU_GROUPS = 8
SGU_GROUP = W_SGU // N_SGU_GROUPS
CHUNK = 128
N_EXPERTS = 32
TOP_K = 4
D_FF = D_MODEL
SWIGLU_LIMIT = 7.0
SWIGLU_ALPHA = 1.702
EPS = 1e-6
PAGE_SIZE = 128
D_IN = 2 * W_SGU + 2 * D_QK + W_ATTN + 2 * D_MODEL
LOG2E = math.log2(math.e)

LANES = 128
SUBLANES = 8
MIB = 1024 * 1024

PROJ_TM = 1024
PROJ_TN = 1024
PROJ_RC = 512
ATT_TQ = 512
ATT_TK = 512
ATT_RB = 256
DEC_PP = 8
MERGE_TM = 256
MOE_RB = 1280
MOE_SB = 640
MOE_UNIT = 128
MOE_TF = 512
MOE_TN = 512
COMBINE_TT = 128
NEG_BIG = -1e30


def _cparams(semantics, vmem_mib):
    return pltpu.CompilerParams(dimension_semantics=semantics, vmem_limit_bytes=vmem_mib * MIB)


def _const_spec(shape):
    nd = len(shape)
    return pl.BlockSpec(shape, lambda *_: (0,) * nd, pipeline_mode=pl.Buffered(1))


def _div_pow2(x, n):
    assert n & (n - 1) == 0
    return lax.shift_right_logical(x, n.bit_length() - 1)


def _mod_pow2(x, n):
    assert n & (n - 1) == 0
    return jnp.bitwise_and(x, n - 1)


def _rmsnorm_rows(x, g):
    ms = jnp.mean(x * x, axis=-1, keepdims=True)
    return x * lax.rsqrt(ms + EPS) * g


def _rmsnorm_body(x_ref, g_ref, o_ref):
    o_ref[...] = _rmsnorm_rows(x_ref[...], g_ref[...]).astype(o_ref.dtype)


def rmsnorm_to(x2d, g, tm, dtype):
    m, d = x2d.shape
    return pl.pallas_call(
        _rmsnorm_body,
        grid=(m // tm,),
        in_specs=[pl.BlockSpec((tm, d), lambda i: (i, 0)), pl.BlockSpec((1, d), lambda i: (0, 0))],
        out_specs=pl.BlockSpec((tm, d), lambda i: (i, 0)),
        out_shape=jax.ShapeDtypeStruct((m, d), dtype),
        compiler_params=_cparams(("parallel",), 48),
        name="rmsnorm",
    )(x2d, g.reshape(1, d))


def _gelu(z):
    return 0.5 * z * (1.0 + lax.erf(z * (2.0 ** -0.5)))


def _proj_body(h_ref, w_ref, *refs, kind, rc):
    tm = h_ref.shape[0]
    *refs, w_bf = refs

    @pl.when(pl.program_id(1) == 0)
    def _():
        w_bf[...] = w_ref[...].astype(BF16)

    def chunk(c, carry):
        r = pl.multiple_of(c * rc, rc)
        rows = pl.ds(r, rc)
        z = jnp.dot(h_ref[rows, :], w_bf[...], preferred_element_type=F32)
        if kind == "gelu":
            refs[0][rows, :] = _gelu(z).astype(refs[0].dtype)
        elif kind == "gelu_ln":
            g_ref, b_ref, o_ref = refs
            a = _gelu(z)
            mu = jnp.mean(a, axis=-1, keepdims=True)
            var = jnp.mean(jnp.square(a - mu), axis=-1, keepdims=True)
            y = (a - mu) * lax.rsqrt(var + EPS)
            o_ref[rows, :] = (y * g_ref[...] + b_ref[...]).astype(o_ref.dtype)
        elif kind == "scale":
            refs[0][rows, :] = (z * (DK ** -0.5 * LOG2E)).astype(refs[0].dtype)
        elif kind == "dual":
            refs[0][rows, :] = z
            refs[1][rows, :] = z.astype(refs[1].dtype)
        elif kind == "sigmoid":
            refs[0][rows, :] = jax.nn.sigmoid(z).astype(refs[0].dtype)
        return carry

    lax.fori_loop(0, tm // rc, chunk, 0)


def in_proj(h, w, col_blk, n_blk, kind, tm, out_dtypes, extra=()):
    m, d = h.shape
    tn = PROJ_TN
    rc = min(PROJ_RC, tm)
    in_specs = [
        pl.BlockSpec((tm, d), lambda n, i: (i, 0)),
        pl.BlockSpec((d, tn), lambda n, i: (0, col_blk + n)),
    ] + [pl.BlockSpec((1, tn), lambda n, i: (0, 0)) for _ in extra]
    out_specs = [pl.BlockSpec((tm, tn), lambda n, i: (i, n)) for _ in out_dtypes]
    out_shape = [jax.ShapeDtypeStruct((m, n_blk * tn), dt) for dt in out_dtypes]
    res = pl.pallas_call(
        functools.partial(_proj_body, kind=kind, rc=rc),
        grid=(n_blk, m // tm),
        in_specs=in_specs,
        out_specs=out_specs,
        out_shape=out_shape,
        scratch_shapes=[pltpu.VMEM((d, tn), BF16)],
        compiler_params=_cparams(("parallel", "arbitrary"), 56),
        name="in_proj_" + kind,
    )(h, w, *extra)
    return res


def _sgu_body(u_ref, v_ref, w_ref, bt_ref, o_ref, *, rows, n_chunks):
    ii = lax.broadcasted_iota(jnp.int32, (CHUNK, CHUNK), 0)
    jj = lax.broadcasted_iota(jnp.int32, (CHUNK, CHUNK), 1)
    causal = jj <= ii
    for c in range(n_chunks):
        r0 = c * rows
        vb = v_ref[r0:r0 + rows, :]
        if rows < CHUNK:
            vb = jnp.concatenate([vb, jnp.zeros((CHUNK - rows, vb.shape[1]), vb.dtype)], axis=0)
        vb = vb.astype(BF16)
        for g in range(N_SGU_GROUPS):
            w = jnp.where(causal, w_ref[g], 0.0).astype(BF16)
            cols = slice(g * SGU_GROUP, (g + 1) * SGU_GROUP)
            f = jnp.dot(w, vb[:, cols], preferred_element_type=F32) + bt_ref[:, g:g + 1]
            u = u_ref[r0:r0 + rows, cols].astype(F32)
            o_ref[r0:r0 + rows, cols] = (u * f[:rows]).astype(o_ref.dtype)


def sgu(u, v, w_s, b_s, rows, n_chunks, out_dtype):
    m, w = u.shape
    tm = rows * n_chunks
    return pl.pallas_call(
        functools.partial(_sgu_body, rows=rows, n_chunks=n_chunks),
        grid=(m // tm,),
        in_specs=[
            pl.BlockSpec((tm, w), lambda i: (i, 0)),
            pl.BlockSpec((tm, w), lambda i: (i, 0)),
            pl.BlockSpec((N_SGU_GROUPS, CHUNK, CHUNK), lambda i: (0, 0, 0)),
            pl.BlockSpec((CHUNK, N_SGU_GROUPS), lambda i: (0, 0)),
        ],
        out_specs=pl.BlockSpec((tm, w), lambda i: (i, 0)),
        out_shape=jax.ShapeDtypeStruct((m, w), out_dtype),
        compiler_params=_cparams(("parallel",), 32),
        name="sgu",
    )(u, v, w_s, b_s.T)


def _lambda_value(lam_ref, lam_init):
    lv = lam_ref[...]
    d1 = jnp.sum(lv[0:1] * lv[1:2], axis=-1, keepdims=True)
    d2 = jnp.sum(lv[2:3] * lv[3:4], axis=-1, keepdims=True)
    return jnp.exp(d1) - jnp.exp(d2) + lam_init


def _head_out(o0, o1, lam, g, lam_init):
    o = o0 - lam * o1
    return _rmsnorm_rows(o, g) * (1.0 - lam_init)


def _flash_body(q_ref, k_ref, v_ref, lam_ref, g_ref, slope_ref, o_ref, vaug, m_s, acc, *, tq, tk, rb, lam_init):
    qi = pl.program_id(2)
    n_rb = tq // rb
    chains = [(c, r) for c in range(2) for r in range(n_rb)]

    @pl.when(qi == 0)
    def _():
        vaug[:, 0:DV] = v_ref[...]
        vaug[:, DV:2 * DV] = jnp.ones((vaug.shape[0], DV), BF16)

    q = q_ref[...]
    lane = lax.broadcasted_iota(jnp.int32, q.shape, 1)
    zero = jnp.zeros_like(q)
    qmap = [jnp.where(lane < DK, q, zero), jnp.where(lane >= DK, q, zero)]
    slope = slope_ref[...]
    col = lax.broadcasted_iota(jnp.int32, (1, tk), 1)
    q0 = qi * tq

    m_s[...] = jnp.full(m_s.shape, NEG_BIG, F32)
    acc[...] = jnp.zeros(acc.shape, F32)

    def step(j, masked):
        k0 = pl.multiple_of(j * tk, tk)
        kj = k_ref[pl.ds(k0, tk), :]
        vj = vaug[pl.ds(k0, tk), :]
        bias = slope * (col + (k0 - q0)).astype(F32)
        for ci, (c, r) in enumerate(chains):
            rows = slice(ci * rb, (ci + 1) * rb)
            qc = qmap[c][r * rb:(r + 1) * rb]
            nk = (r + 1) * rb if masked else tk
            s = lax.dot_general(qc, kj[0:nk], (((1,), (1,)), ((), ())), preferred_element_type=F32)
            s = s + bias[:, 0:nk]
            if masked:
                rr = lax.broadcasted_iota(jnp.int32, s.shape, 0) + (r * rb)
                cc = lax.broadcasted_iota(jnp.int32, s.shape, 1)
                s = jnp.where(cc <= rr, s, NEG_BIG)
            m_old = m_s[rows, :]
            m_new = jnp.maximum(m_old, jnp.max(s, axis=-1, keepdims=True))
            alpha = jnp.exp2(m_old - m_new)
            p = jnp.exp2(s - jnp.tile(m_new, (1, nk // LANES)))
            pv = jnp.dot(p.astype(BF16), vj[0:nk], preferred_element_type=F32)
            acc[rows, :] = jnp.tile(alpha, (1, 2 * DV // LANES)) * acc[rows, :] + pv
            m_s[rows, :] = m_new

    def two_full_steps(i, carry):
        step(2 * i, False)
        step(2 * i + 1, False)
        return carry

    lax.fori_loop(0, lax.shift_right_logical(qi, 1), two_full_steps, 0)
    odd = jnp.bitwise_and(qi, 1) == 1

    @pl.when(odd)
    def _():
        step(qi - 1, False)
        step(qi, True)

    @pl.when(jnp.logical_not(odd))
    def _():
        step(qi, True)

    lam = _lambda_value(lam_ref, lam_init)
    for r in range(n_rb):
        a0 = acc[r * rb:(r + 1) * rb, :]
        a1 = acc[(n_rb + r) * rb:(n_rb + r + 1) * rb, :]
        o0 = a0[:, 0:DV] / a0[:, DV:2 * DV]
        o1 = a1[:, 0:DV] / a1[:, DV:2 * DV]
        o_ref[r * rb:(r + 1) * rb, :] = _head_out(o0, o1, lam, g_ref[...], lam_init).astype(o_ref.dtype)


def prompt_attention(q, k, v, lam_vecs, subln_g, slopes2, batch, seq, lam_init):
    tq, tk, rb = ATT_TQ, ATT_TK, ATT_RB
    assert tq == tk and seq % tq == 0 and tq % rb == 0
    nq = seq // tq
    n_chain_rows = 2 * tq
    slope_b = jnp.broadcast_to(slopes2[:, None, None], (N_HEADS, 1, tk)).astype(F32)
    return pl.pallas_call(
        functools.partial(_flash_body, tq=tq, tk=tk, rb=rb, lam_init=lam_init),
        grid=(batch, N_HEADS, nq),
        in_specs=[
            pl.BlockSpec((tq, DV), lambda b, h, i: (b * nq + i, h)),
            pl.BlockSpec((seq, DV), lambda b, h, i: (b, h)),
            pl.BlockSpec((seq, DV), lambda b, h, i: (b, h)),
            pl.BlockSpec((4, DK), lambda b, h, i: (0, 0)),
            pl.BlockSpec((1, DV), lambda b, h, i: (0, 0)),
            pl.BlockSpec((None, 1, tk), lambda b, h, i: (h, 0, 0)),
        ],
        out_specs=pl.BlockSpec((tq, DV), lambda b, h, i: (b * nq + i, h)),
        out_shape=jax.ShapeDtypeStruct((batch * seq, W_ATTN), BF16),
        scratch_shapes=[
            pltpu.VMEM((seq, 2 * DV), BF16),
            pltpu.VMEM((n_chain_rows, LANES), F32),
            pltpu.VMEM((n_chain_rows, 2 * DV), F32),
        ],
        compiler_params=_cparams(("parallel", "parallel", "arbitrary"), 40),
        name="prompt_attention",
    )(q, k, v, lam_vecs, subln_g.reshape(1, DV), slope_b)


def _decode_body(pt_ref, q_ref, kn_ref, vn_ref, *rest, pp, n_steps, t_new, past, lam_init):
    kp = rest[:pp]
    vp = rest[pp:2 * pp]
    lam_ref, g_ref, slope_ref, o_ref, qx, bias0, m_s, l_s, acc = rest[2 * pp:]
    j = pl.program_id(1)
    nt = (((1,), (1,)), ((), ()))
    nrow = 2 * N_HEADS * t_new
    ncol = PAGE_SIZE * N_HEADS
    slope = slope_ref[...]
    rep = ncol // LANES

    @pl.when(j == 0)
    def _():
        q = q_ref[...]
        lane = lax.broadcasted_iota(jnp.int32, (t_new, 2 * DK), 1)
        parts = []
        for h in range(N_HEADS):
            qh = q[:, h * 2 * DK:(h + 1) * 2 * DK]
            parts.append(jnp.where(lane < DK, qh, 0.0))
            parts.append(jnp.where(lane >= DK, qh, 0.0))
        qx[...] = jnp.concatenate(parts, axis=0).astype(BF16)
        rr = lax.broadcasted_iota(jnp.int32, (nrow, ncol), 0)
        cc = lax.broadcasted_iota(jnp.int32, (nrow, ncol), 1)
        head_ok = _mod_pow2(cc, N_HEADS) == _div_pow2(rr, 2 * t_new)
        pos = _div_pow2(cc, N_HEADS).astype(F32)
        bias0[...] = jnp.where(head_ok, jnp.tile(slope, (1, rep)) * pos, NEG_BIG)
        n_new = t_new * N_HEADS
        pad = jnp.zeros((LANES - n_new, 2 * DK), F32)
        kn = jnp.concatenate([kn_ref[...].reshape(n_new, 2 * DK), pad], axis=0).astype(BF16)
        vn = jnp.concatenate([vn_ref[...].reshape(n_new, DV), pad], axis=0).astype(BF16)
        s = lax.dot_general(qx[...], kn, nt, preferred_element_type=F32)
        r1 = lax.broadcasted_iota(jnp.int32, s.shape, 0)
        c1 = lax.broadcasted_iota(jnp.int32, s.shape, 1)
        tk = _div_pow2(c1, N_HEADS)
        ok = (c1 < n_new) & (_mod_pow2(c1, N_HEADS) == _div_pow2(r1, 2 * t_new)) & (tk <= _mod_pow2(r1, t_new))
        s = jnp.where(ok, s + slope * tk.astype(F32), NEG_BIG)
        m = jnp.max(s, axis=-1, keepdims=True)
        p = jnp.exp2(s - m)
        m_s[...] = jnp.broadcast_to(m, m_s.shape)
        l_s[...] = jnp.broadcast_to(jnp.sum(p, axis=-1, keepdims=True), l_s.shape)
        acc[...] = jnp.dot(p.astype(BF16), vn, preferred_element_type=F32)

    q2 = qx[...]
    b0 = bias0[...]
    ts, offs = [], []
    m_old = m_s[...]
    m_new = m_old
    for i in range(pp):
        kpage = kp[i][...].reshape(ncol, 2 * DK).astype(BF16)
        t_i = lax.dot_general(q2, kpage, nt, preferred_element_type=F32) + b0
        off_i = slope * ((j * pp + i) * PAGE_SIZE - past).astype(F32)
        m_new = jnp.maximum(m_new, jnp.max(t_i, axis=-1, keepdims=True) + off_i)
        ts.append(t_i)
        offs.append(off_i)
    alpha = jnp.exp2(m_old - m_new)
    lsum = None
    pv = None
    for i in range(pp):
        p_i = jnp.exp2(ts[i] - jnp.tile(m_new - offs[i], (1, rep)))
        vpage = vp[i][...].reshape(ncol, DV).astype(BF16)
        d = jnp.dot(p_i.astype(BF16), vpage, preferred_element_type=F32)
        r = jnp.sum(p_i, axis=-1, keepdims=True)
        pv = d if pv is None else pv + d
        lsum = r if lsum is None else lsum + r
    l_s[...] = alpha * l_s[...] + lsum
    acc[...] = alpha * acc[...] + pv
    m_s[...] = m_new

    @pl.when(j == n_steps - 1)
    def _():
        lam = _lambda_value(lam_ref, lam_init)
        on = acc[...] / l_s[...]
        for h in range(N_HEADS):
            r0 = h * 2 * t_new
            o = _head_out(on[r0:r0 + t_new], on[r0 + t_new:r0 + 2 * t_new], lam, g_ref[...], lam_init)
            o_ref[:, h * DV:(h + 1) * DV] = o.astype(o_ref.dtype)


def decode_attention(q, k_new, v_new, cache_k, cache_v, layer, page_table, lam_vecs, subln_g, slopes2, lam_init):
    bd, n_pages = page_table.shape
    t_new = q.shape[0] // bd
    past = n_pages * PAGE_SIZE
    pp = DEC_PP
    n_steps = n_pages // pp
    nrow = 2 * N_HEADS * t_new
    ncol = PAGE_SIZE * N_HEADS
    assert nrow == LANES and t_new * N_HEADS <= LANES and n_pages % pp == 0
    width = N_HEADS * DV
    slope_rows = jnp.broadcast_to(jnp.repeat(slopes2, 2 * t_new)[:, None], (nrow, LANES)).astype(F32)

    def page_spec(i):
        return pl.BlockSpec((None, None, PAGE_SIZE, N_HEADS, DV),
                            lambda b, j, pt: (layer, pt[b * n_pages + j * pp + i], 0, 0, 0))

    new_spec = pl.BlockSpec((None, t_new, N_HEADS, DV), lambda b, j, pt: (b, 0, 0, 0))
    grid_spec = pltpu.PrefetchScalarGridSpec(
        num_scalar_prefetch=1,
        grid=(bd, n_steps),
        in_specs=[pl.BlockSpec((t_new, width), lambda b, j, pt: (b, 0)), new_spec, new_spec]
        + [page_spec(i) for i in range(pp)] + [page_spec(i) for i in range(pp)] + [
            pl.BlockSpec((4, DK), lambda b, j, pt: (0, 0)),
            pl.BlockSpec((1, DV), lambda b, j, pt: (0, 0)),
            pl.BlockSpec((nrow, LANES), lambda b, j, pt: (0, 0)),
        ],
        out_specs=pl.BlockSpec((t_new, width), lambda b, j, pt: (b, 0)),
        scratch_shapes=[
            pltpu.VMEM((nrow, 2 * DK), BF16),
            pltpu.VMEM((nrow, ncol), F32),
            pltpu.VMEM((nrow, LANES), F32),
            pltpu.VMEM((nrow, LANES), F32),
            pltpu.VMEM((nrow, DV), F32),
        ],
    )
    return pl.pallas_call(
        functools.partial(_decode_body, pp=pp, n_steps=n_steps, t_new=t_new, past=past, lam_init=lam_init),
        grid_spec=grid_spec,
        out_shape=jax.ShapeDtypeStruct((bd * t_new, width), F32),
        compiler_params=_cparams(("parallel", "arbitrary"), 40),
        name="decode_attention",
    )(page_table.reshape(-1), q, k_new, v_new, *([cache_k] * pp), *([cache_v] * pp),
      lam_vecs, subln_g.reshape(1, DV), slope_rows)


def _merge_body(x_ref, ya_ref, yb_ref, ga_ref, gb_ref, wpa_ref, wpb_ref, wo_ref, g2_ref, rw_ref, rb_ref,
                x1_ref, h2_ref, lg_ref):
    a = jnp.dot(ya_ref[...].astype(BF16), wpa_ref[...], preferred_element_type=F32)
    b = jnp.dot(yb_ref[...].astype(BF16), wpb_ref[...], preferred_element_type=F32)
    mixed = ga_ref[...].astype(F32) * a + gb_ref[...].astype(F32) * b
    x1 = x_ref[...] + jnp.dot(mixed.astype(BF16), wo_ref[...], preferred_element_type=F32)
    x1_ref[...] = x1
    h2 = _rmsnorm_rows(x1, g2_ref[...])
    h2_ref[...] = h2
    h_hi = h2.astype(BF16)
    h_lo = (h2 - h_hi.astype(F32)).astype(BF16)
    hw = jnp.dot(h_hi, rw_ref[...], preferred_element_type=F32)
    lw = jnp.dot(h_lo, rw_ref[:, 0:LANES], preferred_element_type=F32)
    lg_ref[...] = hw[:, 0:LANES] + (hw[:, LANES:2 * LANES] + lw) + rb_ref[...]


def merge(x, y_a, y_b, gates, w_pa, w_pb, w_o, norm2_g, router_w_pad, router_b_pad, tm):
    m, d = x.shape
    row = lambda i: (i, 0)
    return pl.pallas_call(
        _merge_body,
        grid=(m // tm,),
        in_specs=[
            pl.BlockSpec((tm, d), row),
            pl.BlockSpec((tm, W_SGU), row),
            pl.BlockSpec((tm, W_ATTN), row),
            pl.BlockSpec((tm, d), lambda i: (i, 0)),
            pl.BlockSpec((tm, d), lambda i: (i, 1)),
            _const_spec((W_SGU, d)),
            _const_spec((W_ATTN, d)),
            _const_spec((d, d)),
            _const_spec((1, d)),
            _const_spec((d, 2 * LANES)),
            _const_spec((1, LANES)),
        ],
        out_specs=[pl.BlockSpec((tm, d), row), pl.BlockSpec((tm, d), row), pl.BlockSpec((tm, LANES), row)],
        out_shape=[
            jax.ShapeDtypeStruct((m, d), F32),
            jax.ShapeDtypeStruct((m, d), F32),
            jax.ShapeDtypeStruct((m, LANES), F32),
        ],
        compiler_params=_cparams(("parallel",), 56),
        name="merge",
    )(x, y_a, y_b, gates, gates, w_pa, w_pb, w_o, norm2_g.reshape(1, d), router_w_pad, router_b_pad)


def _row_copies(idx_ref, base, n, src_hbm, buf, slot, sem, start):
    for r in range(n):
        cp = pltpu.make_async_copy(src_hbm.at[pl.ds(idx_ref[base + r], 1)], buf.at[slot, pl.ds(r, 1)],
                                   sem.at[slot])
        if start:
            cp.start(priority=r % 2)
        else:
            cp.wait()


def _dispatch_body(valid_ref, first_ref, idx_ref, h_hbm, o_ref, buf, sem, *, sb, n_steps):
    i = pl.program_id(0)
    slot = _mod_pow2(i, 2)
    nxt = jnp.minimum(i + 1, n_steps - 1)

    @pl.when((i == 0) & (valid_ref[0] > 0))
    def _():
        _row_copies(idx_ref, first_ref[0], sb, h_hbm, buf, 0, sem, True)

    @pl.when((i + 1 < n_steps) & (valid_ref[nxt] > 0))
    def _():
        _row_copies(idx_ref, first_ref[nxt], sb, h_hbm, buf, 1 - slot, sem, True)

    @pl.when(valid_ref[i] > 0)
    def _():
        _row_copies(idx_ref, first_ref[i], sb, h_hbm, buf, slot, sem, False)
        o_ref[...] = buf[slot].astype(o_ref.dtype)

    @pl.when(valid_ref[i] == 0)
    def _():
        o_ref[...] = jnp.zeros(o_ref.shape, o_ref.dtype)


def moe_dispatch(h2, sub_valid, sub_first, sorted_tok, n_rows):
    sb = MOE_SB
    d = h2.shape[1]
    n_steps = n_rows // sb
    grid_spec = pltpu.PrefetchScalarGridSpec(
        num_scalar_prefetch=3,
        grid=(n_steps,),
        in_specs=[pl.BlockSpec(memory_space=pl.ANY)],
        out_specs=pl.BlockSpec((sb, d), lambda i, va, ci, idx: (i, 0)),
        scratch_shapes=[pltpu.VMEM((2, sb, d), F32), pltpu.SemaphoreType.DMA((2,))],
    )
    return pl.pallas_call(
        functools.partial(_dispatch_body, sb=sb, n_steps=n_steps),
        grid_spec=grid_spec,
        out_shape=jax.ShapeDtypeStruct((n_rows, d), BF16),
        compiler_params=_cparams(("arbitrary",), 32),
        name="moe_dispatch",
    )(sub_valid, sub_first, sorted_tok, h2)


def _combine_body(pos_ref, x1_ref, g_ref, fg_ref, y_hbm, o_ref, buf, sem, *, tt, n_steps):
    i = pl.program_id(0)
    n = tt * TOP_K
    slot = _mod_pow2(i, 2)

    @pl.when(i == 0)
    def _():
        _row_copies(pos_ref, 0, n, y_hbm, buf, 0, sem, True)

    @pl.when(i + 1 < n_steps)
    def _():
        _row_copies(pos_ref, (i + 1) * n, n, y_hbm, buf, 1 - slot, sem, True)

    _row_copies(pos_ref, i * n, n, y_hbm, buf, slot, sem, False)
    rows = buf[slot]
    g = g_ref[...]
    moe = g[:, 0:1] * rows[0:tt]
    for k in range(1, TOP_K):
        moe = moe + g[:, k:k + 1] * rows[k * tt:(k + 1) * tt]
    o_ref[...] = _rmsnorm_rows(x1_ref[...] + moe, fg_ref[...])


def moe_combine_final(x1, pos, gates, y_rows, final_g):
    m, d = x1.shape
    tt = min(COMBINE_TT, m)
    n_steps = m // tt
    pos_km = pos.reshape(n_steps, tt, TOP_K).transpose(0, 2, 1).reshape(-1)
    grid_spec = pltpu.PrefetchScalarGridSpec(
        num_scalar_prefetch=1,
        grid=(n_steps,),
        in_specs=[
            pl.BlockSpec((tt, d), lambda i, p: (i, 0)),
            pl.BlockSpec((tt, TOP_K), lambda i, p: (i, 0)),
            pl.BlockSpec((1, d), lambda i, p: (0, 0)),
            pl.BlockSpec(memory_space=pl.ANY),
        ],
        out_specs=pl.BlockSpec((tt, d), lambda i, p: (i, 0)),
        scratch_shapes=[pltpu.VMEM((2, tt * TOP_K, d), F32), pltpu.SemaphoreType.DMA((2,))],
    )
    return pl.pallas_call(
        functools.partial(_combine_body, tt=tt, n_steps=n_steps),
        grid_spec=grid_spec,
        out_shape=jax.ShapeDtypeStruct((m, d), F32),
        compiler_params=_cparams(("arbitrary",), 32),
        name="moe_combine_final",
    )(pos_km, x1, gates, final_g.reshape(1, d), y_rows)


def _moe_body(we_ref, wb_ref, wn_ref, x_ref, wg_ref, wl_ref, bg_ref, bl_ref, wd_ref, bd_ref, y_ref,
              wgl_bf, wd_bf, act_s, *, unit, tf, nf):
    w = pl.program_id(0)
    s = pl.program_id(1)
    n_u = wn_ref[w]
    rb = y_ref.shape[0]

    def rows_at(off, n):
        return pl.ds(pl.multiple_of(off, unit), n)

    def for_each_chunk(fn):
        n4 = lax.shift_right_logical(n_u, 2)

        def big(i, carry):
            fn(i * (4 * unit), 4 * unit)
            return carry

        lax.fori_loop(0, n4, big, 0)
        off2 = n4 * (4 * unit)

        @pl.when(jnp.bitwise_and(n_u, 2) != 0)
        def _():
            fn(off2, 2 * unit)

        @pl.when(jnp.bitwise_and(n_u, 1) != 0)
        def _():
            fn(off2 + jnp.bitwise_and(n_u, 2) * unit, unit)

    @pl.when((s < nf) & (n_u > 0))
    def _():
        wgl_bf[:, 0:tf] = wg_ref[...].astype(BF16)
        wgl_bf[:, tf:2 * tf] = wl_ref[...].astype(BF16)
        bgl = jnp.concatenate([bg_ref[...], bl_ref[...]], axis=1)

        def chunk(off, n):
            rows = rows_at(off, n)
            gl = jnp.dot(x_ref[rows, :], wgl_bf[...], preferred_element_type=F32) + bgl
            glu = jnp.minimum(gl[:, 0:tf], SWIGLU_LIMIT)
            lin = jnp.clip(gl[:, tf:2 * tf], -SWIGLU_LIMIT, SWIGLU_LIMIT)
            act = glu * jax.nn.sigmoid(SWIGLU_ALPHA * glu) * (lin + 1.0)
            act_s[s, rows, :] = act.astype(BF16)

        for_each_chunk(chunk)

    @pl.when(s >= nf)
    def _():
        @pl.when(n_u > 0)
        def _():
            wd_bf[...] = wd_ref[...].astype(BF16)

            def chunk(off, n):
                rows = rows_at(off, n)
                a = jnp.concatenate([act_s[k, rows, :] for k in range(nf)], axis=1)
                y_ref[rows, :] = jnp.dot(a, wd_bf[...], preferred_element_type=F32) + bd_ref[...]

            for_each_chunk(chunk)

        def clear(i, carry):
            y_ref[rows_at(i * unit, unit), :] = jnp.zeros((unit, y_ref.shape[1]), F32)
            return carry

        lax.fori_loop(n_u, rb // unit, clear, 0)


def moe_experts(x_rows, work_e, work_blk, work_nsub, w_gu, b_gu, w_down, b_down):
    n_rows, d = x_rows.shape
    rb, tf, tn = MOE_RB, MOE_TF, MOE_TN
    nw = n_rows // rb
    nf = D_FF // tf
    nd = d // tn

    def down(s):
        return jnp.maximum(s - nf, 0)

    def up_w(w, s, wn):
        return jnp.where(wn[w] > 0, jnp.minimum(s, nf - 1), nf - 1)

    def down_w(w, s, wn):
        return jnp.where(wn[w] > 0, down(s), nd - 1)

    grid_spec = pltpu.PrefetchScalarGridSpec(
        num_scalar_prefetch=3,
        grid=(nw, nf + nd),
        in_specs=[
            pl.BlockSpec((rb, d), lambda w, s, we, wb, wn: (wb[w], 0), pipeline_mode=pl.Buffered(1)),
            pl.BlockSpec((None, d, tf), lambda w, s, we, wb, wn: (we[w], 0, up_w(w, s, wn))),
            pl.BlockSpec((None, d, tf), lambda w, s, we, wb, wn: (we[w], 0, nf + up_w(w, s, wn))),
            pl.BlockSpec((None, 1, tf), lambda w, s, we, wb, wn: (we[w], 0, up_w(w, s, wn))),
            pl.BlockSpec((None, 1, tf), lambda w, s, we, wb, wn: (we[w], 0, nf + up_w(w, s, wn))),
            pl.BlockSpec((None, D_FF, tn), lambda w, s, we, wb, wn: (we[w], 0, down_w(w, s, wn))),
            pl.BlockSpec((None, 1, tn), lambda w, s, we, wb, wn: (we[w], 0, down_w(w, s, wn))),
        ],
        out_specs=pl.BlockSpec((rb, tn), lambda w, s, we, wb, wn: (w, down(s))),
        scratch_shapes=[
            pltpu.VMEM((d, 2 * tf), BF16),
            pltpu.VMEM((D_FF, tn), BF16),
            pltpu.VMEM((nf, rb, tf), BF16),
        ],
    )
    return pl.pallas_call(
        functools.partial(_moe_body, unit=MOE_UNIT, tf=tf, nf=nf),
        grid_spec=grid_spec,
        out_shape=jax.ShapeDtypeStruct((n_rows, d), F32),
        compiler_params=_cparams(("arbitrary", "arbitrary"), 58),
        name="moe_experts",
    )(work_e, work_blk, work_nsub, x_rows, w_gu, w_gu, b_gu.reshape(N_EXPERTS, 1, 2 * D_FF),
      b_gu.reshape(N_EXPERTS, 1, 2 * D_FF), w_down, b_down.reshape(N_EXPERTS, 1, d))


def _routing(logits):
    n_tok = logits.shape[0]
    n_assign = n_tok * TOP_K
    rb, sb, unit = MOE_RB, MOE_SB, MOE_UNIT
    spw = rb // sb
    nw = n_assign // rb + N_EXPERTS
    top_logits, top_idx = lax.top_k(logits, TOP_K)
    gates = jax.nn.softmax(top_logits, axis=-1)
    flat_e = top_idx.reshape(-1).astype(jnp.int32)
    cb = 256
    assert n_assign % cb == 0
    onehot = (flat_e[:, None] == jnp.arange(N_EXPERTS, dtype=jnp.int32)[None, :]).astype(F32)
    tri = jnp.tril(jnp.ones((cb, cb), F32))
    within = jnp.einsum("ij,bjk->bik", tri, onehot.reshape(n_assign // cb, cb, N_EXPERTS))
    bsum = within[:, -1, :]
    csum = (within + (jnp.cumsum(bsum, axis=0) - bsum)[:, None, :]).reshape(n_assign, N_EXPERTS).astype(jnp.int32)
    rank = jnp.take_along_axis(csum, flat_e[:, None], axis=1)[:, 0] - 1
    counts = csum[-1]
    nwe = (counts + rb - 1) // rb
    w_end = jnp.cumsum(nwe)
    w_start = w_end - nwe
    dest = w_start[flat_e] * rb + rank
    widx = jnp.arange(nw, dtype=jnp.int32)
    n_used = w_end[-1]
    valid = widx < n_used
    we = jnp.minimum(jnp.searchsorted(w_end, widx, side="right"), N_EXPERTS - 1).astype(jnp.int32)
    rows_valid = jnp.clip(counts[we] - (widx - w_start[we]) * rb, 0, rb)
    n_units = jnp.where(valid, (rows_valid + unit - 1) // unit, 0).astype(jnp.int32)
    last = jnp.maximum(n_used - 1, 0)
    work_e = jnp.where(valid, we, we[last]).astype(jnp.int32)
    work_blk = jnp.where(valid, widx, last).astype(jnp.int32)
    jblk = jnp.arange(spw, dtype=jnp.int32)[None, :]
    sub_valid = (jblk * (sb // unit) < n_units[:, None]).astype(jnp.int32).reshape(-1)
    a_start = jnp.cumsum(counts) - counts
    first_w = a_start[we] + (widx - w_start[we]) * rb
    sub_first = jnp.clip(first_w[:, None] + jblk * sb, 0, n_assign).astype(jnp.int32).reshape(-1)
    sorted_a = jnp.argsort(flat_e, stable=True).astype(jnp.int32)
    sorted_tok = jnp.concatenate([sorted_a // TOP_K, jnp.zeros((sb,), jnp.int32)])
    return gates, dest, work_e, work_blk, n_units, nw, sub_valid, sub_first, sorted_tok


def _layer_group(x2d, tm, w_in_bf, p, act_dtype):
    h = rmsnorm_to(x2d, p["norm1_g"], tm, BF16)
    (u,) = in_proj(h, w_in_bf, 0, 1, "gelu", tm, [act_dtype])
    (v,) = in_proj(h, w_in_bf, 1, 1, "gelu_ln", tm, [F32],
                   extra=(p["sgu_ln_g"].reshape(1, -1), p["sgu_ln_b"].reshape(1, -1)))
    (q,) = in_proj(h, w_in_bf, 2, 1, "scale", tm, [act_dtype])
    k32, kbf = in_proj(h, w_in_bf, 3, 1, "dual", tm, [F32, BF16])
    va32, vabf = in_proj(h, w_in_bf, 4, 1, "dual", tm, [F32, BF16])
    (gates,) = in_proj(h, w_in_bf, 5, 4, "sigmoid", tm, [BF16])
    return u, v, q, k32, kbf, va32, vabf, gates


def kernel(x_prompt, x_sample, cache_k, cache_v, page_table, norm1_g, w_in, sgu_ln_g, sgu_ln_b, sgu_w, sgu_b,
           lambda_q1, lambda_k1, lambda_q2, lambda_k2, subln_g, w_branch_a, w_branch_b, w_out, norm2_g,
           router_w, router_b, expert_w_gu, expert_b_gu, expert_w_down, expert_b_down, final_norm_g):
    depth = w_in.shape[0]
    assert depth == 1
    l = 0
    bp, sp, d = x_prompt.shape
    bs, ts, _ = x_sample.shape
    lam_init = 0.8 - 0.6 * math.exp(-0.3 * l)
    slopes2 = jnp.exp2(-(8.0 / N_HEADS) * jnp.arange(1, N_HEADS + 1, dtype=F32)) * LOG2E
    lam_vecs = jnp.stack([lambda_q1[l], lambda_k1[l], lambda_q2[l], lambda_k2[l]]).astype(F32)

    w_in_bf = w_in[l]
    w_pa = w_branch_a[l].astype(BF16)
    w_pb = w_branch_b[l].astype(BF16)
    w_o = w_out[l].astype(BF16)
    rw32 = jnp.pad(router_w[l].astype(F32), ((0, 0), (0, LANES - N_EXPERTS)))
    rw_hi = rw32.astype(BF16)
    rw_pad = jnp.concatenate([rw_hi, (rw32 - rw_hi.astype(F32)).astype(BF16)], axis=1)
    rb_pad = jnp.pad(router_b[l].astype(F32), (0, LANES - N_EXPERTS)).reshape(1, LANES)
    p = {"norm1_g": norm1_g[l], "sgu_ln_g": sgu_ln_g[l], "sgu_ln_b": sgu_ln_b[l]}

    xp = x_prompt.reshape(bp * sp, d)
    u, v, q, k32p, kbf, va32p, vabf, gates = _layer_group(xp, PROJ_TM, w_in_bf, p, BF16)
    y_a = sgu(u, v, sgu_w[l], sgu_b[l], CHUNK, 4, BF16)
    y_b = prompt_attention(q, kbf, vabf, lam_vecs, subln_g[l], slopes2, bp, sp, lam_init)
    x1p, h2p, lgp = merge(xp, y_a, y_b, gates, w_pa, w_pb, w_o, norm2_g[l], rw_pad, rb_pad, MERGE_TM)

    xs = x_sample.reshape(bs * ts, d)
    u, v_s, q, k32s, _, va32s, _, gates = _layer_group(xs, bs * ts, w_in_bf, p, F32)
    y_a = sgu(u, v_s, sgu_w[l], sgu_b[l], ts, bs, F32)
    k_sample = k32s.reshape(bs, ts, N_HEADS, 2 * DK)
    v_sample = va32s.reshape(bs, ts, N_HEADS, DV)
    y_b = decode_attention(q, k_sample, v_sample, cache_k, cache_v, l, page_table,
                           lam_vecs, subln_g[l], slopes2, lam_init)
    x1s, h2s, lgs = merge(xs, y_a, y_b, gates, w_pa, w_pb, w_o, norm2_g[l], rw_pad, rb_pad, bs * ts)

    h2 = jnp.concatenate([h2p, h2s], axis=0)
    logits = jnp.concatenate([lgp[:, :N_EXPERTS], lgs[:, :N_EXPERTS]], axis=0)
    gate_w, dest, work_e, work_blk, work_units, nw, sub_valid, sub_first, sorted_tok = _routing(logits)
    x_rows = moe_dispatch(h2, sub_valid, sub_first, sorted_tok, nw * MOE_RB)
    y_rows = moe_experts(x_rows, work_e, work_blk, work_units, expert_w_gu[l], expert_b_gu[l],
                         expert_w_down[l], expert_b_down[l])
    n_p = bp * sp
    pos = dest.reshape(-1, TOP_K)
    y_prompt = moe_combine_final(x1p, pos[:n_p], gate_w[:n_p], y_rows, final_norm_g).reshape(bp, sp, d)
    y_sample = moe_combine_final(x1s, pos[n_p:], gate_w[n_p:], y_rows, final_norm_g).reshape(bs, ts, d)

    k_prompt = k32p.reshape(1, bp, sp, N_HEADS, 2 * DK)
    v_prompt = va32p.reshape(1, bp, sp, N_HEADS, DV)
    state_sgu_v = v_s.reshape(1, bs, ts, W_SGU)
    return (y_prompt, y_sample, k_prompt, v_prompt, k_sample[None], v_sample[None], state_sgu_v)
```

```python
import functools
import math

import jax
import jax.numpy as jnp
from jax import lax
from jax.experimental import pallas as pl
from jax.experimental.pallas import tpu as pltpu

F32 = jnp.float32
BF16 = jnp.bfloat16

D_MODEL = 2048
N_HEADS = 8
DK = 64
DV = 2 * DK
W_ATTN = N_HEADS * DV
D_QK = 2 * N_HEADS * DK
W_SGU = D_MODEL // 2
N_SGU_GROUPS = 8
SGU_GROUP = W_SGU // N_SGU_GROUPS
CHUNK = 128
N_EXPERTS = 32
TOP_K = 4
D_FF = D_MODEL
SWIGLU_LIMIT = 7.0
SWIGLU_ALPHA = 1.702
EPS = 1e-6
PAGE_SIZE = 128
D_IN = 2 * W_SGU + 2 * D_QK + W_ATTN + 2 * D_MODEL
LOG2E = math.log2(math.e)

LANES = 128
SUBLANES = 8
MIB = 1024 * 1024

PROJ_TM = 1024
PROJ_TN = 1024
PROJ_RC = 512
ATT_TQ = 512
ATT_TK = 512
ATT_RB = 256
DEC_PP = 8
MERGE_TM = 256
MOE_RB = 1280
MOE_SB = 640
MOE_UNIT = 128
MOE_TF = 512
MOE_TN = 512
COMBINE_TT = 128
NEG_BIG = -1e30


def _cparams(semantics, vmem_mib):
    return pltpu.CompilerParams(dimension_semantics=semantics, vmem_limit_bytes=vmem_mib * MIB)


def _const_spec(shape):
    nd = len(shape)
    return pl.BlockSpec(shape, lambda *_: (0,) * nd, pipeline_mode=pl.Buffered(1))


def _div_pow2(x, n):
    assert n & (n - 1) == 0
    return lax.shift_right_logical(x, n.bit_length() - 1)


def _mod_pow2(x, n):
    assert n & (n - 1) == 0
    return jnp.bitwise_and(x, n - 1)


def _rmsnorm_rows(x, g):
    ms = jnp.mean(x * x, axis=-1, keepdims=True)
    return x * lax.rsqrt(ms + EPS) * g


def _rmsnorm_body(x_ref, g_ref, o_ref):
    o_ref[...] = _rmsnorm_rows(x_ref[...], g_ref[...]).astype(o_ref.dtype)


def rmsnorm_to(x2d, g, tm, dtype):
    m, d = x2d.shape
    return pl.pallas_call(
        _rmsnorm_body,
        grid=(m // tm,),
        in_specs=[pl.BlockSpec((tm, d), lambda i: (i, 0)), pl.BlockSpec((1, d), lambda i: (0, 0))],
        out_specs=pl.BlockSpec((tm, d), lambda i: (i, 0)),
        out_shape=jax.ShapeDtypeStruct((m, d), dtype),
        compiler_params=_cparams(("parallel",), 48),
        name="rmsnorm",
    )(x2d, g.reshape(1, d))


def _gelu(z):
    return 0.5 * z * (1.0 + lax.erf(z * (2.0 ** -0.5)))


def _proj_body(h_ref, w_ref, *refs, kind, rc):
    tm = h_ref.shape[0]
    *refs, w_bf = refs

    @pl.when(pl.program_id(1) == 0)
    def _():
        w_bf[...] = w_ref[...].astype(BF16)

    def chunk(c, carry):
        r = pl.multiple_of(c * rc, rc)
        rows = pl.ds(r, rc)
        z = jnp.dot(h_ref[rows, :], w_bf[...], preferred_element_type=F32)
        if kind == "gelu":
            refs[0][rows, :] = _gelu(z).astype(refs[0].dtype)
        elif kind == "gelu_ln":
            g_ref, b_ref, o_ref = refs
            a = _gelu(z)
            mu = jnp.mean(a, axis=-1, keepdims=True)
            var = jnp.mean(jnp.square(a - mu), axis=-1, keepdims=True)
            y = (a - mu) * lax.rsqrt(var + EPS)
            o_ref[rows, :] = (y * g_ref[...] + b_ref[...]).astype(o_ref.dtype)
        elif kind == "scale":
            refs[0][rows, :] = (z * (DK ** -0.5 * LOG2E)).astype(refs[0].dtype)
        elif kind == "dual":
            refs[0][rows, :] = z
            refs[1][rows, :] = z.astype(refs[1].dtype)
        elif kind == "sigmoid":
            refs[0][rows, :] = jax.nn.sigmoid(z).astype(refs[0].dtype)
        return carry

    lax.fori_loop(0, tm // rc, chunk, 0)


def in_proj(h, w, col_blk, n_blk, kind, tm, out_dtypes, extra=()):
    m, d = h.shape
    tn = PROJ_TN
    rc = min(PROJ_RC, tm)
    in_specs = [
        pl.BlockSpec((tm, d), lambda n, i: (i, 0)),
        pl.BlockSpec((d, tn), lambda n, i: (0, col_blk + n)),
    ] + [pl.BlockSpec((1, tn), lambda n, i: (0, 0)) for _ in extra]
    out_specs = [pl.BlockSpec((tm, tn), lambda n, i: (i, n)) for _ in out_dtypes]
    out_shape = [jax.ShapeDtypeStruct((m, n_blk * tn), dt) for dt in out_dtypes]
    res = pl.pallas_call(
        functools.partial(_proj_body, kind=kind, rc=rc),
        grid=(n_blk, m // tm),
        in_specs=in_specs,
        out_specs=out_specs,
        out_shape=out_shape,
        scratch_shapes=[pltpu.VMEM((d, tn), BF16)],
        compiler_params=_cparams(("parallel", "arbitrary"), 56),
        name="in_proj_" + kind,
    )(h, w, *extra)
    return res


def _sgu_body(u_ref, v_ref, w_ref, bt_ref, o_ref, *, rows, n_chunks):
    ii = lax.broadcasted_iota(jnp.int32, (CHUNK, CHUNK), 0)
    jj = lax.broadcasted_iota(jnp.int32, (CHUNK, CHUNK), 1)
    causal = jj <= ii
    for c in range(n_chunks):
        r0 = c * rows
        vb = v_ref[r0:r0 + rows, :]
        if rows < CHUNK:
            vb = jnp.concatenate([vb, jnp.zeros((CHUNK - rows, vb.shape[1]), vb.dtype)], axis=0)
        vb = vb.astype(BF16)
        for g in range(N_SGU_GROUPS):
            w = jnp.where(causal, w_ref[g], 0.0).astype(BF16)
            cols = slice(g * SGU_GROUP, (g + 1) * SGU_GROUP)
            f = jnp.dot(w, vb[:, cols], preferred_element_type=F32) + bt_ref[:, g:g + 1]
            u = u_ref[r0:r0 + rows, cols].astype(F32)
            o_ref[r0:r0 + rows, cols] = (u * f[:rows]).astype(o_ref.dtype)


def sgu(u, v, w_s, b_s, rows, n_chunks, out_dtype):
    m, w = u.shape
    tm = rows * n_chunks
    return pl.pallas_call(
        functools.partial(_sgu_body, rows=rows, n_chunks=n_chunks),
        grid=(m // tm,),
        in_specs=[
            pl.BlockSpec((tm, w), lambda i: (i, 0)),
            pl.BlockSpec((tm, w), lambda i: (i, 0)),
            pl.BlockSpec((N_SGU_GROUPS, CHUNK, CHUNK), lambda i: (0, 0, 0)),
            pl.BlockSpec((CHUNK, N_SGU_GROUPS), lambda i: (0, 0)),
        ],
        out_specs=pl.BlockSpec((tm, w), lambda i: (i, 0)),
        out_shape=jax.ShapeDtypeStruct((m, w), out_dtype),
        compiler_params=_cparams(("parallel",), 32),
        name="sgu",
    )(u, v, w_s, b_s.T)


def _lambda_value(lam_ref, lam_init):
    lv = lam_ref[...]
    d1 = jnp.sum(lv[0:1] * lv[1:2], axis=-1, keepdims=True)
    d2 = jnp.sum(lv[2:3] * lv[3:4], axis=-1, keepdims=True)
    return jnp.exp(d1) - jnp.exp(d2) + lam_init


def _head_out(o0, o1, lam, g, lam_init):
    o = o0 - lam * o1
    return _rmsnorm_rows(o, g) * (1.0 - lam_init)


def _flash_body(q_ref, k_ref, v_ref, lam_ref, g_ref, slope_ref, o_ref, vaug, m_s, acc, *, tq, tk, rb, lam_init):
    qi = pl.program_id(2)
    n_rb = tq // rb
    chains = [(c, r) for c in range(2) for r in range(n_rb)]

    @pl.when(qi == 0)
    def _():
        vaug[:, 0:DV] = v_ref[...]
        vaug[:, DV:2 * DV] = jnp.ones((vaug.shape[0], DV), BF16)

    q = q_ref[...]
    lane = lax.broadcasted_iota(jnp.int32, q.shape, 1)
    zero = jnp.zeros_like(q)
    qmap = [jnp.where(lane < DK, q, zero), jnp.where(lane >= DK, q, zero)]
    slope = slope_ref[...]
    col = lax.broadcasted_iota(jnp.int32, (1, tk), 1)
    q0 = qi * tq

    m_s[...] = jnp.full(m_s.shape, NEG_BIG, F32)
    acc[...] = jnp.zeros(acc.shape, F32)

    def step(j, masked):
        k0 = pl.multiple_of(j * tk, tk)
        kj = k_ref[pl.ds(k0, tk), :]
        vj = vaug[pl.ds(k0, tk), :]
        bias = slope * (col + (k0 - q0)).astype(F32)
        for ci, (c, r) in enumerate(chains):
            rows = slice(ci * rb, (ci + 1) * rb)
            qc = qmap[c][r * rb:(r + 1) * rb]
            nk = (r + 1) * rb if masked else tk
            s = lax.dot_general(qc, kj[0:nk], (((1,), (1,)), ((), ())), preferred_element_type=F32)
            s = s + bias[:, 0:nk]
            if masked:
                rr = lax.broadcasted_iota(jnp.int32, s.shape, 0) + (r * rb)
                cc = lax.broadcasted_iota(jnp.int32, s.shape, 1)
                s = jnp.where(cc <= rr, s, NEG_BIG)
            m_old = m_s[rows, :]
            m_new = jnp.maximum(m_old, jnp.max(s, axis=-1, keepdims=True))
            alpha = jnp.exp2(m_old - m_new)
            p = jnp.exp2(s - jnp.tile(m_new, (1, nk // LANES)))
            pv = jnp.dot(p.astype(BF16), vj[0:nk], preferred_element_type=F32)
            acc[rows, :] = jnp.tile(alpha, (1, 2 * DV // LANES)) * acc[rows, :] + pv
            m_s[rows, :] = m_new

    def four_full_steps(i, carry):
        for u in range(4):
            step(4 * i + u, False)
        return carry

    n4 = lax.shift_right_logical(qi, 2)
    lax.fori_loop(0, n4, four_full_steps, 0)
    j0 = n4 * 4
    has2 = jnp.bitwise_and(qi, 2) != 0
    has1 = jnp.bitwise_and(qi, 1) != 0

    @pl.when(has2)
    def _():
        step(j0, False)
        step(j0 + 1, False)

    @pl.when(has1)
    def _():
        step(qi - 1, False)
        step(qi, True)

    @pl.when(jnp.logical_not(has1))
    def _():
        step(qi, True)

    lam = _lambda_value(lam_ref, lam_init)
    for r in range(n_rb):
        a0 = acc[r * rb:(r + 1) * rb, :]
        a1 = acc[(n_rb + r) * rb:(n_rb + r + 1) * rb, :]
        o0 = a0[:, 0:DV] / a0[:, DV:2 * DV]
        o1 = a1[:, 0:DV] / a1[:, DV:2 * DV]
        o_ref[r * rb:(r + 1) * rb, :] = _head_out(o0, o1, lam, g_ref[...], lam_init).astype(o_ref.dtype)


def prompt_attention(q, k, v, lam_vecs, subln_g, slopes2, batch, seq, lam_init):
    tq, tk, rb = ATT_TQ, ATT_TK, ATT_RB
    assert tq == tk and seq % tq == 0 and tq % rb == 0
    nq = seq // tq
    n_chain_rows = 2 * tq
    slope_b = jnp.broadcast_to(slopes2[:, None, None], (N_HEADS, 1, tk)).astype(F32)
    return pl.pallas_call(
        functools.partial(_flash_body, tq=tq, tk=tk, rb=rb, lam_init=lam_init),
        grid=(batch, N_HEADS, nq),
        in_specs=[
            pl.BlockSpec((tq, DV), lambda b, h, i: (b * nq + i, h)),
            pl.BlockSpec((seq, DV), lambda b, h, i: (b, h)),
            pl.BlockSpec((seq, DV), lambda b, h, i: (b, h)),
            pl.BlockSpec((4, DK), lambda b, h, i: (0, 0)),
            pl.BlockSpec((1, DV), lambda b, h, i: (0, 0)),
            pl.BlockSpec((None, 1, tk), lambda b, h, i: (h, 0, 0)),
        ],
        out_specs=pl.BlockSpec((tq, DV), lambda b, h, i: (b * nq + i, h)),
        out_shape=jax.ShapeDtypeStruct((batch * seq, W_ATTN), BF16),
        scratch_shapes=[
            pltpu.VMEM((seq, 2 * DV), BF16),
            pltpu.VMEM((n_chain_rows, LANES), F32),
            pltpu.VMEM((n_chain_rows, 2 * DV), F32),
        ],
        compiler_params=_cparams(("parallel", "parallel", "arbitrary"), 40),
        name="prompt_attention",
    )(q, k, v, lam_vecs, subln_g.reshape(1, DV), slope_b)


def _decode_body(pt_ref, q_ref, kn_ref, vn_ref, *rest, pp, n_steps, t_new, past, lam_init):
    kp = rest[:pp]
    vp = rest[pp:2 * pp]
    lam_ref, g_ref, slope_ref, o_ref, qx, bias0, m_s, l_s, acc = rest[2 * pp:]
    j = pl.program_id(1)
    nt = (((1,), (1,)), ((), ()))
    nrow = 2 * N_HEADS * t_new
    ncol = PAGE_SIZE * N_HEADS
    slope = slope_ref[...]
    rep = ncol // LANES

    @pl.when(j == 0)
    def _():
        q = q_ref[...]
        lane = lax.broadcasted_iota(jnp.int32, (t_new, 2 * DK), 1)
        parts = []
        for h in range(N_HEADS):
            qh = q[:, h * 2 * DK:(h + 1) * 2 * DK]
            parts.append(jnp.where(lane < DK, qh, 0.0))
            parts.append(jnp.where(lane >= DK, qh, 0.0))
        qx[...] = jnp.concatenate(parts, axis=0).astype(BF16)
        rr = lax.broadcasted_iota(jnp.int32, (nrow, ncol), 0)
        cc = lax.broadcasted_iota(jnp.int32, (nrow, ncol), 1)
        head_ok = _mod_pow2(cc, N_HEADS) == _div_pow2(rr, 2 * t_new)
        pos = _div_pow2(cc, N_HEADS).astype(F32)
        bias0[...] = jnp.where(head_ok, jnp.tile(slope, (1, rep)) * pos, NEG_BIG)
        n_new = t_new * N_HEADS
        pad = jnp.zeros((LANES - n_new, 2 * DK), F32)
        kn = jnp.concatenate([kn_ref[...].reshape(n_new, 2 * DK), pad], axis=0).astype(BF16)
        vn = jnp.concatenate([vn_ref[...].reshape(n_new, DV), pad], axis=0).astype(BF16)
        s = lax.dot_general(qx[...], kn, nt, preferred_element_type=F32)
        r1 = lax.broadcasted_iota(jnp.int32, s.shape, 0)
        c1 = lax.broadcasted_iota(jnp.int32, s.shape, 1)
        tk = _div_pow2(c1, N_HEADS)
        ok = (c1 < n_new) & (_mod_pow2(c1, N_HEADS) == _div_pow2(r1, 2 * t_new)) & (tk <= _mod_pow2(r1, t_new))
        s = jnp.where(ok, s + slope * tk.astype(F32), NEG_BIG)
        m = jnp.max(s, axis=-1, keepdims=True)
        p = jnp.exp2(s - m)
        m_s[...] = jnp.broadcast_to(m, m_s.shape)
        l_s[...] = jnp.broadcast_to(jnp.sum(p, axis=-1, keepdims=True), l_s.shape)
        acc[...] = jnp.dot(p.astype(BF16), vn, preferred_element_type=F32)

    q2 = qx[...]
    b0 = bias0[...]
    ts, offs = [], []
    m_old = m_s[...]
    m_new = m_old
    for i in range(pp):
        kpage = kp[i][...].reshape(ncol, 2 * DK).astype(BF16)
        t_i = lax.dot_general(q2, kpage, nt, preferred_element_type=F32) + b0
        off_i = slope * ((j * pp + i) * PAGE_SIZE - past).astype(F32)
        m_new = jnp.maximum(m_new, jnp.max(t_i, axis=-1, keepdims=True) + off_i)
        ts.append(t_i)
        offs.append(off_i)
    alpha = jnp.exp2(m_old - m_new)
    lsum = None
    pv = None
    for i in range(pp):
        p_i = jnp.exp2(ts[i] - jnp.tile(m_new - offs[i], (1, rep)))
        vpage = vp[i][...].reshape(ncol, DV).astype(BF16)
        d = jnp.dot(p_i.astype(BF16), vpage, preferred_element_type=F32)
        r = jnp.sum(p_i, axis=-1, keepdims=True)
        pv = d if pv is None else pv + d
        lsum = r if lsum is None else lsum + r
    l_s[...] = alpha * l_s[...] + lsum
    acc[...] = alpha * acc[...] + pv
    m_s[...] = m_new

    @pl.when(j == n_steps - 1)
    def _():
        lam = _lambda_value(lam_ref, lam_init)
        on = acc[...] / l_s[...]
        for h in range(N_HEADS):
            r0 = h * 2 * t_new
            o = _head_out(on[r0:r0 + t_new], on[r0 + t_new:r0 + 2 * t_new], lam, g_ref[...], lam_init)
            o_ref[:, h * DV:(h + 1) * DV] = o.astype(o_ref.dtype)


def decode_attention(q, k_new, v_new, cache_k, cache_v, layer, page_table, lam_vecs, subln_g, slopes2, lam_init):
    bd, n_pages = page_table.shape
    t_new = q.shape[0] // bd
    past = n_pages * PAGE_SIZE
    pp = DEC_PP
    n_steps = n_pages // pp
    nrow = 2 * N_HEADS * t_new
    ncol = PAGE_SIZE * N_HEADS
    assert nrow == LANES and t_new * N_HEADS <= LANES and n_pages % pp == 0
    width = N_HEADS * DV
    slope_rows = jnp.broadcast_to(jnp.repeat(slopes2, 2 * t_new)[:, None], (nrow, LANES)).astype(F32)

    def page_spec(i):
        return pl.BlockSpec((None, None, PAGE_SIZE, N_HEADS, DV),
                            lambda b, j, pt: (layer, pt[b * n_pages + j * pp + i], 0, 0, 0))

    new_spec = pl.BlockSpec((None, t_new, N_HEADS, DV), lambda b, j, pt: (b, 0, 0, 0))
    grid_spec = pltpu.PrefetchScalarGridSpec(
        num_scalar_prefetch=1,
        grid=(bd, n_steps),
        in_specs=[pl.BlockSpec((t_new, width), lambda b, j, pt: (b, 0)), new_spec, new_spec]
        + [page_spec(i) for i in range(pp)] + [page_spec(i) for i in range(pp)] + [
            pl.BlockSpec((4, DK), lambda b, j, pt: (0, 0)),
            pl.BlockSpec((1, DV), lambda b, j, pt: (0, 0)),
            pl.BlockSpec((nrow, LANES), lambda b, j, pt: (0, 0)),
        ],
        out_specs=pl.BlockSpec((t_new, width), lambda b, j, pt: (b, 0)),
        scratch_shapes=[
            pltpu.VMEM((nrow, 2 * DK), BF16),
            pltpu.VMEM((nrow, ncol), F32),
            pltpu.VMEM((nrow, LANES), F32),
            pltpu.VMEM((nrow, LANES), F32),
            pltpu.VMEM((nrow, DV), F32),
        ],
    )
    return pl.pallas_call(
        functools.partial(_decode_body, pp=pp, n_steps=n_steps, t_new=t_new, past=past, lam_init=lam_init),
        grid_spec=grid_spec,
        out_shape=jax.ShapeDtypeStruct((bd * t_new, width), F32),
        compiler_params=_cparams(("parallel", "arbitrary"), 40),
        name="decode_attention",
    )(page_table.reshape(-1), q, k_new, v_new, *([cache_k] * pp), *([cache_v] * pp),
      lam_vecs, subln_g.reshape(1, DV), slope_rows)


def _merge_body(x_ref, ya_ref, yb_ref, ga_ref, gb_ref, wpa_ref, wpb_ref, wo_ref, g2_ref, rw_ref, rb_ref,
                x1_ref, h2_ref, ti_ref, tg_ref):
    a = jnp.dot(ya_ref[...].astype(BF16), wpa_ref[...], preferred_element_type=F32)
    b = jnp.dot(yb_ref[...].astype(BF16), wpb_ref[...], preferred_element_type=F32)
    mixed = ga_ref[...].astype(F32) * a + gb_ref[...].astype(F32) * b
    x1 = x_ref[...] + jnp.dot(mixed.astype(BF16), wo_ref[...], preferred_element_type=F32)
    x1_ref[...] = x1
    h2 = _rmsnorm_rows(x1, g2_ref[...])
    h2_ref[...] = h2
    h_hi = h2.astype(BF16)
    h_lo = (h2 - h_hi.astype(F32)).astype(BF16)
    hw = jnp.dot(h_hi, rw_ref[...], preferred_element_type=F32)
    lw = jnp.dot(h_lo, rw_ref[:, 0:LANES], preferred_element_type=F32)
    lg = hw[:, 0:LANES] + (hw[:, LANES:2 * LANES] + lw) + rb_ref[...]
    lane = lax.broadcasted_iota(jnp.int32, lg.shape, 1)
    lane_f = lane.astype(F32)
    cur = jnp.where(lane < N_EXPERTS, lg, -jnp.inf)
    idx_out = jnp.zeros(lg.shape, F32)
    vals = []
    for k in range(TOP_K):
        mx = jnp.max(cur, axis=-1, keepdims=True)
        ix = jnp.min(jnp.where(cur == mx, lane_f, float(LANES)), axis=-1, keepdims=True)
        idx_out = jnp.where(lane == k, ix, idx_out)
        vals.append(mx)
        cur = jnp.where(lane_f == ix, -jnp.inf, cur)
    ex = [jnp.exp(v - vals[0]) for v in vals]
    tot = ex[0]
    for k in range(1, TOP_K):
        tot = tot + ex[k]
    gate_out = jnp.zeros(lg.shape, F32)
    for k in range(TOP_K):
        gate_out = jnp.where(lane == k, ex[k] / tot, gate_out)
    ti_ref[...] = idx_out.astype(jnp.int32)
    tg_ref[...] = gate_out


def merge(x, y_a, y_b, gates, w_pa, w_pb, w_o, norm2_g, router_w_pad, router_b_pad, tm):
    m, d = x.shape
    row = lambda i: (i, 0)
    return pl.pallas_call(
        _merge_body,
        grid=(m // tm,),
        in_specs=[
            pl.BlockSpec((tm, d), row),
            pl.BlockSpec((tm, W_SGU), row),
            pl.BlockSpec((tm, W_ATTN), row),
            pl.BlockSpec((tm, d), lambda i: (i, 0)),
            pl.BlockSpec((tm, d), lambda i: (i, 1)),
            _const_spec((W_SGU, d)),
            _const_spec((W_ATTN, d)),
            _const_spec((d, d)),
            _const_spec((1, d)),
            _const_spec((d, 2 * LANES)),
            _const_spec((1, LANES)),
        ],
        out_specs=[pl.BlockSpec((tm, d), row), pl.BlockSpec((tm, d), row),
                   pl.BlockSpec((tm, LANES), row), pl.BlockSpec((tm, LANES), row)],
        out_shape=[
            jax.ShapeDtypeStruct((m, d), F32),
            jax.ShapeDtypeStruct((m, d), F32),
            jax.ShapeDtypeStruct((m, LANES), jnp.int32),
            jax.ShapeDtypeStruct((m, LANES), F32),
        ],
        compiler_params=_cparams(("parallel",), 56),
        name="merge",
    )(x, y_a, y_b, gates, gates, w_pa, w_pb, w_o, norm2_g.reshape(1, d), router_w_pad, router_b_pad)


def _row_copies(idx_ref, base, n, src_hbm, buf, slot, sem, start):
    for r in range(n):
        cp = pltpu.make_async_copy(src_hbm.at[pl.ds(idx_ref[base + r], 1)], buf.at[slot, pl.ds(r, 1)],
                                   sem.at[slot])
        if start:
            cp.start(priority=r % 2)
        else:
            cp.wait()


def _dispatch_body(valid_ref, first_ref, idx_ref, h_hbm, o_ref, buf, sem, *, sb, n_steps):
    i = pl.program_id(0)
    slot = _mod_pow2(i, 2)
    nxt = jnp.minimum(i + 1, n_steps - 1)

    @pl.when((i == 0) & (valid_ref[0] > 0))
    def _():
        _row_copies(idx_ref, first_ref[0], sb, h_hbm, buf, 0, sem, True)

    @pl.when((i + 1 < n_steps) & (valid_ref[nxt] > 0))
    def _():
        _row_copies(idx_ref, first_ref[nxt], sb, h_hbm, buf, 1 - slot, sem, True)

    @pl.when(valid_ref[i] > 0)
    def _():
        _row_copies(idx_ref, first_ref[i], sb, h_hbm, buf, slot, sem, False)
        o_ref[...] = buf[slot].astype(o_ref.dtype)

    @pl.when(valid_ref[i] == 0)
    def _():
        o_ref[...] = jnp.zeros(o_ref.shape, o_ref.dtype)


def moe_dispatch(h2, sub_valid, sub_first, sorted_tok, n_rows):
    sb = MOE_SB
    d = h2.shape[1]
    n_steps = n_rows // sb
    grid_spec = pltpu.PrefetchScalarGridSpec(
        num_scalar_prefetch=3,
        grid=(n_steps,),
        in_specs=[pl.BlockSpec(memory_space=pl.ANY)],
        out_specs=pl.BlockSpec((sb, d), lambda i, va, ci, idx: (i, 0)),
        scratch_shapes=[pltpu.VMEM((2, sb, d), F32), pltpu.SemaphoreType.DMA((2,))],
    )
    return pl.pallas_call(
        functools.partial(_dispatch_body, sb=sb, n_steps=n_steps),
        grid_spec=grid_spec,
        out_shape=jax.ShapeDtypeStruct((n_rows, d), BF16),
        compiler_params=_cparams(("arbitrary",), 32),
        name="moe_dispatch",
    )(sub_valid, sub_first, sorted_tok, h2)


def _combine_body(pos_ref, x1_ref, g_ref, fg_ref, y_hbm, o_ref, buf, sem, *, tt, n_steps):
    i = pl.program_id(0)
    n = tt * TOP_K
    slot = _mod_pow2(i, 2)

    @pl.when(i == 0)
    def _():
        _row_copies(pos_ref, 0, n, y_hbm, buf, 0, sem, True)

    @pl.when(i + 1 < n_steps)
    def _():
        _row_copies(pos_ref, (i + 1) * n, n, y_hbm, buf, 1 - slot, sem, True)

    _row_copies(pos_ref, i * n, n, y_hbm, buf, slot, sem, False)
    rows = buf[slot]
    g = g_ref[...]
    moe = g[:, 0:1] * rows[0:tt]
    for k in range(1, TOP_K):
        moe = moe + g[:, k:k + 1] * rows[k * tt:(k + 1) * tt]
    o_ref[...] = _rmsnorm_rows(x1_ref[...] + moe, fg_ref[...])


def moe_combine_final(x1, pos, gates, y_rows, final_g):
    m, d = x1.shape
    tt = min(COMBINE_TT, m)
    n_steps = m // tt
    pos_km = pos.reshape(n_steps, tt, TOP_K).transpose(0, 2, 1).reshape(-1)
    grid_spec = pltpu.PrefetchScalarGridSpec(
        num_scalar_prefetch=1,
        grid=(n_steps,),
        in_specs=[
            pl.BlockSpec((tt, d), lambda i, p: (i, 0)),
            pl.BlockSpec((tt, TOP_K), lambda i, p: (i, 0)),
            pl.BlockSpec((1, d), lambda i, p: (0, 0)),
            pl.BlockSpec(memory_space=pl.ANY),
        ],
        out_specs=pl.BlockSpec((tt, d), lambda i, p: (i, 0)),
        scratch_shapes=[pltpu.VMEM((2, tt * TOP_K, d), F32), pltpu.SemaphoreType.DMA((2,))],
    )
    return pl.pallas_call(
        functools.partial(_combine_body, tt=tt, n_steps=n_steps),
        grid_spec=grid_spec,
        out_shape=jax.ShapeDtypeStruct((m, d), F32),
        compiler_params=_cparams(("arbitrary",), 32),
        name="moe_combine_final",
    )(pos_km, x1, gates, final_g.reshape(1, d), y_rows)


def _moe_body(we_ref, wb_ref, wn_ref, x_ref, wg_ref, wl_ref, bg_ref, bl_ref, wd_ref, bd_ref, y_ref,
              act_s, *, unit, tf, nf):
    w = pl.program_id(0)
    s = pl.program_id(1)
    n_u = wn_ref[w]
    rb = y_ref.shape[0]

    def rows_at(off, n):
        return pl.ds(pl.multiple_of(off, unit), n)

    def for_each_chunk(fn):
        n4 = lax.shift_right_logical(n_u, 2)

        def big(i, carry):
            fn(i * (4 * unit), 4 * unit)
            return carry

        lax.fori_loop(0, n4, big, 0)
        off2 = n4 * (4 * unit)

        @pl.when(jnp.bitwise_and(n_u, 2) != 0)
        def _():
            fn(off2, 2 * unit)

        @pl.when(jnp.bitwise_and(n_u, 1) != 0)
        def _():
            fn(off2 + jnp.bitwise_and(n_u, 2) * unit, unit)

    @pl.when((s < nf) & (n_u > 0))
    def _():
        def chunk(off, n):
            rows = rows_at(off, n)
            xb = x_ref[rows, :]
            glu = jnp.dot(xb, wg_ref[...].astype(BF16), preferred_element_type=F32) + bg_ref[...]
            lin = jnp.dot(xb, wl_ref[...].astype(BF16), preferred_element_type=F32) + bl_ref[...]
            glu = jnp.minimum(glu, SWIGLU_LIMIT)
            lin = jnp.clip(lin, -SWIGLU_LIMIT, SWIGLU_LIMIT)
            act = glu * jax.nn.sigmoid(SWIGLU_ALPHA * glu) * (lin + 1.0)
            act_s[s, rows, :] = act.astype(BF16)

        for_each_chunk(chunk)

    @pl.when(s >= nf)
    def _():
        @pl.when(n_u > 0)
        def _():
            def chunk(off, n):
                rows = rows_at(off, n)
                a = jnp.concatenate([act_s[k, rows, :] for k in range(nf)], axis=1)
                y_ref[rows, :] = (jnp.dot(a, wd_ref[...].astype(BF16), preferred_element_type=F32)
                                  + bd_ref[...])

            for_each_chunk(chunk)

        def clear(i, carry):
            y_ref[rows_at(i * unit, unit), :] = jnp.zeros((unit, y_ref.shape[1]), F32)
            return carry

        lax.fori_loop(n_u, rb // unit, clear, 0)


def moe_experts(x_rows, work_e, work_blk, work_nsub, w_gu, b_gu, w_down, b_down):
    n_rows, d = x_rows.shape
    rb, tf, tn = MOE_RB, MOE_TF, MOE_TN
    nw = n_rows // rb
    nf = D_FF // tf
    nd = d // tn

    def down(s):
        return jnp.maximum(s - nf, 0)

    def up_w(w, s, wn):
        return jnp.where(wn[w] > 0, jnp.minimum(s, nf - 1), nf - 1)

    def down_w(w, s, wn):
        return jnp.where(wn[w] > 0, down(s), nd - 1)

    grid_spec = pltpu.PrefetchScalarGridSpec(
        num_scalar_prefetch=3,
        grid=(nw, nf + nd),
        in_specs=[
            pl.BlockSpec((rb, d), lambda w, s, we, wb, wn: (wb[w], 0)),
            pl.BlockSpec((None, d, tf), lambda w, s, we, wb, wn: (we[w], 0, up_w(w, s, wn))),
            pl.BlockSpec((None, d, tf), lambda w, s, we, wb, wn: (we[w], 0, nf + up_w(w, s, wn))),
            pl.BlockSpec((None, 1, tf), lambda w, s, we, wb, wn: (we[w], 0, up_w(w, s, wn))),
            pl.BlockSpec((None, 1, tf), lambda w, s, we, wb, wn: (we[w], 0, nf + up_w(w, s, wn))),
            pl.BlockSpec((None, D_FF, tn), lambda w, s, we, wb, wn: (we[w], 0, down_w(w, s, wn))),
            pl.BlockSpec((None, 1, tn), lambda w, s, we, wb, wn: (we[w], 0, down_w(w, s, wn))),
        ],
        out_specs=pl.BlockSpec((rb, tn), lambda w, s, we, wb, wn: (w, down(s))),
        scratch_shapes=[pltpu.VMEM((nf, rb, tf), BF16)],
    )
    return pl.pallas_call(
        functools.partial(_moe_body, unit=MOE_UNIT, tf=tf, nf=nf),
        grid_spec=grid_spec,
        out_shape=jax.ShapeDtypeStruct((n_rows, d), F32),
        compiler_params=_cparams(("arbitrary", "arbitrary"), 58),
        name="moe_experts",
    )(work_e, work_blk, work_nsub, x_rows, w_gu, w_gu, b_gu.reshape(N_EXPERTS, 1, 2 * D_FF),
      b_gu.reshape(N_EXPERTS, 1, 2 * D_FF), w_down, b_down.reshape(N_EXPERTS, 1, d))


def _routing(top_idx):
    n_tok = top_idx.shape[0]
    n_assign = n_tok * TOP_K
    rb, sb, unit = MOE_RB, MOE_SB, MOE_UNIT
    spw = rb // sb
    nw = n_assign // rb + N_EXPERTS
    flat_e = top_idx.reshape(-1).astype(jnp.int32)
    cb = 256
    assert n_assign % cb == 0
    onehot = (flat_e[:, None] == jnp.arange(N_EXPERTS, dtype=jnp.int32)[None, :]).astype(F32)
    tri = jnp.tril(jnp.ones((cb, cb), F32))
    within = jnp.einsum("ij,bjk->bik", tri, onehot.reshape(n_assign // cb, cb, N_EXPERTS))
    bsum = within[:, -1, :]
    csum = (within + (jnp.cumsum(bsum, axis=0) - bsum)[:, None, :]).reshape(n_assign, N_EXPERTS).astype(jnp.int32)
    rank = jnp.take_along_axis(csum, flat_e[:, None], axis=1)[:, 0] - 1
    counts = csum[-1]
    nwe = (counts + rb - 1) // rb
    w_end = jnp.cumsum(nwe)
    w_start = w_end - nwe
    dest = w_start[flat_e] * rb + rank
    widx = jnp.arange(nw, dtype=jnp.int32)
    n_used = w_end[-1]
    valid = widx < n_used
    we = jnp.minimum(jnp.searchsorted(w_end, widx, side="right"), N_EXPERTS - 1).astype(jnp.int32)
    rows_valid = jnp.clip(counts[we] - (widx - w_start[we]) * rb, 0, rb)
    n_units = jnp.where(valid, (rows_valid + unit - 1) // unit, 0).astype(jnp.int32)
    last = jnp.maximum(n_used - 1, 0)
    work_e = jnp.where(valid, we, we[last]).astype(jnp.int32)
    work_blk = jnp.where(valid, widx, last).astype(jnp.int32)
    jblk = jnp.arange(spw, dtype=jnp.int32)[None, :]
    sub_valid = (jblk * (sb // unit) < n_units[:, None]).astype(jnp.int32).reshape(-1)
    a_start = jnp.cumsum(counts) - counts
    first_w = a_start[we] + (widx - w_start[we]) * rb
    sub_first = jnp.clip(first_w[:, None] + jblk * sb, 0, n_assign).astype(jnp.int32).reshape(-1)
    sorted_a = jnp.argsort(flat_e, stable=True).astype(jnp.int32)
    sorted_tok = jnp.concatenate([sorted_a // TOP_K, jnp.zeros((sb,), jnp.int32)])
    return dest, work_e, work_blk, n_units, nw, sub_valid, sub_first, sorted_tok


def _layer_group(x2d, tm, w_in_bf, p, act_dtype):
    h = rmsnorm_to(x2d, p["norm1_g"], tm, BF16)
    (u,) = in_proj(h, w_in_bf, 0, 1, "gelu", tm, [act_dtype])
    (v,) = in_proj(h, w_in_bf, 1, 1, "gelu_ln", tm, [F32],
                   extra=(p["sgu_ln_g"].reshape(1, -1), p["sgu_ln_b"].reshape(1, -1)))
    (q,) = in_proj(h, w_in_bf, 2, 1, "scale", tm, [act_dtype])
    k32, kbf = in_proj(h, w_in_bf, 3, 1, "dual", tm, [F32, BF16])
    va32, vabf = in_proj(h, w_in_bf, 4, 1, "dual", tm, [F32, BF16])
    (gates,) = in_proj(h, w_in_bf, 5, 4, "sigmoid", tm, [BF16])
    return u, v, q, k32, kbf, va32, vabf, gates


def kernel(x_prompt, x_sample, cache_k, cache_v, page_table, norm1_g, w_in, sgu_ln_g, sgu_ln_b, sgu_w, sgu_b,
           lambda_q1, lambda_k1, lambda_q2, lambda_k2, subln_g, w_branch_a, w_branch_b, w_out, norm2_g,
           router_w, router_b, expert_w_gu, expert_b_gu, expert_w_down, expert_b_down, final_norm_g):
    depth = w_in.shape[0]
    assert depth == 1
    l = 0
    bp, sp, d = x_prompt.shape
    bs, ts, _ = x_sample.shape
    lam_init = 0.8 - 0.6 * math.exp(-0.3 * l)
    slopes2 = jnp.exp2(-(8.0 / N_HEADS) * jnp.arange(1, N_HEADS + 1, dtype=F32)) * LOG2E
    lam_vecs = jnp.stack([lambda_q1[l], lambda_k1[l], lambda_q2[l], lambda_k2[l]]).astype(F32)

    w_in_bf = w_in[l]
    w_pa = w_branch_a[l].astype(BF16)
    w_pb = w_branch_b[l].astype(BF16)
    w_o = w_out[l].astype(BF16)
    rw32 = jnp.pad(router_w[l].astype(F32), ((0, 0), (0, LANES - N_EXPERTS)))
    rw_hi = rw32.astype(BF16)
    rw_pad = jnp.concatenate([rw_hi, (rw32 - rw_hi.astype(F32)).astype(BF16)], axis=1)
    rb_pad = jnp.pad(router_b[l].astype(F32), (0, LANES - N_EXPERTS)).reshape(1, LANES)
    p = {"norm1_g": norm1_g[l], "sgu_ln_g": sgu_ln_g[l], "sgu_ln_b": sgu_ln_b[l]}

    xp = x_prompt.reshape(bp * sp, d)
    u, v, q, k32p, kbf, va32p, vabf, gates = _layer_group(xp, PROJ_TM, w_in_bf, p, BF16)
    y_a = sgu(u, v, sgu_w[l], sgu_b[l], CHUNK, 4, BF16)
    y_b = prompt_attention(q, kbf, vabf, lam_vecs, subln_g[l], slopes2, bp, sp, lam_init)
    x1p, h2p, tip, tgp = merge(xp, y_a, y_b, gates, w_pa, w_pb, w_o, norm2_g[l], rw_pad, rb_pad, MERGE_TM)

    xs = x_sample.reshape(bs * ts, d)
    u, v_s, q, k32s, _, va32s, _, gates = _layer_group(xs, bs * ts, w_in_bf, p, F32)
    y_a = sgu(u, v_s, sgu_w[l], sgu_b[l], ts, bs, F32)
    k_sample = k32s.reshape(bs, ts, N_HEADS, 2 * DK)
    v_sample = va32s.reshape(bs, ts, N_HEADS, DV)
    y_b = decode_attention(q, k_sample, v_sample, cache_k, cache_v, l, page_table,
                           lam_vecs, subln_g[l], slopes2, lam_init)
    x1s, h2s, tis, tgs = merge(xs, y_a, y_b, gates, w_pa, w_pb, w_o, norm2_g[l], rw_pad, rb_pad, bs * ts)

    h2 = jnp.concatenate([h2p, h2s], axis=0)
    top_idx = jnp.concatenate([tip[:, :TOP_K], tis[:, :TOP_K]], axis=0)
    gate_w = jnp.concatenate([tgp[:, :TOP_K], tgs[:, :TOP_K]], axis=0)
    dest, work_e, work_blk, work_units, nw, sub_valid, sub_first, sorted_tok = _routing(top_idx)
    x_rows = moe_dispatch(h2, sub_valid, sub_first, sorted_tok, nw * MOE_RB)
    y_rows = moe_experts(x_rows, work_e, work_blk, work_units, expert_w_gu[l], expert_b_gu[l],
                         expert_w_down[l], expert_b_down[l])
    n_p = bp * sp
    pos = dest.reshape(-1, TOP_K)
    y_prompt = moe_combine_final(x1p, pos[:n_p], gate_w[:n_p], y_rows, final_norm_g).reshape(bp, sp, d)
    y_sample = moe_combine_final(x1s, pos[n_p:], gate_w[n_p:], y_rows, final_norm_g).reshape(bs, ts, d)

    k_prompt = k32p.reshape(1, bp, sp, N_HEADS, 2 * DK)
    v_prompt = va32p.reshape(1, bp, sp, N_HEADS, DV)
    state_sgu_v = v_s.reshape(1, bs, ts, W_SGU)
    return (y_prompt, y_sample, k_prompt, v_prompt, k_sample[None], v_sample[None], state_sgu_v)
```

```python
import functools
import math

import jax
import jax.numpy as jnp
from jax import lax
from jax.experimental import pallas as pl
from jax.experimental.pallas import tpu as pltpu

F32 = jnp.float32
BF16 = jnp.bfloat16

D_MODEL = 2048
N_HEADS = 8
DK = 64
DV = 2 * DK
W_ATTN = N_HEADS * DV
D_QK = 2 * N_HEADS * DK
W_SGU = D_MODEL // 2
N_SGU_GROUPS = 8
SGU_GROUP = W_SGU // N_SGU_GROUPS
CHUNK = 128
N_EXPERTS = 32
TOP_K = 4
D_FF = D_MODEL
SWIGLU_LIMIT = 7.0
SWIGLU_ALPHA = 1.702
EPS = 1e-6
PAGE_SIZE = 128
D_IN = 2 * W_SGU + 2 * D_QK + W_ATTN + 2 * D_MODEL
LOG2E = math.log2(math.e)

LANES = 128
SUBLANES = 8
MIB = 1024 * 1024

PROJ_TM = 1024
PROJ_TN = 1024
PROJ_RC = 512
ATT_TQ = 512
ATT_TK = 512
ATT_RB = 256
DEC_PP = 8
MERGE_TM = 256
MOE_RB = 1280
MOE_SB = 640
MOE_UNIT = 128
MOE_TF = 512
MOE_TN = 512
COMBINE_TT = 128
NEG_BIG = -1e30


def _cparams(semantics, vmem_mib):
    return pltpu.CompilerParams(dimension_semantics=semantics, vmem_limit_bytes=vmem_mib * MIB)


def _const_spec(shape):
    nd = len(shape)
    return pl.BlockSpec(shape, lambda *_: (0,) * nd, pipeline_mode=pl.Buffered(1))


def _div_pow2(x, n):
    assert n & (n - 1) == 0
    return lax.shift_right_logical(x, n.bit_length() - 1)


def _mod_pow2(x, n):
    assert n & (n - 1) == 0
    return jnp.bitwise_and(x, n - 1)


def _rmsnorm_rows(x, g):
    ms = jnp.mean(x * x, axis=-1, keepdims=True)
    return x * lax.rsqrt(ms + EPS) * g


def _rmsnorm_body(x_ref, g_ref, o_ref):
    o_ref[...] = _rmsnorm_rows(x_ref[...], g_ref[...]).astype(o_ref.dtype)


def rmsnorm_to(x2d, g, tm, dtype):
    m, d = x2d.shape
    return pl.pallas_call(
        _rmsnorm_body,
        grid=(m // tm,),
        in_specs=[pl.BlockSpec((tm, d), lambda i: (i, 0)), pl.BlockSpec((1, d), lambda i: (0, 0))],
        out_specs=pl.BlockSpec((tm, d), lambda i: (i, 0)),
        out_shape=jax.ShapeDtypeStruct((m, d), dtype),
        compiler_params=_cparams(("parallel",), 48),
        name="rmsnorm",
    )(x2d, g.reshape(1, d))


def _gelu(z):
    return 0.5 * z * (1.0 + lax.erf(z * (2.0 ** -0.5)))


def _proj_body(h_ref, w_ref, *refs, kind, rc):
    tm = h_ref.shape[0]
    *refs, w_bf = refs

    @pl.when(pl.program_id(1) == 0)
    def _():
        w_bf[...] = w_ref[...].astype(BF16)

    def chunk(c, carry):
        r = pl.multiple_of(c * rc, rc)
        rows = pl.ds(r, rc)
        z = jnp.dot(h_ref[rows, :], w_bf[...], preferred_element_type=F32)
        if kind == "gelu":
            refs[0][rows, :] = _gelu(z).astype(refs[0].dtype)
        elif kind == "gelu_ln":
            g_ref, b_ref, o_ref = refs
            a = _gelu(z)
            mu = jnp.mean(a, axis=-1, keepdims=True)
            var = jnp.mean(jnp.square(a - mu), axis=-1, keepdims=True)
            y = (a - mu) * lax.rsqrt(var + EPS)
            o_ref[rows, :] = (y * g_ref[...] + b_ref[...]).astype(o_ref.dtype)
        elif kind == "scale":
            refs[0][rows, :] = (z * (DK ** -0.5 * LOG2E)).astype(refs[0].dtype)
        elif kind == "dual":
            refs[0][rows, :] = z
            refs[1][rows, :] = z.astype(refs[1].dtype)
        elif kind == "sigmoid":
            refs[0][rows, :] = jax.nn.sigmoid(z).astype(refs[0].dtype)
        return carry

    lax.fori_loop(0, tm // rc, chunk, 0)


def in_proj(h, w, col_blk, n_blk, kind, tm, out_dtypes, extra=()):
    m, d = h.shape
    tn = PROJ_TN
    rc = min(PROJ_RC, tm)
    in_specs = [
        pl.BlockSpec((tm, d), lambda n, i: (i, 0)),
        pl.BlockSpec((d, tn), lambda n, i: (0, col_blk + n)),
    ] + [pl.BlockSpec((1, tn), lambda n, i: (0, 0)) for _ in extra]
    out_specs = [pl.BlockSpec((tm, tn), lambda n, i: (i, n)) for _ in out_dtypes]
    out_shape = [jax.ShapeDtypeStruct((m, n_blk * tn), dt) for dt in out_dtypes]
    res = pl.pallas_call(
        functools.partial(_proj_body, kind=kind, rc=rc),
        grid=(n_blk, m // tm),
        in_specs=in_specs,
        out_specs=out_specs,
        out_shape=out_shape,
        scratch_shapes=[pltpu.VMEM((d, tn), BF16)],
        compiler_params=_cparams(("parallel", "arbitrary"), 56),
        name="in_proj_" + kind,
    )(h, w, *extra)
    return res


def _sgu_body(u_ref, v_ref, w_ref, bt_ref, o_ref, *, rows, n_chunks):
    ii = lax.broadcasted_iota(jnp.int32, (CHUNK, CHUNK), 0)
    jj = lax.broadcasted_iota(jnp.int32, (CHUNK, CHUNK), 1)
    causal = jj <= ii
    for c in range(n_chunks):
        r0 = c * rows
        vb = v_ref[r0:r0 + rows, :]
        if rows < CHUNK:
            vb = jnp.concatenate([vb, jnp.zeros((CHUNK - rows, vb.shape[1]), vb.dtype)], axis=0)
        vb = vb.astype(BF16)
        for g in range(N_SGU_GROUPS):
            w = jnp.where(causal, w_ref[g], 0.0).astype(BF16)
            cols = slice(g * SGU_GROUP, (g + 1) * SGU_GROUP)
            f = jnp.dot(w, vb[:, cols], preferred_element_type=F32) + bt_ref[:, g:g + 1]
            u = u_ref[r0:r0 + rows, cols].astype(F32)
            o_ref[r0:r0 + rows, cols] = (u * f[:rows]).astype(o_ref.dtype)


def sgu(u, v, w_s, b_s, rows, n_chunks, out_dtype):
    m, w = u.shape
    tm = rows * n_chunks
    return pl.pallas_call(
        functools.partial(_sgu_body, rows=rows, n_chunks=n_chunks),
        grid=(m // tm,),
        in_specs=[
            pl.BlockSpec((tm, w), lambda i: (i, 0)),
            pl.BlockSpec((tm, w), lambda i: (i, 0)),
            pl.BlockSpec((N_SGU_GROUPS, CHUNK, CHUNK), lambda i: (0, 0, 0)),
            pl.BlockSpec((CHUNK, N_SGU_GROUPS), lambda i: (0, 0)),
        ],
        out_specs=pl.BlockSpec((tm, w), lambda i: (i, 0)),
        out_shape=jax.ShapeDtypeStruct((m, w), out_dtype),
        compiler_params=_cparams(("parallel",), 32),
        name="sgu",
    )(u, v, w_s, b_s.T)


def _lambda_value(lam_ref, lam_init):
    lv = lam_ref[...]
    d1 = jnp.sum(lv[0:1] * lv[1:2], axis=-1, keepdims=True)
    d2 = jnp.sum(lv[2:3] * lv[3:4], axis=-1, keepdims=True)
    return jnp.exp(d1) - jnp.exp(d2) + lam_init


def _head_out(o0, o1, lam, g, lam_init):
    o = o0 - lam * o1
    return _rmsnorm_rows(o, g) * (1.0 - lam_init)


def _flash_body(q_ref, k_ref, v_ref, lam_ref, g_ref, slope_ref, o_ref, vaug, m_s, acc, *, tq, tk, rb, lam_init):
    qi = pl.program_id(2)
    n_rb = tq // rb
    chains = [(c, r) for c in range(2) for r in range(n_rb)]

    @pl.when(qi == 0)
    def _():
        vaug[:, 0:DV] = v_ref[...]
        vaug[:, DV:2 * DV] = jnp.ones((vaug.shape[0], DV), BF16)

    q = q_ref[...]
    lane = lax.broadcasted_iota(jnp.int32, q.shape, 1)
    zero = jnp.zeros_like(q)
    qmap = [jnp.where(lane < DK, q, zero), jnp.where(lane >= DK, q, zero)]
    slope = slope_ref[...]
    col = lax.broadcasted_iota(jnp.int32, (1, tk), 1)
    q0 = qi * tq

    m_s[...] = jnp.full(m_s.shape, NEG_BIG, F32)
    acc[...] = jnp.zeros(acc.shape, F32)

    def step(j, masked):
        k0 = pl.multiple_of(j * tk, tk)
        kj = k_ref[pl.ds(k0, tk), :]
        vj = vaug[pl.ds(k0, tk), :]
        bias = slope * (col + (k0 - q0)).astype(F32)
        for ci, (c, r) in enumerate(chains):
            rows = slice(ci * rb, (ci + 1) * rb)
            qc = qmap[c][r * rb:(r + 1) * rb]
            nk = (r + 1) * rb if masked else tk
            s = lax.dot_general(qc, kj[0:nk], (((1,), (1,)), ((), ())), preferred_element_type=F32)
            s = s + bias[:, 0:nk]
            if masked:
                rr = lax.broadcasted_iota(jnp.int32, s.shape, 0) + (r * rb)
                cc = lax.broadcasted_iota(jnp.int32, s.shape, 1)
                s = jnp.where(cc <= rr, s, NEG_BIG)
            m_old = m_s[rows, :]
            m_new = jnp.maximum(m_old, jnp.max(s, axis=-1, keepdims=True))
            alpha = jnp.exp2(m_old - m_new)
            p = jnp.exp2(s - jnp.tile(m_new, (1, nk // LANES)))
            pv = jnp.dot(p.astype(BF16), vj[0:nk], preferred_element_type=F32)
            acc[rows, :] = jnp.tile(alpha, (1, 2 * DV // LANES)) * acc[rows, :] + pv
            m_s[rows, :] = m_new

    def four_full_steps(i, carry):
        for u in range(4):
            step(4 * i + u, False)
        return carry

    n4 = lax.shift_right_logical(qi, 2)
    lax.fori_loop(0, n4, four_full_steps, 0)
    j0 = n4 * 4
    has2 = jnp.bitwise_and(qi, 2) != 0
    has1 = jnp.bitwise_and(qi, 1) != 0

    @pl.when(has2)
    def _():
        step(j0, False)
        step(j0 + 1, False)

    @pl.when(has1)
    def _():
        step(qi - 1, False)
        step(qi, True)

    @pl.when(jnp.logical_not(has1))
    def _():
        step(qi, True)

    lam = _lambda_value(lam_ref, lam_init)
    for r in range(n_rb):
        a0 = acc[r * rb:(r + 1) * rb, :]
        a1 = acc[(n_rb + r) * rb:(n_rb + r + 1) * rb, :]
        o0 = a0[:, 0:DV] / a0[:, DV:2 * DV]
        o1 = a1[:, 0:DV] / a1[:, DV:2 * DV]
        o_ref[r * rb:(r + 1) * rb, :] = _head_out(o0, o1, lam, g_ref[...], lam_init).astype(o_ref.dtype)


def prompt_attention(q, k, v, lam_vecs, subln_g, slopes2, batch, seq, lam_init):
    tq, tk, rb = ATT_TQ, ATT_TK, ATT_RB
    assert tq == tk and seq % tq == 0 and tq % rb == 0
    nq = seq // tq
    n_chain_rows = 2 * tq
    slope_b = jnp.broadcast_to(slopes2[:, None, None], (N_HEADS, 1, tk)).astype(F32)
    return pl.pallas_call(
        functools.partial(_flash_body, tq=tq, tk=tk, rb=rb, lam_init=lam_init),
        grid=(batch, N_HEADS, nq),
        in_specs=[
            pl.BlockSpec((tq, DV), lambda b, h, i: (b * nq + i, h)),
            pl.BlockSpec((seq, DV), lambda b, h, i: (b, h)),
            pl.BlockSpec((seq, DV), lambda b, h, i: (b, h)),
            pl.BlockSpec((4, DK), lambda b, h, i: (0, 0)),
            pl.BlockSpec((1, DV), lambda b, h, i: (0, 0)),
            pl.BlockSpec((None, 1, tk), lambda b, h, i: (h, 0, 0)),
        ],
        out_specs=pl.BlockSpec((tq, DV), lambda b, h, i: (b * nq + i, h)),
        out_shape=jax.ShapeDtypeStruct((batch * seq, W_ATTN), BF16),
        scratch_shapes=[
            pltpu.VMEM((seq, 2 * DV), BF16),
            pltpu.VMEM((n_chain_rows, LANES), F32),
            pltpu.VMEM((n_chain_rows, 2 * DV), F32),
        ],
        compiler_params=_cparams(("parallel", "parallel", "arbitrary"), 40),
        name="prompt_attention",
    )(q, k, v, lam_vecs, subln_g.reshape(1, DV), slope_b)


def _decode_body(pt_ref, q_ref, kn_ref, vn_ref, *rest, pp, n_steps, t_new, past, lam_init):
    kp = rest[:pp]
    vp = rest[pp:2 * pp]
    lam_ref, g_ref, slope_ref, o_ref, qx, bias0, m_s, l_s, acc = rest[2 * pp:]
    j = pl.program_id(1)
    nt = (((1,), (1,)), ((), ()))
    nrow = 2 * N_HEADS * t_new
    ncol = PAGE_SIZE * N_HEADS
    slope = slope_ref[...]
    rep = ncol // LANES

    @pl.when(j == 0)
    def _():
        q = q_ref[...]
        lane = lax.broadcasted_iota(jnp.int32, (t_new, 2 * DK), 1)
        parts = []
        for h in range(N_HEADS):
            qh = q[:, h * 2 * DK:(h + 1) * 2 * DK]
            parts.append(jnp.where(lane < DK, qh, 0.0))
            parts.append(jnp.where(lane >= DK, qh, 0.0))
        qx[...] = jnp.concatenate(parts, axis=0).astype(BF16)
        rr = lax.broadcasted_iota(jnp.int32, (nrow, ncol), 0)
        cc = lax.broadcasted_iota(jnp.int32, (nrow, ncol), 1)
        head_ok = _mod_pow2(cc, N_HEADS) == _div_pow2(rr, 2 * t_new)
        pos = _div_pow2(cc, N_HEADS).astype(F32)
        bias0[...] = jnp.where(head_ok, jnp.tile(slope, (1, rep)) * pos, NEG_BIG)
        n_new = t_new * N_HEADS
        pad = jnp.zeros((LANES - n_new, 2 * DK), F32)
        kn = jnp.concatenate([kn_ref[...].reshape(n_new, 2 * DK), pad], axis=0).astype(BF16)
        vn = jnp.concatenate([vn_ref[...].reshape(n_new, DV), pad], axis=0).astype(BF16)
        s = lax.dot_general(qx[...], kn, nt, preferred_element_type=F32)
        r1 = lax.broadcasted_iota(jnp.int32, s.shape, 0)
        c1 = lax.broadcasted_iota(jnp.int32, s.shape, 1)
        tk = _div_pow2(c1, N_HEADS)
        ok = (c1 < n_new) & (_mod_pow2(c1, N_HEADS) == _div_pow2(r1, 2 * t_new)) & (tk <= _mod_pow2(r1, t_new))
        s = jnp.where(ok, s + slope * tk.astype(F32), NEG_BIG)
        m = jnp.max(s, axis=-1, keepdims=True)
        p = jnp.exp2(s - m)
        m_s[...] = jnp.broadcast_to(m, m_s.shape)
        l_s[...] = jnp.broadcast_to(jnp.sum(p, axis=-1, keepdims=True), l_s.shape)
        acc[...] = jnp.dot(p.astype(BF16), vn, preferred_element_type=F32)

    q2 = qx[...]
    b0 = bias0[...]
    ts, offs = [], []
    m_old = m_s[...]
    m_new = m_old
    for i in range(pp):
        kpage = kp[i][...].reshape(ncol, 2 * DK).astype(BF16)
        t_i = lax.dot_general(q2, kpage, nt, preferred_element_type=F32) + b0
        off_i = slope * ((j * pp + i) * PAGE_SIZE - past).astype(F32)
        m_new = jnp.maximum(m_new, jnp.max(t_i, axis=-1, keepdims=True) + off_i)
        ts.append(t_i)
        offs.append(off_i)
    alpha = jnp.exp2(m_old - m_new)
    lsum = None
    pv = None
    for i in range(pp):
        p_i = jnp.exp2(ts[i] - jnp.tile(m_new - offs[i], (1, rep)))
        vpage = vp[i][...].reshape(ncol, DV).astype(BF16)
        d = jnp.dot(p_i.astype(BF16), vpage, preferred_element_type=F32)
        r = jnp.sum(p_i, axis=-1, keepdims=True)
        pv = d if pv is None else pv + d
        lsum = r if lsum is None else lsum + r
    l_s[...] = alpha * l_s[...] + lsum
    acc[...] = alpha * acc[...] + pv
    m_s[...] = m_new

    @pl.when(j == n_steps - 1)
    def _():
        lam = _lambda_value(lam_ref, lam_init)
        on = acc[...] / l_s[...]
        for h in range(N_HEADS):
            r0 = h * 2 * t_new
            o = _head_out(on[r0:r0 + t_new], on[r0 + t_new:r0 + 2 * t_new], lam, g_ref[...], lam_init)
            o_ref[:, h * DV:(h + 1) * DV] = o.astype(o_ref.dtype)


def decode_attention(q, k_new, v_new, cache_k, cache_v, layer, page_table, lam_vecs, subln_g, slopes2, lam_init):
    bd, n_pages = page_table.shape
    t_new = q.shape[0] // bd
    past = n_pages * PAGE_SIZE
    pp = DEC_PP
    n_steps = n_pages // pp
    nrow = 2 * N_HEADS * t_new
    ncol = PAGE_SIZE * N_HEADS
    assert nrow == LANES and t_new * N_HEADS <= LANES and n_pages % pp == 0
    width = N_HEADS * DV
    slope_rows = jnp.broadcast_to(jnp.repeat(slopes2, 2 * t_new)[:, None], (nrow, LANES)).astype(F32)

    def page_spec(i):
        return pl.BlockSpec((None, None, PAGE_SIZE, N_HEADS, DV),
                            lambda b, j, pt: (layer, pt[b * n_pages + j * pp + i], 0, 0, 0))

    new_spec = pl.BlockSpec((None, t_new, N_HEADS, DV), lambda b, j, pt: (b, 0, 0, 0))
    grid_spec = pltpu.PrefetchScalarGridSpec(
        num_scalar_prefetch=1,
        grid=(bd, n_steps),
        in_specs=[pl.BlockSpec((t_new, width), lambda b, j, pt: (b, 0)), new_spec, new_spec]
        + [page_spec(i) for i in range(pp)] + [page_spec(i) for i in range(pp)] + [
            pl.BlockSpec((4, DK), lambda b, j, pt: (0, 0)),
            pl.BlockSpec((1, DV), lambda b, j, pt: (0, 0)),
            pl.BlockSpec((nrow, LANES), lambda b, j, pt: (0, 0)),
        ],
        out_specs=pl.BlockSpec((t_new, width), lambda b, j, pt: (b, 0)),
        scratch_shapes=[
            pltpu.VMEM((nrow, 2 * DK), BF16),
            pltpu.VMEM((nrow, ncol), F32),
            pltpu.VMEM((nrow, LANES), F32),
            pltpu.VMEM((nrow, LANES), F32),
            pltpu.VMEM((nrow, DV), F32),
        ],
    )
    return pl.pallas_call(
        functools.partial(_decode_body, pp=pp, n_steps=n_steps, t_new=t_new, past=past, lam_init=lam_init),
        grid_spec=grid_spec,
        out_shape=jax.ShapeDtypeStruct((bd * t_new, width), F32),
        compiler_params=_cparams(("parallel", "arbitrary"), 40),
        name="decode_attention",
    )(page_table.reshape(-1), q, k_new, v_new, *([cache_k] * pp), *([cache_v] * pp),
      lam_vecs, subln_g.reshape(1, DV), slope_rows)


def _merge_rows(x, ya, yb, ga, gb, wpa_ref, wpb_ref, wo_ref, g2_ref, rw_ref, rb_ref):
    a = jnp.dot(ya.astype(BF16), wpa_ref[...], preferred_element_type=F32)
    b = jnp.dot(yb.astype(BF16), wpb_ref[...], preferred_element_type=F32)
    mixed = ga.astype(F32) * a + gb.astype(F32) * b
    x1 = x + jnp.dot(mixed.astype(BF16), wo_ref[...], preferred_element_type=F32)
    h2 = _rmsnorm_rows(x1, g2_ref[...])
    h_hi = h2.astype(BF16)
    h_lo = (h2 - h_hi.astype(F32)).astype(BF16)
    hw = jnp.dot(h_hi, rw_ref[...], preferred_element_type=F32)
    lw = jnp.dot(h_lo, rw_ref[:, 0:LANES], preferred_element_type=F32)
    lg = hw[:, 0:LANES] + (hw[:, LANES:2 * LANES] + lw) + rb_ref[...]
    lane = lax.broadcasted_iota(jnp.int32, lg.shape, 1)
    lane_f = lane.astype(F32)
    cur = jnp.where(lane < N_EXPERTS, lg, -jnp.inf)
    idx_out = jnp.zeros(lg.shape, F32)
    vals = []
    for k in range(TOP_K):
        mx = jnp.max(cur, axis=-1, keepdims=True)
        ix = jnp.min(jnp.where(cur == mx, lane_f, float(LANES)), axis=-1, keepdims=True)
        idx_out = jnp.where(lane == k, ix, idx_out)
        vals.append(mx)
        cur = jnp.where(lane_f == ix, -jnp.inf, cur)
    ex = [jnp.exp(v - vals[0]) for v in vals]
    tot = ex[0]
    for k in range(1, TOP_K):
        tot = tot + ex[k]
    gate_out = jnp.zeros(lg.shape, F32)
    for k in range(TOP_K):
        gate_out = jnp.where(lane == k, ex[k] / tot, gate_out)
    return x1, h2, idx_out.astype(jnp.int32), gate_out


def _merge_body(xp_ref, yap_ref, ybp_ref, gap_ref, gbp_ref, xs_ref, yas_ref, ybs_ref, gas_ref, gbs_ref,
                wpa_ref, wpb_ref, wo_ref, g2_ref, rw_ref, rb_ref, x1_ref, h2_ref, ti_ref, tg_ref, *, n_p):
    i = pl.program_id(0)
    weights = (wpa_ref, wpb_ref, wo_ref, g2_ref, rw_ref, rb_ref)
    outs = (x1_ref, h2_ref, ti_ref, tg_ref)

    @pl.when(i < n_p)
    def _():
        res = _merge_rows(xp_ref[...], yap_ref[...], ybp_ref[...], gap_ref[...], gbp_ref[...], *weights)
        for o_ref, r in zip(outs, res):
            o_ref[...] = r

    @pl.when(i == n_p)
    def _():
        ms = xs_ref.shape[0]
        res = _merge_rows(xs_ref[...], yas_ref[...], ybs_ref[...], gas_ref[...], gbs_ref[...], *weights)
        for o_ref, r in zip(outs, res):
            o_ref[0:ms, :] = r


def merge(xp, ya_p, yb_p, gates_p, xs, ya_s, yb_s, gates_s, w_pa, w_pb, w_o, norm2_g, router_w_pad, router_b_pad,
          tm):
    mp, d = xp.shape
    ms = xs.shape[0]
    assert mp % tm == 0 and ms <= tm
    n_p = mp // tm
    m = mp + ms
    prow = lambda i: (jnp.minimum(i, n_p - 1), 0)
    row = lambda i: (i, 0)
    return pl.pallas_call(
        functools.partial(_merge_body, n_p=n_p),
        grid=(n_p + 1,),
        in_specs=[
            pl.BlockSpec((tm, d), prow),
            pl.BlockSpec((tm, W_SGU), prow),
            pl.BlockSpec((tm, W_ATTN), prow),
            pl.BlockSpec((tm, d), lambda i: (jnp.minimum(i, n_p - 1), 0)),
            pl.BlockSpec((tm, d), lambda i: (jnp.minimum(i, n_p - 1), 1)),
            pl.BlockSpec((ms, d), lambda i: (0, 0)),
            pl.BlockSpec((ms, W_SGU), lambda i: (0, 0)),
            pl.BlockSpec((ms, W_ATTN), lambda i: (0, 0)),
            pl.BlockSpec((ms, d), lambda i: (0, 0)),
            pl.BlockSpec((ms, d), lambda i: (0, 1)),
            _const_spec((W_SGU, d)),
            _const_spec((W_ATTN, d)),
            _const_spec((d, d)),
            _const_spec((1, d)),
            _const_spec((d, 2 * LANES)),
            _const_spec((1, LANES)),
        ],
        out_specs=[pl.BlockSpec((tm, d), row), pl.BlockSpec((tm, d), row),
                   pl.BlockSpec((tm, LANES), row), pl.BlockSpec((tm, LANES), row)],
        out_shape=[
            jax.ShapeDtypeStruct((m, d), F32),
            jax.ShapeDtypeStruct((m, d), F32),
            jax.ShapeDtypeStruct((m, LANES), jnp.int32),
            jax.ShapeDtypeStruct((m, LANES), F32),
        ],
        compiler_params=_cparams(("arbitrary",), 58),
        name="merge",
    )(xp, ya_p, yb_p, gates_p, gates_p, xs, ya_s, yb_s, gates_s, gates_s,
      w_pa, w_pb, w_o, norm2_g.reshape(1, d), router_w_pad, router_b_pad)


def _row_copies(idx_ref, base, n, src_hbm, buf, slot, sem, start):
    for r in range(n):
        cp = pltpu.make_async_copy(src_hbm.at[pl.ds(idx_ref[base + r], 1)], buf.at[slot, pl.ds(r, 1)],
                                   sem.at[slot])
        if start:
            cp.start(priority=r % 2)
        else:
            cp.wait()


def _dispatch_body(valid_ref, first_ref, idx_ref, h_hbm, o_ref, buf, sem, *, sb, n_steps):
    i = pl.program_id(0)
    slot = _mod_pow2(i, 2)
    nxt = jnp.minimum(i + 1, n_steps - 1)

    @pl.when((i == 0) & (valid_ref[0] > 0))
    def _():
        _row_copies(idx_ref, first_ref[0], sb, h_hbm, buf, 0, sem, True)

    @pl.when((i + 1 < n_steps) & (valid_ref[nxt] > 0))
    def _():
        _row_copies(idx_ref, first_ref[nxt], sb, h_hbm, buf, 1 - slot, sem, True)

    @pl.when(valid_ref[i] > 0)
    def _():
        _row_copies(idx_ref, first_ref[i], sb, h_hbm, buf, slot, sem, False)
        o_ref[...] = buf[slot].astype(o_ref.dtype)

    @pl.when(valid_ref[i] == 0)
    def _():
        o_ref[...] = jnp.zeros(o_ref.shape, o_ref.dtype)


def moe_dispatch(h2, sub_valid, sub_first, sorted_tok, n_rows):
    sb = MOE_SB
    d = h2.shape[1]
    n_steps = n_rows // sb
    grid_spec = pltpu.PrefetchScalarGridSpec(
        num_scalar_prefetch=3,
        grid=(n_steps,),
        in_specs=[pl.BlockSpec(memory_space=pl.ANY)],
        out_specs=pl.BlockSpec((sb, d), lambda i, va, ci, idx: (i, 0)),
        scratch_shapes=[pltpu.VMEM((2, sb, d), F32), pltpu.SemaphoreType.DMA((2,))],
    )
    return pl.pallas_call(
        functools.partial(_dispatch_body, sb=sb, n_steps=n_steps),
        grid_spec=grid_spec,
        out_shape=jax.ShapeDtypeStruct((n_rows, d), BF16),
        compiler_params=_cparams(("arbitrary",), 32),
        name="moe_dispatch",
    )(sub_valid, sub_first, sorted_tok, h2)


def _combine_body(pos_ref, x1_ref, g_ref, fg_ref, y_hbm, o_ref, buf, sem, *, tt, n_steps):
    i = pl.program_id(0)
    n = tt * TOP_K
    slot = _mod_pow2(i, 2)

    @pl.when(i == 0)
    def _():
        _row_copies(pos_ref, 0, n, y_hbm, buf, 0, sem, True)

    @pl.when(i + 1 < n_steps)
    def _():
        _row_copies(pos_ref, (i + 1) * n, n, y_hbm, buf, 1 - slot, sem, True)

    _row_copies(pos_ref, i * n, n, y_hbm, buf, slot, sem, False)
    rows = buf[slot]
    g = g_ref[...]
    moe = g[:, 0:1] * rows[0:tt]
    for k in range(1, TOP_K):
        moe = moe + g[:, k:k + 1] * rows[k * tt:(k + 1) * tt]
    o_ref[...] = _rmsnorm_rows(x1_ref[...] + moe, fg_ref[...])


def moe_combine_final(x1, gates, pos, row0, m, y_rows, final_g):
    d = x1.shape[1]
    tt = min(COMBINE_TT, m)
    assert m % tt == 0 and row0 % tt == 0
    n_steps = m // tt
    blk0 = row0 // tt
    pos_km = pos[row0:row0 + m].reshape(n_steps, tt, TOP_K).transpose(0, 2, 1).reshape(-1)
    grid_spec = pltpu.PrefetchScalarGridSpec(
        num_scalar_prefetch=1,
        grid=(n_steps,),
        in_specs=[
            pl.BlockSpec((tt, d), lambda i, p: (blk0 + i, 0)),
            pl.BlockSpec((tt, LANES), lambda i, p: (blk0 + i, 0)),
            pl.BlockSpec((1, d), lambda i, p: (0, 0)),
            pl.BlockSpec(memory_space=pl.ANY),
        ],
        out_specs=pl.BlockSpec((tt, d), lambda i, p: (i, 0)),
        scratch_shapes=[pltpu.VMEM((2, tt * TOP_K, d), F32), pltpu.SemaphoreType.DMA((2,))],
    )
    return pl.pallas_call(
        functools.partial(_combine_body, tt=tt, n_steps=n_steps),
        grid_spec=grid_spec,
        out_shape=jax.ShapeDtypeStruct((m, d), F32),
        compiler_params=_cparams(("arbitrary",), 32),
        name="moe_combine_final",
    )(pos_km, x1, gates, final_g.reshape(1, d), y_rows)


def _moe_body(we_ref, wb_ref, wn_ref, x_ref, wg_ref, wl_ref, bg_ref, bl_ref, wd_ref, bd_ref, y_ref,
              act_s, *, unit, tf, nf):
    w = pl.program_id(0)
    s = pl.program_id(1)
    n_u = wn_ref[w]
    rb = y_ref.shape[0]

    def rows_at(off, n):
        return pl.ds(pl.multiple_of(off, unit), n)

    def for_each_chunk(fn):
        n4 = lax.shift_right_logical(n_u, 2)

        def big(i, carry):
            fn(i * (4 * unit), 4 * unit)
            return carry

        lax.fori_loop(0, n4, big, 0)
        off2 = n4 * (4 * unit)

        @pl.when(jnp.bitwise_and(n_u, 2) != 0)
        def _():
            fn(off2, 2 * unit)

        @pl.when(jnp.bitwise_and(n_u, 1) != 0)
        def _():
            fn(off2 + jnp.bitwise_and(n_u, 2) * unit, unit)

    @pl.when((s < nf) & (n_u > 0))
    def _():
        def chunk(off, n):
            rows = rows_at(off, n)
            xb = x_ref[rows, :]
            glu = jnp.dot(xb, wg_ref[...].astype(BF16), preferred_element_type=F32) + bg_ref[...]
            lin = jnp.dot(xb, wl_ref[...].astype(BF16), preferred_element_type=F32) + bl_ref[...]
            glu = jnp.minimum(glu, SWIGLU_LIMIT)
            lin = jnp.clip(lin, -SWIGLU_LIMIT, SWIGLU_LIMIT)
            act = glu * jax.nn.sigmoid(SWIGLU_ALPHA * glu) * (lin + 1.0)
            act_s[s, rows, :] = act.astype(BF16)

        for_each_chunk(chunk)

    @pl.when(s >= nf)
    def _():
        @pl.when(n_u > 0)
        def _():
            def chunk(off, n):
                rows = rows_at(off, n)
                a = jnp.concatenate([act_s[k, rows, :] for k in range(nf)], axis=1)
                y_ref[rows, :] = (jnp.dot(a, wd_ref[...].astype(BF16), preferred_element_type=F32)
                                  + bd_ref[...])

            for_each_chunk(chunk)

        def clear(i, carry):
            y_ref[rows_at(i * unit, unit), :] = jnp.zeros((unit, y_ref.shape[1]), F32)
            return carry

        lax.fori_loop(n_u, rb // unit, clear, 0)


def moe_experts(x_rows, work_e, work_blk, work_nsub, w_gu, b_gu, w_down, b_down):
    n_rows, d = x_rows.shape
    rb, tf, tn = MOE_RB, MOE_TF, MOE_TN
    nw = n_rows // rb
    nf = D_FF // tf
    nd = d // tn

    def down(s):
        return jnp.maximum(s - nf, 0)

    def up_w(w, s, wn):
        return jnp.where(wn[w] > 0, jnp.minimum(s, nf - 1), nf - 1)

    def down_w(w, s, wn):
        return jnp.where(wn[w] > 0, down(s), nd - 1)

    grid_spec = pltpu.PrefetchScalarGridSpec(
        num_scalar_prefetch=3,
        grid=(nw, nf + nd),
        in_specs=[
            pl.BlockSpec((rb, d), lambda w, s, we, wb, wn: (wb[w], 0)),
            pl.BlockSpec((None, d, tf), lambda w, s, we, wb, wn: (we[w], 0, up_w(w, s, wn))),
            pl.BlockSpec((None, d, tf), lambda w, s, we, wb, wn: (we[w], 0, nf + up_w(w, s, wn))),
            pl.BlockSpec((None, 1, tf), lambda w, s, we, wb, wn: (we[w], 0, up_w(w, s, wn))),
            pl.BlockSpec((None, 1, tf), lambda w, s, we, wb, wn: (we[w], 0, nf + up_w(w, s, wn))),
            pl.BlockSpec((None, D_FF, tn), lambda w, s, we, wb, wn: (we[w], 0, down_w(w, s, wn))),
            pl.BlockSpec((None, 1, tn), lambda w, s, we, wb, wn: (we[w], 0, down_w(w, s, wn))),
        ],
        out_specs=pl.BlockSpec((rb, tn), lambda w, s, we, wb, wn: (w, down(s))),
        scratch_shapes=[pltpu.VMEM((nf, rb, tf), BF16)],
    )
    return pl.pallas_call(
        functools.partial(_moe_body, unit=MOE_UNIT, tf=tf, nf=nf),
        grid_spec=grid_spec,
        out_shape=jax.ShapeDtypeStruct((n_rows, d), F32),
        compiler_params=_cparams(("arbitrary", "arbitrary"), 58),
        name="moe_experts",
    )(work_e, work_blk, work_nsub, x_rows, w_gu, w_gu, b_gu.reshape(N_EXPERTS, 1, 2 * D_FF),
      b_gu.reshape(N_EXPERTS, 1, 2 * D_FF), w_down, b_down.reshape(N_EXPERTS, 1, d))


def _routing(top_idx):
    n_tok = top_idx.shape[0]
    n_assign = n_tok * TOP_K
    rb, sb, unit = MOE_RB, MOE_SB, MOE_UNIT
    spw = rb // sb
    nw = n_assign // rb + N_EXPERTS
    flat_e = top_idx.reshape(-1).astype(jnp.int32)
    cb = 256
    assert n_assign % cb == 0
    onehot = (flat_e[:, None] == jnp.arange(N_EXPERTS, dtype=jnp.int32)[None, :]).astype(F32)
    tri = jnp.tril(jnp.ones((cb, cb), F32))
    within = jnp.einsum("ij,bjk->bik", tri, onehot.reshape(n_assign // cb, cb, N_EXPERTS))
    bsum = within[:, -1, :]
    csum = (within + (jnp.cumsum(bsum, axis=0) - bsum)[:, None, :]).reshape(n_assign, N_EXPERTS).astype(jnp.int32)
    rank = jnp.take_along_axis(csum, flat_e[:, None], axis=1)[:, 0] - 1
    counts = csum[-1]
    nwe = (counts + rb - 1) // rb
    w_end = jnp.cumsum(nwe)
    w_start = w_end - nwe
    dest = w_start[flat_e] * rb + rank
    widx = jnp.arange(nw, dtype=jnp.int32)
    n_used = w_end[-1]
    valid = widx < n_used
    we = jnp.minimum(jnp.searchsorted(w_end, widx, side="right"), N_EXPERTS - 1).astype(jnp.int32)
    rows_valid = jnp.clip(counts[we] - (widx - w_start[we]) * rb, 0, rb)
    n_units = jnp.where(valid, (rows_valid + unit - 1) // unit, 0).astype(jnp.int32)
    last = jnp.maximum(n_used - 1, 0)
    work_e = jnp.where(valid, we, we[last]).astype(jnp.int32)
    work_blk = jnp.where(valid, widx, last).astype(jnp.int32)
    jblk = jnp.arange(spw, dtype=jnp.int32)[None, :]
    sub_valid = (jblk * (sb // unit) < n_units[:, None]).astype(jnp.int32).reshape(-1)
    a_start = jnp.cumsum(counts) - counts
    first_w = a_start[we] + (widx - w_start[we]) * rb
    sub_first = jnp.clip(first_w[:, None] + jblk * sb, 0, n_assign).astype(jnp.int32).reshape(-1)
    sorted_a = jnp.argsort(flat_e, stable=True).astype(jnp.int32)
    sorted_tok = jnp.concatenate([sorted_a // TOP_K, jnp.zeros((sb,), jnp.int32)])
    return dest, work_e, work_blk, n_units, nw, sub_valid, sub_first, sorted_tok


def _layer_group(x2d, tm, w_in_bf, p, act_dtype):
    h = rmsnorm_to(x2d, p["norm1_g"], tm, BF16)
    (u,) = in_proj(h, w_in_bf, 0, 1, "gelu", tm, [act_dtype])
    (v,) = in_proj(h, w_in_bf, 1, 1, "gelu_ln", tm, [F32],
                   extra=(p["sgu_ln_g"].reshape(1, -1), p["sgu_ln_b"].reshape(1, -1)))
    (q,) = in_proj(h, w_in_bf, 2, 1, "scale", tm, [act_dtype])
    k32, kbf = in_proj(h, w_in_bf, 3, 1, "dual", tm, [F32, BF16])
    va32, vabf = in_proj(h, w_in_bf, 4, 1, "dual", tm, [F32, BF16])
    (gates,) = in_proj(h, w_in_bf, 5, 4, "sigmoid", tm, [BF16])
    return u, v, q, k32, kbf, va32, vabf, gates


def kernel(x_prompt, x_sample, cache_k, cache_v, page_table, norm1_g, w_in, sgu_ln_g, sgu_ln_b, sgu_w, sgu_b,
           lambda_q1, lambda_k1, lambda_q2, lambda_k2, subln_g, w_branch_a, w_branch_b, w_out, norm2_g,
           router_w, router_b, expert_w_gu, expert_b_gu, expert_w_down, expert_b_down, final_norm_g):
    depth = w_in.shape[0]
    assert depth == 1
    l = 0
    bp, sp, d = x_prompt.shape
    bs, ts, _ = x_sample.shape
    lam_init = 0.8 - 0.6 * math.exp(-0.3 * l)
    slopes2 = jnp.exp2(-(8.0 / N_HEADS) * jnp.arange(1, N_HEADS + 1, dtype=F32)) * LOG2E
    lam_vecs = jnp.stack([lambda_q1[l], lambda_k1[l], lambda_q2[l], lambda_k2[l]]).astype(F32)

    w_in_bf = w_in[l]
    w_pa = w_branch_a[l].astype(BF16)
    w_pb = w_branch_b[l].astype(BF16)
    w_o = w_out[l].astype(BF16)
    rw32 = jnp.pad(router_w[l].astype(F32), ((0, 0), (0, LANES - N_EXPERTS)))
    rw_hi = rw32.astype(BF16)
    rw_pad = jnp.concatenate([rw_hi, (rw32 - rw_hi.astype(F32)).astype(BF16)], axis=1)
    rb_pad = jnp.pad(router_b[l].astype(F32), (0, LANES - N_EXPERTS)).reshape(1, LANES)
    p = {"norm1_g": norm1_g[l], "sgu_ln_g": sgu_ln_g[l], "sgu_ln_b": sgu_ln_b[l]}

    xp = x_prompt.reshape(bp * sp, d)
    u, v, q, k32p, kbf, va32p, vabf, gates_p = _layer_group(xp, PROJ_TM, w_in_bf, p, BF16)
    ya_p = sgu(u, v, sgu_w[l], sgu_b[l], CHUNK, 4, BF16)
    yb_p = prompt_attention(q, kbf, vabf, lam_vecs, subln_g[l], slopes2, bp, sp, lam_init)

    xs = x_sample.reshape(bs * ts, d)
    u, v_s, q, k32s, _, va32s, _, gates_s = _layer_group(xs, bs * ts, w_in_bf, p, F32)
    ya_s = sgu(u, v_s, sgu_w[l], sgu_b[l], ts, bs, F32)
    k_sample = k32s.reshape(bs, ts, N_HEADS, 2 * DK)
    v_sample = va32s.reshape(bs, ts, N_HEADS, DV)
    yb_s = decode_attention(q, k_sample, v_sample, cache_k, cache_v, l, page_table,
                            lam_vecs, subln_g[l], slopes2, lam_init)

    x1, h2, top_idx, gate_w = merge(xp, ya_p, yb_p, gates_p, xs, ya_s, yb_s, gates_s, w_pa, w_pb, w_o, norm2_g[l],
                                    rw_pad, rb_pad, MERGE_TM)
    dest, work_e, work_blk, work_units, nw, sub_valid, sub_first, sorted_tok = _routing(top_idx[:, :TOP_K])
    x_rows = moe_dispatch(h2, sub_valid, sub_first, sorted_tok, nw * MOE_RB)
    y_rows = moe_experts(x_rows, work_e, work_blk, work_units, expert_w_gu[l], expert_b_gu[l],
                         expert_w_down[l], expert_b_down[l])
    n_p = bp * sp
    pos = dest.reshape(-1, TOP_K)
    y_prompt = moe_combine_final(x1, gate_w, pos, 0, n_p, y_rows, final_norm_g).reshape(bp, sp, d)
    y_sample = moe_combine_final(x1, gate_w, pos, n_p, bs * ts, y_rows, final_norm_g).reshape(bs, ts, d)

    k_prompt = k32p.reshape(1, bp, sp, N_HEADS, 2 * DK)
    v_prompt = va32p.reshape(1, bp, sp, N_HEADS, DV)
    state_sgu_v = v_s.reshape(1, bs, ts, W_SGU)
    return (y_prompt, y_sample, k_prompt, v_prompt, k_sample[None], v_sample[None], state_sgu_v)
```

```python
import functools
import math

import jax
import jax.numpy as jnp
from jax import lax
from jax.experimental import pallas as pl
from jax.experimental.pallas import tpu as pltpu

F32 = jnp.float32
BF16 = jnp.bfloat16

D_MODEL = 2048
N_HEADS = 8
DK = 64
DV = 2 * DK
W_ATTN = N_HEADS * DV
D_QK = 2 * N_HEADS * DK
W_SGU = D_MODEL // 2
N_SGU_GROUPS = 8
SGU_GROUP = W_SGU // N_SGU_GROUPS
CHUNK = 128
N_EXPERTS = 32
TOP_K = 4
D_FF = D_MODEL
SWIGLU_LIMIT = 7.0
SWIGLU_ALPHA = 1.702
EPS = 1e-6
PAGE_SIZE = 128
D_IN = 2 * W_SGU + 2 * D_QK + W_ATTN + 2 * D_MODEL
LOG2E = math.log2(math.e)

LANES = 128
SUBLANES = 8
MIB = 1024 * 1024

PROJ_TM = 1024
PROJ_TN = 1024
PROJ_RC = 512
ATT_TQ = 512
ATT_TK = 512
ATT_RB = 256
DEC_PP = 16
MERGE_TM = 256
MOE_RB = 1280
MOE_SB = 640
MOE_UNIT = 128
MOE_TF = 512
MOE_TN = 512
COMBINE_TT = 128
NEG_BIG = -1e30


def _cparams(semantics, vmem_mib):
    return pltpu.CompilerParams(dimension_semantics=semantics, vmem_limit_bytes=vmem_mib * MIB)


def _const_spec(shape):
    nd = len(shape)
    return pl.BlockSpec(shape, lambda *_: (0,) * nd, pipeline_mode=pl.Buffered(1))


def _div_pow2(x, n):
    assert n & (n - 1) == 0
    return lax.shift_right_logical(x, n.bit_length() - 1)


def _mod_pow2(x, n):
    assert n & (n - 1) == 0
    return jnp.bitwise_and(x, n - 1)


def _rmsnorm_rows(x, g):
    ms = jnp.mean(x * x, axis=-1, keepdims=True)
    return x * lax.rsqrt(ms + EPS) * g


def _rmsnorm_body(x_ref, g_ref, o_ref):
    o_ref[...] = _rmsnorm_rows(x_ref[...], g_ref[...]).astype(o_ref.dtype)


def rmsnorm_to(x2d, g, tm, dtype):
    m, d = x2d.shape
    return pl.pallas_call(
        _rmsnorm_body,
        grid=(m // tm,),
        in_specs=[pl.BlockSpec((tm, d), lambda i: (i, 0)), pl.BlockSpec((1, d), lambda i: (0, 0))],
        out_specs=pl.BlockSpec((tm, d), lambda i: (i, 0)),
        out_shape=jax.ShapeDtypeStruct((m, d), dtype),
        compiler_params=_cparams(("parallel",), 48),
        name="rmsnorm",
    )(x2d, g.reshape(1, d))


def _gelu(z):
    return 0.5 * z * (1.0 + lax.erf(z * (2.0 ** -0.5)))


def _proj_body(h_ref, w_ref, *refs, kind, rc):
    tm = h_ref.shape[0]
    *refs, w_bf = refs

    @pl.when(pl.program_id(1) == 0)
    def _():
        w_bf[...] = w_ref[...].astype(BF16)

    def chunk(c, carry):
        r = pl.multiple_of(c * rc, rc)
        rows = pl.ds(r, rc)
        z = jnp.dot(h_ref[rows, :], w_bf[...], preferred_element_type=F32)
        if kind == "gelu":
            refs[0][rows, :] = _gelu(z).astype(refs[0].dtype)
        elif kind == "gelu_ln":
            g_ref, b_ref, o_ref = refs
            a = _gelu(z)
            mu = jnp.mean(a, axis=-1, keepdims=True)
            var = jnp.mean(jnp.square(a - mu), axis=-1, keepdims=True)
            y = (a - mu) * lax.rsqrt(var + EPS)
            o_ref[rows, :] = (y * g_ref[...] + b_ref[...]).astype(o_ref.dtype)
        elif kind == "scale":
            refs[0][rows, :] = (z * (DK ** -0.5 * LOG2E)).astype(refs[0].dtype)
        elif kind == "dual":
            refs[0][rows, :] = z
            refs[1][rows, :] = z.astype(refs[1].dtype)
        elif kind == "sigmoid":
            refs[0][rows, :] = jax.nn.sigmoid(z).astype(refs[0].dtype)
        return carry

    lax.fori_loop(0, tm // rc, chunk, 0)


def in_proj(h, w, col_blk, n_blk, kind, tm, out_dtypes, extra=()):
    m, d = h.shape
    tn = PROJ_TN
    rc = min(PROJ_RC, tm)
    in_specs = [
        pl.BlockSpec((tm, d), lambda n, i: (i, 0)),
        pl.BlockSpec((d, tn), lambda n, i: (0, col_blk + n)),
    ] + [pl.BlockSpec((1, tn), lambda n, i: (0, 0)) for _ in extra]
    out_specs = [pl.BlockSpec((tm, tn), lambda n, i: (i, n)) for _ in out_dtypes]
    out_shape = [jax.ShapeDtypeStruct((m, n_blk * tn), dt) for dt in out_dtypes]
    res = pl.pallas_call(
        functools.partial(_proj_body, kind=kind, rc=rc),
        grid=(n_blk, m // tm),
        in_specs=in_specs,
        out_specs=out_specs,
        out_shape=out_shape,
        scratch_shapes=[pltpu.VMEM((d, tn), BF16)],
        compiler_params=_cparams(("parallel", "arbitrary"), 56),
        name="in_proj_" + kind,
    )(h, w, *extra)
    return res


def _sgu_body(u_ref, v_ref, w_ref, bt_ref, o_ref, *, rows, n_chunks):
    ii = lax.broadcasted_iota(jnp.int32, (CHUNK, CHUNK), 0)
    jj = lax.broadcasted_iota(jnp.int32, (CHUNK, CHUNK), 1)
    causal = jj <= ii
    for c in range(n_chunks):
        r0 = c * rows
        vb = v_ref[r0:r0 + rows, :]
        if rows < CHUNK:
            vb = jnp.concatenate([vb, jnp.zeros((CHUNK - rows, vb.shape[1]), vb.dtype)], axis=0)
        vb = vb.astype(BF16)
        for g in range(N_SGU_GROUPS):
            w = jnp.where(causal, w_ref[g], 0.0).astype(BF16)
            cols = slice(g * SGU_GROUP, (g + 1) * SGU_GROUP)
            f = jnp.dot(w, vb[:, cols], preferred_element_type=F32) + bt_ref[:, g:g + 1]
            u = u_ref[r0:r0 + rows, cols].astype(F32)
            o_ref[r0:r0 + rows, cols] = (u * f[:rows]).astype(o_ref.dtype)


def sgu(u, v, w_s, b_s, rows, n_chunks, out_dtype):
    m, w = u.shape
    tm = rows * n_chunks
    return pl.pallas_call(
        functools.partial(_sgu_body, rows=rows, n_chunks=n_chunks),
        grid=(m // tm,),
        in_specs=[
            pl.BlockSpec((tm, w), lambda i: (i, 0)),
            pl.BlockSpec((tm, w), lambda i: (i, 0)),
            pl.BlockSpec((N_SGU_GROUPS, CHUNK, CHUNK), lambda i: (0, 0, 0)),
            pl.BlockSpec((CHUNK, N_SGU_GROUPS), lambda i: (0, 0)),
        ],
        out_specs=pl.BlockSpec((tm, w), lambda i: (i, 0)),
        out_shape=jax.ShapeDtypeStruct((m, w), out_dtype),
        compiler_params=_cparams(("parallel",), 32),
        name="sgu",
    )(u, v, w_s, b_s.T)


def _lambda_value(lam_ref, lam_init):
    lv = lam_ref[...]
    d1 = jnp.sum(lv[0:1] * lv[1:2], axis=-1, keepdims=True)
    d2 = jnp.sum(lv[2:3] * lv[3:4], axis=-1, keepdims=True)
    return jnp.exp(d1) - jnp.exp(d2) + lam_init


def _head_out(o0, o1, lam, g, lam_init):
    o = o0 - lam * o1
    return _rmsnorm_rows(o, g) * (1.0 - lam_init)


def _flash_body(q_ref, k_ref, v_ref, lam_ref, g_ref, slope_ref, o_ref, vaug, m_s, acc, *, tq, tk, rb, lam_init):
    qi = pl.program_id(2)
    n_rb = tq // rb
    chains = [(c, r) for c in range(2) for r in range(n_rb)]

    @pl.when(qi == 0)
    def _():
        vaug[:, 0:DV] = v_ref[...]
        vaug[:, DV:2 * DV] = jnp.ones((vaug.shape[0], DV), BF16)

    q = q_ref[...]
    lane = lax.broadcasted_iota(jnp.int32, q.shape, 1)
    zero = jnp.zeros_like(q)
    qmap = [jnp.where(lane < DK, q, zero), jnp.where(lane >= DK, q, zero)]
    slope = slope_ref[...]
    col = lax.broadcasted_iota(jnp.int32, (1, tk), 1)
    q0 = qi * tq

    m_s[...] = jnp.full(m_s.shape, NEG_BIG, F32)
    acc[...] = jnp.zeros(acc.shape, F32)

    def step(j, masked):
        k0 = pl.multiple_of(j * tk, tk)
        kj = k_ref[pl.ds(k0, tk), :]
        vj = vaug[pl.ds(k0, tk), :]
        bias = slope * (col + (k0 - q0)).astype(F32)
        for ci, (c, r) in enumerate(chains):
            rows = slice(ci * rb, (ci + 1) * rb)
            qc = qmap[c][r * rb:(r + 1) * rb]
            nk = (r + 1) * rb if masked else tk
            s = lax.dot_general(qc, kj[0:nk], (((1,), (1,)), ((), ())), preferred_element_type=F32)
            s = s + bias[:, 0:nk]
            if masked:
                rr = lax.broadcasted_iota(jnp.int32, s.shape, 0) + (r * rb)
                cc = lax.broadcasted_iota(jnp.int32, s.shape, 1)
                s = jnp.where(cc <= rr, s, NEG_BIG)
            m_old = m_s[rows, :]
            m_new = jnp.maximum(m_old, jnp.max(s, axis=-1, keepdims=True))
            alpha = jnp.exp2(m_old - m_new)
            p = jnp.exp2(s - jnp.tile(m_new, (1, nk // LANES)))
            pv = jnp.dot(p.astype(BF16), vj[0:nk], preferred_element_type=F32)
            acc[rows, :] = jnp.tile(alpha, (1, 2 * DV // LANES)) * acc[rows, :] + pv
            m_s[rows, :] = m_new

    def four_full_steps(i, carry):
        for u in range(4):
            step(4 * i + u, False)
        return carry

    n4 = lax.shift_right_logical(qi, 2)
    lax.fori_loop(0, n4, four_full_steps, 0)
    j0 = n4 * 4
    has2 = jnp.bitwise_and(qi, 2) != 0
    has1 = jnp.bitwise_and(qi, 1) != 0

    @pl.when(has2)
    def _():
        step(j0, False)
        step(j0 + 1, False)

    @pl.when(has1)
    def _():
        step(qi - 1, False)
        step(qi, True)

    @pl.when(jnp.logical_not(has1))
    def _():
        step(qi, True)

    lam = _lambda_value(lam_ref, lam_init)
    for r in range(n_rb):
        a0 = acc[r * rb:(r + 1) * rb, :]
        a1 = acc[(n_rb + r) * rb:(n_rb + r + 1) * rb, :]
        o0 = a0[:, 0:DV] / a0[:, DV:2 * DV]
        o1 = a1[:, 0:DV] / a1[:, DV:2 * DV]
        o_ref[r * rb:(r + 1) * rb, :] = _head_out(o0, o1, lam, g_ref[...], lam_init).astype(o_ref.dtype)


def prompt_attention(q, k, v, lam_vecs, subln_g, slopes2, batch, seq, lam_init):
    tq, tk, rb = ATT_TQ, ATT_TK, ATT_RB
    assert tq == tk and seq % tq == 0 and tq % rb == 0
    nq = seq // tq
    n_chain_rows = 2 * tq
    slope_b = jnp.broadcast_to(slopes2[:, None, None], (N_HEADS, 1, tk)).astype(F32)
    return pl.pallas_call(
        functools.partial(_flash_body, tq=tq, tk=tk, rb=rb, lam_init=lam_init),
        grid=(batch, N_HEADS, nq),
        in_specs=[
            pl.BlockSpec((tq, DV), lambda b, h, i: (b * nq + i, h)),
            pl.BlockSpec((seq, DV), lambda b, h, i: (b, h)),
            pl.BlockSpec((seq, DV), lambda b, h, i: (b, h)),
            pl.BlockSpec((4, DK), lambda b, h, i: (0, 0)),
            pl.BlockSpec((1, DV), lambda b, h, i: (0, 0)),
            pl.BlockSpec((None, 1, tk), lambda b, h, i: (h, 0, 0)),
        ],
        out_specs=pl.BlockSpec((tq, DV), lambda b, h, i: (b * nq + i, h)),
        out_shape=jax.ShapeDtypeStruct((batch * seq, W_ATTN), BF16),
        scratch_shapes=[
            pltpu.VMEM((seq, 2 * DV), BF16),
            pltpu.VMEM((n_chain_rows, LANES), F32),
            pltpu.VMEM((n_chain_rows, 2 * DV), F32),
        ],
        compiler_params=_cparams(("parallel", "parallel", "arbitrary"), 40),
        name="prompt_attention",
    )(q, k, v, lam_vecs, subln_g.reshape(1, DV), slope_b)


def _decode_body(pt_ref, q_ref, kn_ref, vn_ref, *rest, pp, n_steps, t_new, past, lam_init):
    kp = rest[:pp]
    vp = rest[pp:2 * pp]
    lam_ref, g_ref, slope_ref, o_ref, qx, bias0, m_s, l_s, acc = rest[2 * pp:]
    j = pl.program_id(1)
    nt = (((1,), (1,)), ((), ()))
    nrow = 2 * N_HEADS * t_new
    ncol = PAGE_SIZE * N_HEADS
    slope = slope_ref[...]
    rep = ncol // LANES

    @pl.when(j == 0)
    def _():
        q = q_ref[...]
        lane = lax.broadcasted_iota(jnp.int32, (t_new, 2 * DK), 1)
        parts = []
        for h in range(N_HEADS):
            qh = q[:, h * 2 * DK:(h + 1) * 2 * DK]
            parts.append(jnp.where(lane < DK, qh, 0.0))
            parts.append(jnp.where(lane >= DK, qh, 0.0))
        qx[...] = jnp.concatenate(parts, axis=0).astype(BF16)
        rr = lax.broadcasted_iota(jnp.int32, (nrow, ncol), 0)
        cc = lax.broadcasted_iota(jnp.int32, (nrow, ncol), 1)
        head_ok = _mod_pow2(cc, N_HEADS) == _div_pow2(rr, 2 * t_new)
        pos = _div_pow2(cc, N_HEADS).astype(F32)
        bias0[...] = jnp.where(head_ok, jnp.tile(slope, (1, rep)) * pos, NEG_BIG)
        n_new = t_new * N_HEADS
        pad = jnp.zeros((LANES - n_new, 2 * DK), F32)
        kn = jnp.concatenate([kn_ref[...].reshape(n_new, 2 * DK), pad], axis=0).astype(BF16)
        vn = jnp.concatenate([vn_ref[...].reshape(n_new, DV), pad], axis=0).astype(BF16)
        s = lax.dot_general(qx[...], kn, nt, preferred_element_type=F32)
        r1 = lax.broadcasted_iota(jnp.int32, s.shape, 0)
        c1 = lax.broadcasted_iota(jnp.int32, s.shape, 1)
        tk = _div_pow2(c1, N_HEADS)
        ok = (c1 < n_new) & (_mod_pow2(c1, N_HEADS) == _div_pow2(r1, 2 * t_new)) & (tk <= _mod_pow2(r1, t_new))
        s = jnp.where(ok, s + slope * tk.astype(F32), NEG_BIG)
        m = jnp.max(s, axis=-1, keepdims=True)
        p = jnp.exp2(s - m)
        m_s[...] = jnp.broadcast_to(m, m_s.shape)
        l_s[...] = jnp.broadcast_to(jnp.sum(p, axis=-1, keepdims=True), l_s.shape)
        acc[...] = jnp.dot(p.astype(BF16), vn, preferred_element_type=F32)

    q2 = qx[...]
    b0 = bias0[...]
    ts, offs = [], []
    m_old = m_s[...]
    m_new = m_old
    for i in range(pp):
        kpage = kp[i][...].reshape(ncol, 2 * DK).astype(BF16)
        t_i = lax.dot_general(q2, kpage, nt, preferred_element_type=F32) + b0
        off_i = slope * ((j * pp + i) * PAGE_SIZE - past).astype(F32)
        m_new = jnp.maximum(m_new, jnp.max(t_i, axis=-1, keepdims=True) + off_i)
        ts.append(t_i)
        offs.append(off_i)
    alpha = jnp.exp2(m_old - m_new)
    lsum = None
    pv = None
    for i in range(pp):
        p_i = jnp.exp2(ts[i] - jnp.tile(m_new - offs[i], (1, rep)))
        vpage = vp[i][...].reshape(ncol, DV).astype(BF16)
        d = jnp.dot(p_i.astype(BF16), vpage, preferred_element_type=F32)
        r = jnp.sum(p_i, axis=-1, keepdims=True)
        pv = d if pv is None else pv + d
        lsum = r if lsum is None else lsum + r
    l_s[...] = alpha * l_s[...] + lsum
    acc[...] = alpha * acc[...] + pv
    m_s[...] = m_new

    @pl.when(j == n_steps - 1)
    def _():
        lam = _lambda_value(lam_ref, lam_init)
        on = acc[...] / l_s[...]
        for h in range(N_HEADS):
            r0 = h * 2 * t_new
            o = _head_out(on[r0:r0 + t_new], on[r0 + t_new:r0 + 2 * t_new], lam, g_ref[...], lam_init)
            o_ref[:, h * DV:(h + 1) * DV] = o.astype(o_ref.dtype)


def decode_attention(q, k_new, v_new, cache_k, cache_v, layer, page_table, lam_vecs, subln_g, slopes2, lam_init):
    bd, n_pages = page_table.shape
    t_new = q.shape[0] // bd
    past = n_pages * PAGE_SIZE
    pp = DEC_PP
    n_steps = n_pages // pp
    nrow = 2 * N_HEADS * t_new
    ncol = PAGE_SIZE * N_HEADS
    assert nrow == LANES and t_new * N_HEADS <= LANES and n_pages % pp == 0
    width = N_HEADS * DV
    slope_rows = jnp.broadcast_to(jnp.repeat(slopes2, 2 * t_new)[:, None], (nrow, LANES)).astype(F32)

    def page_spec(i):
        return pl.BlockSpec((None, None, PAGE_SIZE, N_HEADS, DV),
                            lambda b, j, pt: (layer, pt[b * n_pages + j * pp + i], 0, 0, 0))

    new_spec = pl.BlockSpec((None, t_new, N_HEADS, DV), lambda b, j, pt: (b, 0, 0, 0))
    grid_spec = pltpu.PrefetchScalarGridSpec(
        num_scalar_prefetch=1,
        grid=(bd, n_steps),
        in_specs=[pl.BlockSpec((t_new, width), lambda b, j, pt: (b, 0)), new_spec, new_spec]
        + [page_spec(i) for i in range(pp)] + [page_spec(i) for i in range(pp)] + [
            pl.BlockSpec((4, DK), lambda b, j, pt: (0, 0)),
            pl.BlockSpec((1, DV), lambda b, j, pt: (0, 0)),
            pl.BlockSpec((nrow, LANES), lambda b, j, pt: (0, 0)),
        ],
        out_specs=pl.BlockSpec((t_new, width), lambda b, j, pt: (b, 0)),
        scratch_shapes=[
            pltpu.VMEM((nrow, 2 * DK), BF16),
            pltpu.VMEM((nrow, ncol), F32),
            pltpu.VMEM((nrow, LANES), F32),
            pltpu.VMEM((nrow, LANES), F32),
            pltpu.VMEM((nrow, DV), F32),
        ],
    )
    return pl.pallas_call(
        functools.partial(_decode_body, pp=pp, n_steps=n_steps, t_new=t_new, past=past, lam_init=lam_init),
        grid_spec=grid_spec,
        out_shape=jax.ShapeDtypeStruct((bd * t_new, width), F32),
        compiler_params=_cparams(("parallel", "arbitrary"), 56),
        name="decode_attention",
    )(page_table.reshape(-1), q, k_new, v_new, *([cache_k] * pp), *([cache_v] * pp),
      lam_vecs, subln_g.reshape(1, DV), slope_rows)


def _merge_rows(x, ya, yb, ga, gb, wpa_ref, wpb_ref, wo_ref, g2_ref, rw_ref, rb_ref):
    a = jnp.dot(ya.astype(BF16), wpa_ref[...], preferred_element_type=F32)
    b = jnp.dot(yb.astype(BF16), wpb_ref[...], preferred_element_type=F32)
    mixed = ga.astype(F32) * a + gb.astype(F32) * b
    x1 = x + jnp.dot(mixed.astype(BF16), wo_ref[...], preferred_element_type=F32)
    h2 = _rmsnorm_rows(x1, g2_ref[...])
    h_hi = h2.astype(BF16)
    h_lo = (h2 - h_hi.astype(F32)).astype(BF16)
    hw = jnp.dot(h_hi, rw_ref[...], preferred_element_type=F32)
    lw = jnp.dot(h_lo, rw_ref[:, 0:LANES], preferred_element_type=F32)
    lg = hw[:, 0:LANES] + (hw[:, LANES:2 * LANES] + lw) + rb_ref[...]
    lane = lax.broadcasted_iota(jnp.int32, lg.shape, 1)
    lane_f = lane.astype(F32)
    cur = jnp.where(lane < N_EXPERTS, lg, -jnp.inf)
    idx_out = jnp.zeros(lg.shape, F32)
    vals = []
    for k in range(TOP_K):
        mx = jnp.max(cur, axis=-1, keepdims=True)
        ix = jnp.min(jnp.where(cur == mx, lane_f, float(LANES)), axis=-1, keepdims=True)
        idx_out = jnp.where(lane == k, ix, idx_out)
        vals.append(mx)
        cur = jnp.where(lane_f == ix, -jnp.inf, cur)
    ex = [jnp.exp(v - vals[0]) for v in vals]
    tot = ex[0]
    for k in range(1, TOP_K):
        tot = tot + ex[k]
    gate_out = jnp.zeros(lg.shape, F32)
    for k in range(TOP_K):
        gate_out = jnp.where(lane == k, ex[k] / tot, gate_out)
    return x1, h2, idx_out.astype(jnp.int32), gate_out


def _merge_body(xp_ref, yap_ref, ybp_ref, gap_ref, gbp_ref, xs_ref, yas_ref, ybs_ref, gas_ref, gbs_ref,
                wpa_ref, wpb_ref, wo_ref, g2_ref, rw_ref, rb_ref, x1_ref, h2_ref, ti_ref, tg_ref, *, n_p):
    i = pl.program_id(0)
    weights = (wpa_ref, wpb_ref, wo_ref, g2_ref, rw_ref, rb_ref)
    outs = (x1_ref, h2_ref, ti_ref, tg_ref)

    @pl.when(i < n_p)
    def _():
        res = _merge_rows(xp_ref[...], yap_ref[...], ybp_ref[...], gap_ref[...], gbp_ref[...], *weights)
        for o_ref, r in zip(outs, res):
            o_ref[...] = r

    @pl.when(i == n_p)
    def _():
        ms = xs_ref.shape[0]
        res = _merge_rows(xs_ref[...], yas_ref[...], ybs_ref[...], gas_ref[...], gbs_ref[...], *weights)
        for o_ref, r in zip(outs, res):
            o_ref[0:ms, :] = r


def merge(xp, ya_p, yb_p, gates_p, xs, ya_s, yb_s, gates_s, w_pa, w_pb, w_o, norm2_g, router_w_pad, router_b_pad,
          tm):
    mp, d = xp.shape
    ms = xs.shape[0]
    assert mp % tm == 0 and ms <= tm
    n_p = mp // tm
    m = mp + ms
    prow = lambda i: (jnp.minimum(i, n_p - 1), 0)
    row = lambda i: (i, 0)
    return pl.pallas_call(
        functools.partial(_merge_body, n_p=n_p),
        grid=(n_p + 1,),
        in_specs=[
            pl.BlockSpec((tm, d), prow),
            pl.BlockSpec((tm, W_SGU), prow),
            pl.BlockSpec((tm, W_ATTN), prow),
            pl.BlockSpec((tm, d), lambda i: (jnp.minimum(i, n_p - 1), 0)),
            pl.BlockSpec((tm, d), lambda i: (jnp.minimum(i, n_p - 1), 1)),
            pl.BlockSpec((ms, d), lambda i: (0, 0)),
            pl.BlockSpec((ms, W_SGU), lambda i: (0, 0)),
            pl.BlockSpec((ms, W_ATTN), lambda i: (0, 0)),
            pl.BlockSpec((ms, d), lambda i: (0, 0)),
            pl.BlockSpec((ms, d), lambda i: (0, 1)),
            _const_spec((W_SGU, d)),
            _const_spec((W_ATTN, d)),
            _const_spec((d, d)),
            _const_spec((1, d)),
            _const_spec((d, 2 * LANES)),
            _const_spec((1, LANES)),
        ],
        out_specs=[pl.BlockSpec((tm, d), row), pl.BlockSpec((tm, d), row),
                   pl.BlockSpec((tm, LANES), row), pl.BlockSpec((tm, LANES), row)],
        out_shape=[
            jax.ShapeDtypeStruct((m, d), F32),
            jax.ShapeDtypeStruct((m, d), F32),
            jax.ShapeDtypeStruct((m, LANES), jnp.int32),
            jax.ShapeDtypeStruct((m, LANES), F32),
        ],
        compiler_params=_cparams(("arbitrary",), 58),
        name="merge",
    )(xp, ya_p, yb_p, gates_p, gates_p, xs, ya_s, yb_s, gates_s, gates_s,
      w_pa, w_pb, w_o, norm2_g.reshape(1, d), router_w_pad, router_b_pad)


def _row_copies(idx_ref, base, n, src_hbm, buf, slot, sem, start):
    for r in range(n):
        cp = pltpu.make_async_copy(src_hbm.at[pl.ds(idx_ref[base + r], 1)], buf.at[slot, pl.ds(r, 1)],
                                   sem.at[slot])
        if start:
            cp.start(priority=r % 2)
        else:
            cp.wait()


def _dispatch_body(valid_ref, first_ref, idx_ref, h_hbm, o_ref, buf, sem, *, sb, n_steps):
    i = pl.program_id(0)
    slot = _mod_pow2(i, 2)
    nxt = jnp.minimum(i + 1, n_steps - 1)

    @pl.when((i == 0) & (valid_ref[0] > 0))
    def _():
        _row_copies(idx_ref, first_ref[0], sb, h_hbm, buf, 0, sem, True)

    @pl.when((i + 1 < n_steps) & (valid_ref[nxt] > 0))
    def _():
        _row_copies(idx_ref, first_ref[nxt], sb, h_hbm, buf, 1 - slot, sem, True)

    @pl.when(valid_ref[i] > 0)
    def _():
        _row_copies(idx_ref, first_ref[i], sb, h_hbm, buf, slot, sem, False)
        o_ref[...] = buf[slot].astype(o_ref.dtype)

    @pl.when(valid_ref[i] == 0)
    def _():
        o_ref[...] = jnp.zeros(o_ref.shape, o_ref.dtype)


def moe_dispatch(h2, sub_valid, sub_first, sorted_tok, n_rows):
    sb = MOE_SB
    d = h2.shape[1]
    n_steps = n_rows // sb
    grid_spec = pltpu.PrefetchScalarGridSpec(
        num_scalar_prefetch=3,
        grid=(n_steps,),
        in_specs=[pl.BlockSpec(memory_space=pl.ANY)],
        out_specs=pl.BlockSpec((sb, d), lambda i, va, ci, idx: (i, 0)),
        scratch_shapes=[pltpu.VMEM((2, sb, d), F32), pltpu.SemaphoreType.DMA((2,))],
    )
    return pl.pallas_call(
        functools.partial(_dispatch_body, sb=sb, n_steps=n_steps),
        grid_spec=grid_spec,
        out_shape=jax.ShapeDtypeStruct((n_rows, d), BF16),
        compiler_params=_cparams(("arbitrary",), 32),
        name="moe_dispatch",
    )(sub_valid, sub_first, sorted_tok, h2)


def _combine_body(pos_ref, x1_ref, g_ref, fg_ref, y_hbm, o_ref, buf, sem, *, tt, n_steps):
    i = pl.program_id(0)
    n = tt * TOP_K
    slot = _mod_pow2(i, 2)

    @pl.when(i == 0)
    def _():
        _row_copies(pos_ref, 0, n, y_hbm, buf, 0, sem, True)

    @pl.when(i + 1 < n_steps)
    def _():
        _row_copies(pos_ref, (i + 1) * n, n, y_hbm, buf, 1 - slot, sem, True)

    _row_copies(pos_ref, i * n, n, y_hbm, buf, slot, sem, False)
    rows = buf[slot]
    g = g_ref[...]
    moe = g[:, 0:1] * rows[0:tt]
    for k in range(1, TOP_K):
        moe = moe + g[:, k:k + 1] * rows[k * tt:(k + 1) * tt]
    o_ref[...] = _rmsnorm_rows(x1_ref[...] + moe, fg_ref[...])


def moe_combine_final(x1, gates, pos, row0, m, y_rows, final_g):
    d = x1.shape[1]
    tt = min(COMBINE_TT, m)
    assert m % tt == 0 and row0 % tt == 0
    n_steps = m // tt
    blk0 = row0 // tt
    pos_km = pos[row0:row0 + m].reshape(n_steps, tt, TOP_K).transpose(0, 2, 1).reshape(-1)
    grid_spec = pltpu.PrefetchScalarGridSpec(
        num_scalar_prefetch=1,
        grid=(n_steps,),
        in_specs=[
            pl.BlockSpec((tt, d), lambda i, p: (blk0 + i, 0)),
            pl.BlockSpec((tt, LANES), lambda i, p: (blk0 + i, 0)),
            pl.BlockSpec((1, d), lambda i, p: (0, 0)),
            pl.BlockSpec(memory_space=pl.ANY),
        ],
        out_specs=pl.BlockSpec((tt, d), lambda i, p: (i, 0)),
        scratch_shapes=[pltpu.VMEM((2, tt * TOP_K, d), F32), pltpu.SemaphoreType.DMA((2,))],
    )
    return pl.pallas_call(
        functools.partial(_combine_body, tt=tt, n_steps=n_steps),
        grid_spec=grid_spec,
        out_shape=jax.ShapeDtypeStruct((m, d), F32),
        compiler_params=_cparams(("arbitrary",), 32),
        name="moe_combine_final",
    )(pos_km, x1, gates, final_g.reshape(1, d), y_rows)


def _moe_body(we_ref, wb_ref, wn_ref, x_ref, wg_ref, wl_ref, bg_ref, bl_ref, wd_ref, bd_ref, y_ref,
              act_s, *, unit, tf, nf):
    w = pl.program_id(0)
    s = pl.program_id(1)
    n_u = wn_ref[w]
    rb = y_ref.shape[0]

    def rows_at(off, n):
        return pl.ds(pl.multiple_of(off, unit), n)

    def for_each_chunk(fn):
        n4 = lax.shift_right_logical(n_u, 2)

        def big(i, carry):
            fn(i * (4 * unit), 4 * unit)
            return carry

        lax.fori_loop(0, n4, big, 0)
        off2 = n4 * (4 * unit)

        @pl.when(jnp.bitwise_and(n_u, 2) != 0)
        def _():
            fn(off2, 2 * unit)

        @pl.when(jnp.bitwise_and(n_u, 1) != 0)
        def _():
            fn(off2 + jnp.bitwise_and(n_u, 2) * unit, unit)

    @pl.when((s < nf) & (n_u > 0))
    def _():
        def chunk(off, n):
            rows = rows_at(off, n)
            xb = x_ref[rows, :]
            glu = jnp.dot(xb, wg_ref[...].astype(BF16), preferred_element_type=F32) + bg_ref[...]
            lin = jnp.dot(xb, wl_ref[...].astype(BF16), preferred_element_type=F32) + bl_ref[...]
            glu = jnp.minimum(glu, SWIGLU_LIMIT)
            lin = jnp.clip(lin, -SWIGLU_LIMIT, SWIGLU_LIMIT)
            act = glu * jax.nn.sigmoid(SWIGLU_ALPHA * glu) * (lin + 1.0)
            act_s[s, rows, :] = act.astype(BF16)

        for_each_chunk(chunk)

    @pl.when(s >= nf)
    def _():
        @pl.when(n_u > 0)
        def _():
            def chunk(off, n):
                rows = rows_at(off, n)
                a = jnp.concatenate([act_s[k, rows, :] for k in range(nf)], axis=1)
                y_ref[rows, :] = (jnp.dot(a, wd_ref[...].astype(BF16), preferred_element_type=F32)
                                  + bd_ref[...])

            for_each_chunk(chunk)

        def clear(i, carry):
            y_ref[rows_at(i * unit, unit), :] = jnp.zeros((unit, y_ref.shape[1]), F32)
            return carry

        lax.fori_loop(n_u, rb // unit, clear, 0)


def moe_experts(x_rows, work_e, work_blk, work_nsub, w_gu, b_gu, w_down, b_down):
    n_rows, d = x_rows.shape
    rb, tf, tn = MOE_RB, MOE_TF, MOE_TN
    nw = n_rows // rb
    nf = D_FF // tf
    nd = d // tn

    def down(s):
        return jnp.maximum(s - nf, 0)

    def up_w(w, s, wn):
        return jnp.where(wn[w] > 0, jnp.minimum(s, nf - 1), nf - 1)

    def down_w(w, s, wn):
        return jnp.where(wn[w] > 0, down(s), nd - 1)

    grid_spec = pltpu.PrefetchScalarGridSpec(
        num_scalar_prefetch=3,
        grid=(nw, nf + nd),
        in_specs=[
            pl.BlockSpec((rb, d), lambda w, s, we, wb, wn: (wb[w], 0)),
            pl.BlockSpec((None, d, tf), lambda w, s, we, wb, wn: (we[w], 0, up_w(w, s, wn))),
            pl.BlockSpec((None, d, tf), lambda w, s, we, wb, wn: (we[w], 0, nf + up_w(w, s, wn))),
            pl.BlockSpec((None, 1, tf), lambda w, s, we, wb, wn: (we[w], 0, up_w(w, s, wn))),
            pl.BlockSpec((None, 1, tf), lambda w, s, we, wb, wn: (we[w], 0, nf + up_w(w, s, wn))),
            pl.BlockSpec((None, D_FF, tn), lambda w, s, we, wb, wn: (we[w], 0, down_w(w, s, wn))),
            pl.BlockSpec((None, 1, tn), lambda w, s, we, wb, wn: (we[w], 0, down_w(w, s, wn))),
        ],
        out_specs=pl.BlockSpec((rb, tn), lambda w, s, we, wb, wn: (w, down(s))),
        scratch_shapes=[pltpu.VMEM((nf, rb, tf), BF16)],
    )
    return pl.pallas_call(
        functools.partial(_moe_body, unit=MOE_UNIT, tf=tf, nf=nf),
        grid_spec=grid_spec,
        out_shape=jax.ShapeDtypeStruct((n_rows, d), F32),
        compiler_params=_cparams(("arbitrary", "arbitrary"), 58),
        name="moe_experts",
    )(work_e, work_blk, work_nsub, x_rows, w_gu, w_gu, b_gu.reshape(N_EXPERTS, 1, 2 * D_FF),
      b_gu.reshape(N_EXPERTS, 1, 2 * D_FF), w_down, b_down.reshape(N_EXPERTS, 1, d))


def _routing(top_idx):
    n_tok = top_idx.shape[0]
    n_assign = n_tok * TOP_K
    rb, sb, unit = MOE_RB, MOE_SB, MOE_UNIT
    spw = rb // sb
    nw = n_assign // rb + N_EXPERTS
    flat_e = top_idx.reshape(-1).astype(jnp.int32)
    cb = 256
    assert n_assign % cb == 0
    onehot = (flat_e[:, None] == jnp.arange(N_EXPERTS, dtype=jnp.int32)[None, :]).astype(F32)
    tri = jnp.tril(jnp.ones((cb, cb), F32))
    within = jnp.einsum("ij,bjk->bik", tri, onehot.reshape(n_assign // cb, cb, N_EXPERTS))
    bsum = within[:, -1, :]
    csum = (within + (jnp.cumsum(bsum, axis=0) - bsum)[:, None, :]).reshape(n_assign, N_EXPERTS).astype(jnp.int32)
    rank = jnp.take_along_axis(csum, flat_e[:, None], axis=1)[:, 0] - 1
    counts = csum[-1]
    nwe = (counts + rb - 1) // rb
    w_end = jnp.cumsum(nwe)
    w_start = w_end - nwe
    dest = w_start[flat_e] * rb + rank
    widx = jnp.arange(nw, dtype=jnp.int32)
    n_used = w_end[-1]
    valid = widx < n_used
    we = jnp.minimum(jnp.searchsorted(w_end, widx, side="right"), N_EXPERTS - 1).astype(jnp.int32)
    rows_valid = jnp.clip(counts[we] - (widx - w_start[we]) * rb, 0, rb)
    n_units = jnp.where(valid, (rows_valid + unit - 1) // unit, 0).astype(jnp.int32)
    last = jnp.maximum(n_used - 1, 0)
    work_e = jnp.where(valid, we, we[last]).astype(jnp.int32)
    work_blk = jnp.where(valid, widx, last).astype(jnp.int32)
    jblk = jnp.arange(spw, dtype=jnp.int32)[None, :]
    sub_valid = (jblk * (sb // unit) < n_units[:, None]).astype(jnp.int32).reshape(-1)
    a_start = jnp.cumsum(counts) - counts
    first_w = a_start[we] + (widx - w_start[we]) * rb
    sub_first = jnp.clip(first_w[:, None] + jblk * sb, 0, n_assign).astype(jnp.int32).reshape(-1)
    sorted_a = jnp.argsort(flat_e, stable=True).astype(jnp.int32)
    sorted_tok = jnp.concatenate([sorted_a // TOP_K, jnp.zeros((sb,), jnp.int32)])
    return dest, work_e, work_blk, n_units, nw, sub_valid, sub_first, sorted_tok


def _layer_group(x2d, tm, w_in_bf, p, act_dtype):
    h = rmsnorm_to(x2d, p["norm1_g"], tm, BF16)
    (u,) = in_proj(h, w_in_bf, 0, 1, "gelu", tm, [act_dtype])
    (v,) = in_proj(h, w_in_bf, 1, 1, "gelu_ln", tm, [F32],
                   extra=(p["sgu_ln_g"].reshape(1, -1), p["sgu_ln_b"].reshape(1, -1)))
    (q,) = in_proj(h, w_in_bf, 2, 1, "scale", tm, [act_dtype])
    k32, kbf = in_proj(h, w_in_bf, 3, 1, "dual", tm, [F32, BF16])
    va32, vabf = in_proj(h, w_in_bf, 4, 1, "dual", tm, [F32, BF16])
    (gates,) = in_proj(h, w_in_bf, 5, 4, "sigmoid", tm, [BF16])
    return u, v, q, k32, kbf, va32, vabf, gates


def kernel(x_prompt, x_sample, cache_k, cache_v, page_table, norm1_g, w_in, sgu_ln_g, sgu_ln_b, sgu_w, sgu_b,
           lambda_q1, lambda_k1, lambda_q2, lambda_k2, subln_g, w_branch_a, w_branch_b, w_out, norm2_g,
           router_w, router_b, expert_w_gu, expert_b_gu, expert_w_down, expert_b_down, final_norm_g):
    depth = w_in.shape[0]
    assert depth == 1
    l = 0
    bp, sp, d = x_prompt.shape
    bs, ts, _ = x_sample.shape
    lam_init = 0.8 - 0.6 * math.exp(-0.3 * l)
    slopes2 = jnp.exp2(-(8.0 / N_HEADS) * jnp.arange(1, N_HEADS + 1, dtype=F32)) * LOG2E
    lam_vecs = jnp.stack([lambda_q1[l], lambda_k1[l], lambda_q2[l], lambda_k2[l]]).astype(F32)

    w_in_bf = w_in[l]
    w_pa = w_branch_a[l].astype(BF16)
    w_pb = w_branch_b[l].astype(BF16)
    w_o = w_out[l].astype(BF16)
    rw32 = jnp.pad(router_w[l].astype(F32), ((0, 0), (0, LANES - N_EXPERTS)))
    rw_hi = rw32.astype(BF16)
    rw_pad = jnp.concatenate([rw_hi, (rw32 - rw_hi.astype(F32)).astype(BF16)], axis=1)
    rb_pad = jnp.pad(router_b[l].astype(F32), (0, LANES - N_EXPERTS)).reshape(1, LANES)
    p = {"norm1_g": norm1_g[l], "sgu_ln_g": sgu_ln_g[l], "sgu_ln_b": sgu_ln_b[l]}

    xp = x_prompt.reshape(bp * sp, d)
    u, v, q, k32p, kbf, va32p, vabf, gates_p = _layer_group(xp, PROJ_TM, w_in_bf, p, BF16)
    ya_p = sgu(u, v, sgu_w[l], sgu_b[l], CHUNK, 4, BF16)
    yb_p = prompt_attention(q, kbf, vabf, lam_vecs, subln_g[l], slopes2, bp, sp, lam_init)

    xs = x_sample.reshape(bs * ts, d)
    u, v_s, q, k32s, _, va32s, _, gates_s = _layer_group(xs, bs * ts, w_in_bf, p, F32)
    ya_s = sgu(u, v_s, sgu_w[l], sgu_b[l], ts, bs, F32)
    k_sample = k32s.reshape(bs, ts, N_HEADS, 2 * DK)
    v_sample = va32s.reshape(bs, ts, N_HEADS, DV)
    yb_s = decode_attention(q, k_sample, v_sample, cache_k, cache_v, l, page_table,
                            lam_vecs, subln_g[l], slopes2, lam_init)

    x1, h2, top_idx, gate_w = merge(xp, ya_p, yb_p, gates_p, xs, ya_s, yb_s, gates_s, w_pa, w_pb, w_o, norm2_g[l],
                                    rw_pad, rb_pad, MERGE_TM)
    dest, work_e, work_blk, work_units, nw, sub_valid, sub_first, sorted_tok = _routing(top_idx[:, :TOP_K])
    x_rows = moe_dispatch(h2, sub_valid, sub_first, sorted_tok, nw * MOE_RB)
    y_rows = moe_experts(x_rows, work_e, work_blk, work_units, expert_w_gu[l], expert_b_gu[l],
                         expert_w_down[l], expert_b_down[l])
    n_p = bp * sp
    pos = dest.reshape(-1, TOP_K)
    y_prompt = moe_combine_final(x1, gate_w, pos, 0, n_p, y_rows, final_norm_g).reshape(bp, sp, d)
    y_sample = moe_combine_final(x1, gate_w, pos, n_p, bs * ts, y_rows, final_norm_g).reshape(bs, ts, d)

    k_prompt = k32p.reshape(1, bp, sp, N_HEADS, 2 * DK)
    v_prompt = va32p.reshape(1, bp, sp, N_HEADS, DV)
    state_sgu_v = v_s.reshape(1, bs, ts, W_SGU)
    return (y_prompt, y_sample, k_prompt, v_prompt, k_sample[None], v_sample[None], state_sgu_v)
```
